```python
import math
import jax, jax.numpy as jnp
from jax import lax
import numpy as np

D_MODEL = 2048
BATCH = 2
SEQ = 8192
DEPTH = 2

N_EVEN = (DEPTH + 1) // 2
N_ODD = DEPTH // 2
EPS = 1e-6

RWKV_HEADS = 16
RWKV_HEAD_DIM = 64
RWKV_WIDTH = RWKV_HEADS * RWKV_HEAD_DIM
RWKV_DECAY_LORA = 96
RWKV_AAA_LORA = 96
RWKV_GATE_LORA = 256
RWKV_GN_EPS = 6.4e-4
RWKV_STREAMS = (RWKV_WIDTH, RWKV_WIDTH, RWKV_WIDTH, RWKV_DECAY_LORA, RWKV_AAA_LORA, RWKV_GATE_LORA)
RWKV_COLS = sum(RWKV_STREAMS)

DSA_HEADS = 8
DSA_HEAD_DIM = 128
DSA_WIDTH = DSA_HEADS * DSA_HEAD_DIM
DSA_LATENT = 256
IDX_HEADS = 16
IDX_HEAD_DIM = 64
IDX_TOPK_MAX = 256
Q_BLOCK = 128
DSA_STREAMS = (DSA_WIDTH, DSA_LATENT, IDX_HEADS * IDX_HEAD_DIM, IDX_HEAD_DIM, IDX_HEADS)
AB_IN_COLS = RWKV_COLS + sum(DSA_STREAMS)
AB_OUT_COLS = RWKV_WIDTH + DSA_WIDTH

REL_BUCKETS = 32
REL_MAX_DIST = 128

GDN_QK_HEADS = 16
GDN_V_HEADS = 32
GDN_HEAD_DIM = 128
GDN_KEY_WIDTH = GDN_QK_HEADS * GDN_HEAD_DIM
GDN_VALUE_WIDTH = GDN_V_HEADS * GDN_HEAD_DIM
GDN_CONV_WIDTH = 4
GDN_CHUNK = 64
GDN_CONV_CH = 2 * GDN_KEY_WIDTH + GDN_VALUE_WIDTH
GDN_STREAMS = (GDN_CONV_CH, GDN_VALUE_WIDTH, GDN_V_HEADS, GDN_V_HEADS)
GDN_IN_COLS = sum(GDN_STREAMS)

FFN_DENSE = 5632
N_EXPERTS = 8
TOP_K = 2
FFN_EXPERT = 7168
MOE_BLOCK = 512
PLE_DIM = 256

kernel_name = 'hybrid_rwkv7_dsa_gdn_moe_trunk'


def split_cols(z, widths):
    cuts = [int(c) for c in np.cumsum(widths)[:-1]]
    return jnp.split(z, cuts, axis=-1)


def rmsnorm(x, g, eps=EPS):
    xf = x.astype(jnp.float32)
    y = xf * lax.rsqrt(jnp.mean(jnp.square(xf), axis=-1, keepdims=True) + eps)
    return (y * g.astype(jnp.float32)).astype(x.dtype)


def l2norm(x, eps=1e-6):
    return x * lax.rsqrt(jnp.sum(jnp.square(x), axis=-1, keepdims=True) + eps)


def token_shift(z):
    return jnp.pad(z, ((0, 0), (1, 0), (0, 0)))[:, :-1]


def swiglu(h, w_gate, w_up, w_down):
    return (jax.nn.silu(h @ w_gate) * (h @ w_up)) @ w_down


def t5_bucket(dist):
    n = jnp.maximum(dist, 0)
    exact = REL_BUCKETS // 2
    log_ratio = jnp.log(jnp.maximum(n, 1).astype(jnp.float32) / exact) / math.log(REL_MAX_DIST / exact)
    large = exact + (log_ratio * (REL_BUCKETS - exact)).astype(jnp.int32)
    return jnp.where(n < exact, n, jnp.minimum(large, REL_BUCKETS - 1))


def causal_depthwise_conv(x, w):
    k, c = w.shape
    return lax.conv_general_dilated(x, w[:, None, :], window_strides=(1,), padding=[(k - 1, 0)],
                                    dimension_numbers=('NWC', 'WIO', 'NWC'), feature_group_count=c)


def rwkv7_time_mix(z, mu, w0, w_up, a0, a_up, g_up, k_k, k_a, r_k, gn_w, gn_b):
    B, L, _ = z.shape
    H, N = RWKV_HEADS, RWKV_HEAD_DIM
    z = z.astype(jnp.float32)
    z = z + (token_shift(z) - z) * mu
    r, k, v, wl, al, gl = split_cols(z, RWKV_STREAMS)
    w = -jax.nn.softplus(-(w0 + jnp.tanh(wl) @ w_up)) - 0.5
    decay = jnp.exp(-jnp.exp(w))
    a = jax.nn.sigmoid(a0 + al @ a_up)
    g = jax.nn.sigmoid(gl) @ g_up
    heads = lambda t: t.reshape(B, L, H, N)
    kk = l2norm(heads(k * k_k))
    k = k * (1.0 + (a - 1.0) * k_a)
    r, k, v, a, decay = heads(r), heads(k), heads(v), heads(a), heads(decay)

    def step(state, inp):
        r_t, w_t, k_t, v_t, a_t, b_t = inp
        sa = jnp.einsum('bhvk,bhk->bhv', state, a_t)
        state = state * w_t[:, :, None, :] + sa[..., None] * b_t[:, :, None, :] + v_t[..., None] * k_t[:, :, None, :]
        return state, jnp.einsum('bhvk,bhk->bhv', state, r_t)

    xs = tuple(jnp.swapaxes(t, 0, 1) for t in (r, decay, k, v, -kk, kk * a))
    _, y = lax.scan(step, jnp.zeros((B, H, N, N), jnp.float32), xs)
    y = jnp.swapaxes(y, 0, 1)
    mean = jnp.mean(y, axis=-1, keepdims=True)
    var = jnp.mean(jnp.square(y - mean), axis=-1, keepdims=True)
    y = (y - mean) * lax.rsqrt(var + RWKV_GN_EPS) * gn_w.reshape(H, N) + gn_b.reshape(H, N)
    bonus = jnp.sum(r * k * r_k, axis=-1, keepdims=True) * v
    return (y + bonus).reshape(B, L, RWKV_WIDTH) * g


def dsa_attention(zq, zc, zqi, zki, zwi, ckv_norm, w_uk, w_uv, rel_bias):
    B, L, _ = zq.shape
    f32 = jnp.float32
    q = zq.astype(f32).reshape(B, L, DSA_HEADS, DSA_HEAD_DIM)
    c = rmsnorm(zc.astype(f32), ckv_norm)
    q_lat = jnp.einsum('bshd,hdc->bshc', q, w_uk.astype(f32)) * DSA_HEAD_DIM ** -0.5
    qi = zqi.astype(f32).reshape(B, L, IDX_HEADS, IDX_HEAD_DIM) * IDX_HEAD_DIM ** -0.5
    ki = zki.astype(f32)
    wi = zwi.astype(f32) * IDX_HEADS ** -0.5
    bias_table = rel_bias.astype(f32)
    topk = min(IDX_TOPK_MAX, L // 4)
    n_blk = L // Q_BLOCK
    key_pos = jnp.arange(L)

    def blocks(t):
        return jnp.moveaxis(t.reshape(B, n_blk, Q_BLOCK, *t.shape[2:]), 1, 0)

    def attend_block(args):
        ql, qib, wib, tq = args
        s = jax.nn.relu(jnp.einsum('bqhd,bsd->bqhs', qib, ki))
        score = jnp.einsum('bqhs,bqh->bqs', s, wib)
        score = jnp.where((key_pos[None, :] <= tq[:, None])[None], score, -jnp.inf)
        _, idx = lax.top_k(score, topk)
        c_sel = jax.vmap(lambda cb, ib: cb[ib])(c, idx)
        bias = jnp.moveaxis(bias_table[t5_bucket(tq[None, :, None] - idx)], -1, 2)
        logits = jnp.einsum('bqhc,bqkc->bqhk', ql, c_sel) + bias
        logits = jnp.where((idx <= tq[None, :, None])[:, :, None, :], logits, -jnp.inf)
        prob = jax.nn.softmax(logits, axis=-1)
        return jnp.einsum('bqhk,bqkc->bqhc', prob, c_sel)

    o_lat = lax.map(attend_block, (blocks(q_lat), blocks(qi), blocks(wi), key_pos.reshape(n_blk, Q_BLOCK)))
    o_lat = jnp.moveaxis(o_lat, 0, 1).reshape(B, L, DSA_HEADS, DSA_LATENT)
    return jnp.einsum('bshc,hcd->bshd', o_lat, w_uv.astype(f32)).reshape(B, L, DSA_WIDTH)


def ab_mixer(h, w_in, mu, w0, w_up, a0, a_up, g_up, k_k, k_a, r_k, gn_w, gn_b,
             ckv_norm, w_uk, w_uv, w_out, rel_bias):
    z = h @ w_in
    z_rwkv, zq, zc, zqi, zki, zwi = split_cols(z, (RWKV_COLS,) + DSA_STREAMS)
    y_a = rwkv7_time_mix(z_rwkv, mu, w0, w_up, a0, a_up, g_up, k_k, k_a, r_k, gn_w, gn_b)
    y_b = dsa_attention(zq, zc, zqi, zki, zwi, ckv_norm, w_uk, w_uv, rel_bias)
    y = jnp.concatenate([y_a, y_b], axis=-1).astype(h.dtype)
    return y @ w_out


def chunk_gated_delta_rule(q, k, v, g, beta):
    B, L, H, Dk = q.shape
    Dv = v.shape[-1]
    C = GDN_CHUNK
    N = L // C
    to_chunks = lambda t: jnp.moveaxis(t.reshape(B, N, C, H, *t.shape[3:]), 3, 1)
    q, k, v, g, beta = to_chunks(q), to_chunks(k), to_chunks(v), to_chunks(g), to_chunks(beta)
    G = jnp.cumsum(g, axis=-1)
    causal = jnp.tril(jnp.ones((C, C), dtype=bool))
    strict = jnp.tril(jnp.ones((C, C), dtype=bool), -1)
    decay = jnp.exp(jnp.where(causal, G[..., :, None] - G[..., None, :], -jnp.inf))
    k_beta = k * beta[..., None]
    m = jnp.where(strict, jnp.einsum('bhnid,bhnjd->bhnij', k_beta, k) * decay, 0.0)
    rhs = jnp.concatenate([v * beta[..., None], k_beta * jnp.exp(G)[..., None]], axis=-1)
    sol = lax.linalg.triangular_solve(m + jnp.eye(C, dtype=m.dtype), rhs, left_side=True,
                                      lower=True, unit_diagonal=True)
    u, w = sol[..., :Dv], sol[..., Dv:]
    attn = jnp.where(causal, jnp.einsum('bhnid,bhnjd->bhnij', q, k) * decay, 0.0)
    q_dec = q * jnp.exp(G)[..., None]
    k_dec = k * jnp.exp(G[..., -1:] - G)[..., None]
    chunk_decay = jnp.exp(G[..., -1])

    def step(state, inp):
        u_n, w_n, attn_n, q_n, k_n, d_n = inp
        v_new = u_n - jnp.einsum('bhck,bhkv->bhcv', w_n, state)
        o_n = jnp.einsum('bhck,bhkv->bhcv', q_n, state) + jnp.einsum('bhij,bhjv->bhiv', attn_n, v_new)
        state = state * d_n[..., None, None] + jnp.einsum('bhck,bhcv->bhkv', k_n, v_new)
        return state, o_n

    xs = tuple(jnp.moveaxis(t, 2, 0) for t in (u, w, attn, q_dec, k_dec, chunk_decay))
    _, o = lax.scan(step, jnp.zeros((B, H, Dk, Dv), jnp.float32), xs)
    return o.transpose(1, 0, 3, 2, 4).reshape(B, L, H, Dv)


def gdn_mixer(h, w_in, conv_w, a_log, dt_bias, out_norm, w_out):
    B, L, _ = h.shape
    f32 = jnp.float32
    z = h @ w_in
    qkv, zg, b, a = split_cols(z, GDN_STREAMS)
    qkv = jax.nn.silu(causal_depthwise_conv(qkv.astype(f32), conv_w.astype(f32)))
    q, k, v = split_cols(qkv, (GDN_KEY_WIDTH, GDN_KEY_WIDTH, GDN_VALUE_WIDTH))
    rep = GDN_V_HEADS // GDN_QK_HEADS
    q = jnp.repeat(l2norm(q.reshape(B, L, GDN_QK_HEADS, GDN_HEAD_DIM)), rep, axis=2) * GDN_HEAD_DIM ** -0.5
    k = jnp.repeat(l2norm(k.reshape(B, L, GDN_QK_HEADS, GDN_HEAD_DIM)), rep, axis=2)
    v = v.reshape(B, L, GDN_V_HEADS, GDN_HEAD_DIM)
    beta = jax.nn.sigmoid(b.astype(f32))
    g = -jnp.exp(a_log.astype(f32)) * jax.nn.softplus(a.astype(f32) + dt_bias.astype(f32))
    o = chunk_gated_delta_rule(q, k, v, g, beta)
    o = rmsnorm(o, out_norm) * jax.nn.silu(zg.astype(f32).reshape(B, L, GDN_V_HEADS, GDN_HEAD_DIM))
    return o.reshape(B, L, GDN_VALUE_WIDTH).astype(h.dtype) @ w_out


def moe_swiglu(h, w_router, w_gate, w_up, w_down):
    B, L, D = h.shape
    T = B * L
    hf = h.reshape(T, D)
    logits = (hf @ w_router).astype(jnp.float32)
    top_logit, top_e = lax.top_k(logits, TOP_K)
    top_w = jax.nn.softmax(top_logit, axis=-1)
    A = T * TOP_K
    flat_e = top_e.reshape(A)
    flat_tok = jnp.repeat(jnp.arange(T, dtype=jnp.int32), TOP_K)
    order = jnp.argsort(flat_e)
    se, stok, sw = flat_e[order], flat_tok[order], top_w.reshape(A)[order]
    counts = jnp.bincount(flat_e, length=N_EXPERTS)
    padded = (counts + MOE_BLOCK - 1) // MOE_BLOCK * MOE_BLOCK
    start = jnp.cumsum(counts) - counts
    pstart = jnp.cumsum(padded) - padded
    dest = pstart[se] + jnp.arange(A) - start[se]
    P = (-(-A // MOE_BLOCK) + N_EXPERTS) * MOE_BLOCK
    tok_buf = jnp.zeros((P,), jnp.int32).at[dest].set(stok)
    w_buf = jnp.zeros((P,), jnp.float32).at[dest].set(sw)
    n_blk = P // MOE_BLOCK
    blk_start = jnp.arange(n_blk) * MOE_BLOCK
    blk_e = jnp.minimum(jnp.sum(blk_start[:, None] >= jnp.cumsum(padded)[None, :], axis=1), N_EXPERTS - 1)

    def expert_block(args):
        tok, wt, e = args
        yb = swiglu(hf[tok], w_gate[e], w_up[e], w_down[e])
        return yb * wt[:, None].astype(yb.dtype)

    y = lax.map(expert_block, (tok_buf.reshape(n_blk, MOE_BLOCK), w_buf.reshape(n_blk, MOE_BLOCK), blk_e))
    out = jnp.zeros((T, D), h.dtype).at[tok_buf].add(y.reshape(P, D).astype(h.dtype))
    return out.reshape(B, L, D)


def setup_inputs(seed: int = 0) -> dict:
    key = jax.random.key(seed)
    ks = iter(jax.random.split(key, 64))
    f32 = jnp.float32
    nrm = lambda shape, scale: scale * jax.random.normal(next(ks), shape, f32)
    uni = lambda shape, lo, hi: jax.random.uniform(next(ks), shape, f32, lo, hi)
    D, NE, NO = D_MODEL, N_EVEN, N_ODD
    x = nrm((BATCH, SEQ, D), 1.0)
    p = nrm((DEPTH, BATCH, SEQ, PLE_DIM), 1.0)
    norm_mix = 1.0 + nrm((DEPTH, D), 0.02)
    norm_ffn = 1.0 + nrm((DEPTH, D), 0.02)
    ab_w_in = nrm((NE, D, AB_IN_COLS), D ** -0.5)
    ab_mu = uni((NE, RWKV_COLS), 0.0, 1.0)
    rwkv_w0 = uni((NE, RWKV_WIDTH), -6.0, 1.0)
    rwkv_w_up = nrm((NE, RWKV_DECAY_LORA, RWKV_WIDTH), 0.1 * RWKV_DECAY_LORA ** -0.5)
    rwkv_a0 = nrm((NE, RWKV_WIDTH), 0.5)
    rwkv_a_up = nrm((NE, RWKV_AAA_LORA, RWKV_WIDTH), 0.5 * RWKV_AAA_LORA ** -0.5)
    rwkv_g_up = nrm((NE, RWKV_GATE_LORA, RWKV_WIDTH), RWKV_GATE_LORA ** -0.5)
    rwkv_k_k = 0.85 + nrm((NE, RWKV_WIDTH), 0.02)
    rwkv_k_a = 1.0 + nrm((NE, RWKV_WIDTH), 0.02)
    rwkv_r_k = nrm((NE, RWKV_HEADS, RWKV_HEAD_DIM), 0.1)
    rwkv_gn_w = 1.0 + nrm((NE, RWKV_WIDTH), 0.02)
    rwkv_gn_b = nrm((NE, RWKV_WIDTH), 0.02)
    dsa_ckv_norm = 1.0 + nrm((NE, DSA_LATENT), 0.02)
    dsa_w_uk = nrm((NE, DSA_HEADS, DSA_HEAD_DIM, DSA_LATENT), DSA_HEAD_DIM ** -0.5)
    dsa_w_uv = nrm((NE, DSA_HEADS, DSA_LATENT, DSA_HEAD_DIM), DSA_LATENT ** -0.5)
    ab_w_out = nrm((NE, AB_OUT_COLS, D), AB_OUT_COLS ** -0.5)
    rel_bias = nrm((REL_BUCKETS, DSA_HEADS), 0.5)
    ffn_w_gate = nrm((NE, D, FFN_DENSE), D ** -0.5)
    ffn_w_up = nrm((NE, D, FFN_DENSE), D ** -0.5)
    ffn_w_down = nrm((NE, FFN_DENSE, D), FFN_DENSE ** -0.5)
    gdn_w_in = nrm((NO, D, GDN_IN_COLS), D ** -0.5)
    gdn_conv = nrm((NO, GDN_CONV_WIDTH, GDN_CONV_CH), GDN_CONV_WIDTH ** -0.5)
    gdn_a_log = jnp.log(uni((NO, GDN_V_HEADS), 1.0, 16.0))
    dt = jnp.exp(uni((NO, GDN_V_HEADS), math.log(1e-3), math.log(1e-1)))
    gdn_dt_bias = dt + jnp.log(-jnp.expm1(-dt))
    gdn_out_norm = 1.0 + nrm((NO, GDN_HEAD_DIM), 0.02)
    gdn_w_out = nrm((NO, GDN_VALUE_WIDTH, D), GDN_VALUE_WIDTH ** -0.5)
    moe_router = nrm((NO, D, N_EXPERTS), D ** -0.5)
    moe_w_gate = nrm((NO, N_EXPERTS, D, FFN_EXPERT), D ** -0.5)
    moe_w_up = nrm((NO, N_EXPERTS, D, FFN_EXPERT), D ** -0.5)
    moe_w_down = nrm((NO, N_EXPERTS, FFN_EXPERT, D), FFN_EXPERT ** -0.5)
    ple_norm = 1.0 + nrm((DEPTH, D), 0.02)
    ple_w_gate = nrm((DEPTH, D, D), D ** -0.5)
    ple_w_proj = nrm((DEPTH, PLE_DIM, D), PLE_DIM ** -0.5)
    final_norm = 1.0 + nrm((D,), 0.02)
    return {'x': x, 'p': p, 'norm_mix': norm_mix, 'norm_ffn': norm_ffn,
            'ab_w_in': ab_w_in, 'ab_mu': ab_mu, 'rwkv_w0': rwkv_w0, 'rwkv_w_up': rwkv_w_up,
            'rwkv_a0': rwkv_a0, 'rwkv_a_up': rwkv_a_up, 'rwkv_g_up': rwkv_g_up, 'rwkv_k_k': rwkv_k_k,
            'rwkv_k_a': rwkv_k_a, 'rwkv_r_k': rwkv_r_k, 'rwkv_gn_w': rwkv_gn_w, 'rwkv_gn_b': rwkv_gn_b,
            'dsa_ckv_norm': dsa_ckv_norm, 'dsa_w_uk': dsa_w_uk, 'dsa_w_uv': dsa_w_uv, 'ab_w_out': ab_w_out,
            'rel_bias': rel_bias, 'ffn_w_gate': ffn_w_gate, 'ffn_w_up': ffn_w_up, 'ffn_w_down': ffn_w_down,
            'gdn_w_in': gdn_w_in, 'gdn_conv': gdn_conv, 'gdn_a_log': gdn_a_log, 'gdn_dt_bias': gdn_dt_bias,
            'gdn_out_norm': gdn_out_norm, 'gdn_w_out': gdn_w_out, 'moe_router': moe_router,
            'moe_w_gate': moe_w_gate, 'moe_w_up': moe_w_up, 'moe_w_down': moe_w_down,
            'ple_norm': ple_norm, 'ple_w_gate': ple_w_gate, 'ple_w_proj': ple_w_proj, 'final_norm': final_norm}


def reference(x, p, norm_mix, norm_ffn, ab_w_in, ab_mu, rwkv_w0, rwkv_w_up, rwkv_a0, rwkv_a_up,
              rwkv_g_up, rwkv_k_k, rwkv_k_a, rwkv_r_k, rwkv_gn_w, rwkv_gn_b, dsa_ckv_norm, dsa_w_uk,
              dsa_w_uv, ab_w_out, rel_bias, ffn_w_gate, ffn_w_up, ffn_w_down, gdn_w_in, gdn_conv,
              gdn_a_log, gdn_dt_bias, gdn_out_norm, gdn_w_out, moe_router, moe_w_gate, moe_w_up,
              moe_w_down, ple_norm, ple_w_gate, ple_w_proj, final_norm):
    for i in range(DEPTH):
        j = i // 2
        hn = rmsnorm(x, norm_mix[i])
        if i % 2 == 0:
            x = x + ab_mixer(hn, ab_w_in[j], ab_mu[j], rwkv_w0[j], rwkv_w_up[j], rwkv_a0[j], rwkv_a_up[j],
                             rwkv_g_up[j], rwkv_k_k[j], rwkv_k_a[j], rwkv_r_k[j], rwkv_gn_w[j], rwkv_gn_b[j],
                             dsa_ckv_norm[j], dsa_w_uk[j], dsa_w_uv[j], ab_w_out[j], rel_bias).astype(x.dtype)
            hn = rmsnorm(x, norm_ffn[i])
            x = x + swiglu(hn, ffn_w_gate[j], ffn_w_up[j], ffn_w_down[j])
        else:
            x = x + gdn_mixer(hn, gdn_w_in[j], gdn_conv[j], gdn_a_log[j], gdn_dt_bias[j],
                              gdn_out_norm[j], gdn_w_out[j]).astype(x.dtype)
            hn = rmsnorm(x, norm_ffn[i])
            x = x + moe_swiglu(hn, moe_router[j], moe_w_gate[j], moe_w_up[j], moe_w_down[j])
        gate = jax.nn.sigmoid(rmsnorm(x, ple_norm[i]) @ ple_w_gate[i])
        x = x + gate * (p[i] @ ple_w_proj[i])
    return rmsnorm(x, final_norm)
```

```python
import functools
import math

import numpy as np
import jax
import jax.numpy as jnp
from jax import lax
from jax.experimental import pallas as pl
from jax.experimental.pallas import tpu as pltpu

F32 = jnp.float32
BF16 = jnp.bfloat16
I32 = jnp.int32
HI = lax.Precision.HIGHEST

EPS = 1e-6
LANES = 128
MIB = 1024 * 1024

RWKV_H, RWKV_N = 16, 64
RWKV_W = RWKV_H * RWKV_N
RWKV_LORA = 96
RWKV_GATE = 256
RWKV_GN_EPS = 6.4e-4
RWKV_CHUNK = 64
RWKV_TB = 256

DSA_H, DSA_D, DSA_C = 8, 128, 256
IDX_H, IDX_D = 16, 64
TOPK_MAX = 256
QB = 128
REL_BUCKETS, REL_MAX_DIST = 32, 128
NEG = -1e30
INT_MIN = -(2 ** 31)

GDN_QK_H, GDN_V_H, GDN_D = 16, 32, 128
GDN_KW = GDN_QK_H * GDN_D
GDN_VW = GDN_V_H * GDN_D
GDN_CHUNK = 64
GDN_TB = 256

N_EXPERTS, TOP_K = 8, 2
MOE_BLOCK = 512
PLE_DIM = 256

AB_R, AB_K, AB_V = 0, 1024, 2048
AB_ZQ, AB_ZQI = 3072, 4096
AB_GL, AB_ZC = 5120, 5376
AB_WL, AB_AL = 5632, 5760
AB_ZKI, AB_ZWI = 5888, 6016
AB_COLS = 6144
GD_Q, GD_K, GD_V, GD_ZG, GD_GATES = 0, 2048, 4096, 8192, 12288
GD_COLS = 12800


def _cp(sem, vmem_mib):
    return pltpu.CompilerParams(dimension_semantics=sem, vmem_limit_bytes=vmem_mib * MIB)


def _dot(a, b, precision=None):
    return jnp.dot(a, b, preferred_element_type=F32, precision=precision)


def _dot_nt(a, b, precision=None):
    return lax.dot_general(a, b, (((1,), (1,)), ((), ())), preferred_element_type=F32,
                           precision=precision)


def _rms(x, g):
    ms = jnp.mean(x * x, axis=-1, keepdims=True)
    return x * lax.rsqrt(ms + EPS) * g


def _sigmoid(x):
    return 1.0 / (1.0 + jnp.exp(-x))


def _silu(x):
    return x * _sigmoid(x)


def _softplus(x):
    return jnp.maximum(x, 0.0) + jnp.log1p(jnp.exp(-jnp.abs(x)))


def _iota(shape, dim):
    return lax.broadcasted_iota(I32, shape, dim)


def _norm_mm_body(x_ref, g_ref, w_ref, o_ref, xn_ref):
    @pl.when(pl.program_id(1) == 0)
    def _():
        xn_ref[...] = _rms(x_ref[...], g_ref[...]).astype(BF16)

    o_ref[...] = _dot(xn_ref[...], w_ref[...]).astype(o_ref.dtype)


def _norm_mm(x, g, w, *, tm, tn):
    T, K = x.shape
    N = w.shape[1]
    return pl.pallas_call(
        _norm_mm_body,
        grid=(T // tm, N // tn),
        in_specs=[pl.BlockSpec((tm, K), lambda i, j: (i, 0)),
                  pl.BlockSpec((1, K), lambda i, j: (0, 0)),
                  pl.BlockSpec((K, tn), lambda i, j: (0, j))],
        out_specs=pl.BlockSpec((tm, tn), lambda i, j: (i, j)),
        out_shape=jax.ShapeDtypeStruct((T, N), F32),
        scratch_shapes=[pltpu.VMEM((tm, K), BF16)],
        compiler_params=_cp(("parallel", "arbitrary"), 48),
        name="norm_mm",
    )(x, g.reshape(1, K), w)


def _mm_res_body(a_ref, w_ref, r_ref, o_ref):
    o_ref[...] = r_ref[...] + _dot(a_ref[...], w_ref[...])


def _mm_res(a, w, res, *, tm, tn):
    T, K = a.shape
    N = w.shape[1]
    return pl.pallas_call(
        _mm_res_body,
        grid=(T // tm, N // tn),
        in_specs=[pl.BlockSpec((tm, K), lambda i, j: (i, 0)),
                  pl.BlockSpec((K, tn), lambda i, j: (0, j)),
                  pl.BlockSpec((tm, tn), lambda i, j: (i, j))],
        out_specs=pl.BlockSpec((tm, tn), lambda i, j: (i, j)),
        out_shape=jax.ShapeDtypeStruct((T, N), F32),
        compiler_params=_cp(("parallel", "arbitrary"), 48),
        name="mm_res",
    )(a, w, res)


def _ple_body(x_ref, g_ref, wg_ref, p_ref, wp_ref, fg_ref, o_ref, *, final):
    x = x_ref[...]
    gate = _sigmoid(_dot(_rms(x, g_ref[...]).astype(BF16), wg_ref[...]))
    y = x + gate * _dot(p_ref[...].astype(BF16), wp_ref[...])
    if final:
        y = _rms(y, fg_ref[...])
    o_ref[...] = y


def _ple(x, g, wg, p, wp, fg, *, final, tm):
    T, D = x.shape
    P = p.shape[1]
    return pl.pallas_call(
        functools.partial(_ple_body, final=final),
        grid=(T // tm,),
        in_specs=[pl.BlockSpec((tm, D), lambda i: (i, 0)),
                  pl.BlockSpec((1, D), lambda i: (0, 0)),
                  pl.BlockSpec((D, D), lambda i: (0, 0)),
                  pl.BlockSpec((tm, P), lambda i: (i, 0)),
                  pl.BlockSpec((P, D), lambda i: (0, 0)),
                  pl.BlockSpec((1, D), lambda i: (0, 0))],
        out_specs=pl.BlockSpec((tm, D), lambda i: (i, 0)),
        out_shape=jax.ShapeDtypeStruct((T, D), F32),
        compiler_params=_cp(("parallel",), 48),
        name="ple",
    )(x, g.reshape(1, D), wg, p, wp, fg.reshape(1, D))


def _ffn_body(x_ref, g_ref, wg_ref, wu_ref, wd_ref, o_ref, xn_ref):
    @pl.when(pl.program_id(1) == 0)
    def _():
        x = x_ref[...]
        xn_ref[...] = _rms(x, g_ref[...]).astype(BF16)
        o_ref[...] = x

    xn = xn_ref[...]
    h = (_silu(_dot(xn, wg_ref[...])) * _dot(xn, wu_ref[...])).astype(BF16)
    o_ref[...] += _dot(h, wd_ref[...])


def _ffn(x, g, wg, wu, wd, *, tm, tf):
    T, D = x.shape
    Fh = wg.shape[1]
    return pl.pallas_call(
        _ffn_body,
        grid=(T // tm, Fh // tf),
        in_specs=[pl.BlockSpec((tm, D), lambda i, f: (i, 0)),
                  pl.BlockSpec((1, D), lambda i, f: (0, 0)),
                  pl.BlockSpec((D, tf), lambda i, f: (0, f)),
                  pl.BlockSpec((D, tf), lambda i, f: (0, f)),
                  pl.BlockSpec((tf, D), lambda i, f: (f, 0))],
        out_specs=pl.BlockSpec((tm, D), lambda i, f: (i, 0)),
        out_shape=jax.ShapeDtypeStruct((T, D), F32),
        scratch_shapes=[pltpu.VMEM((tm, D), BF16)],
        compiler_params=_cp(("parallel", "arbitrary"), 48),
        name="ffn",
    )(x, g.reshape(1, D), wg, wu, wd)


def _shift_mix(x, prev_row, mu):
    xs = pltpu.roll(x, 1, axis=0)
    xs = jnp.where(_iota(x.shape, 0) == 0, prev_row, xs)
    return x + (xs - x) * mu


def _bd(x, lo):
    return jnp.concatenate([jnp.where(lo, x, 0.0), jnp.where(lo, 0.0, x)], axis=0)


def _unit_lower_inverse(x, eye, steps):
    p = eye + x
    xp = x
    for _ in range(steps):
        xp = _dot(xp, xp, HI)
        p = _dot(p, eye + xp, HI)
    return p


def _rwkv_body(rkv_ref, gl_ref, wa_ref, mu_rkv_ref, mu_gl_ref, mu_wa_ref, w0_ref, wup_ref,
               a0_ref, aup_ref, gup_ref, kk_ref, ka_ref, rk_ref, gnw_ref, gnb_ref, o_ref,
               prev_rkv, prev_gl, prev_wa, state, r_s, k_s, v_s, kn_s, a_s, lw_s, cg_s, g_s, y_s):
    TB = rkv_ref.shape[0]
    C = RWKV_CHUNK
    NP = RWKV_W // LANES

    @pl.when(pl.program_id(1) == 0)
    def _():
        prev_rkv[...] = jnp.zeros_like(prev_rkv)
        prev_gl[...] = jnp.zeros_like(prev_gl)
        prev_wa[...] = jnp.zeros_like(prev_wa)
        state[...] = jnp.zeros_like(state)

    lane = _iota((1, LANES), 1)
    lo = lane < RWKV_N
    r128 = _iota((LANES, LANES), 0)
    c128 = _iota((LANES, LANES), 1)
    same = (r128 // C) == (c128 // C)
    seg_ones = jnp.where(same, 1.0, 0.0).astype(F32)
    eye = jnp.where(r128 == c128, 1.0, 0.0).astype(F32)
    strict = same & (c128 < r128)
    incl = same & (c128 <= r128)
    rt = _iota((TB, TB), 0)
    ct = _iota((TB, TB), 1)
    tri_chunks = jnp.where(((rt // C) == (ct // C)) & (ct <= rt), 1.0, 0.0).astype(F32)

    wa_raw = wa_ref[...]
    wa = _shift_mix(wa_raw, prev_wa[...], mu_wa_ref[...])
    prev_wa[...] = wa_raw[TB - 1:TB, :]
    wl = jnp.tanh(wa[:, :LANES]).astype(BF16)
    al = wa[:, LANES:].astype(BF16)
    w = -_softplus(-(w0_ref[...] + _dot(wl, wup_ref[...]))) - 0.5
    lw = -jnp.exp(w)
    lw_s[...] = lw
    cg_s[...] = _dot(tri_chunks, lw, HI)
    a_s[...] = _sigmoid(a0_ref[...] + _dot(al, aup_ref[...]))
    gl_raw = gl_ref[...]
    gl = _shift_mix(gl_raw, prev_gl[...], mu_gl_ref[...])
    prev_gl[...] = gl_raw[TB - 1:TB, :]
    g_s[...] = _dot(_sigmoid(gl).astype(BF16), gup_ref[...])

    for p in range(NP):
        cs = slice(p * LANES, (p + 1) * LANES)
        cols = [slice(off + p * LANES, off + (p + 1) * LANES) for off in (AB_R, AB_K, AB_V)]
        mixed = []
        for c in cols:
            raw = rkv_ref[:, c]
            mixed.append(_shift_mix(raw, prev_rkv[:, c], mu_rkv_ref[:, c]))
            prev_rkv[:, c] = raw[TB - 1:TB, :]
        r, k, v = mixed
        kk = k * kk_ref[:, cs]
        kn_s[:, cs] = kk * lax.rsqrt(_dot(kk * kk, seg_ones, HI) + 1e-6)
        r_s[:, cs] = r
        k_s[:, cs] = k * (1.0 + (a_s[:, cs] - 1.0) * ka_ref[:, cs])
        v_s[:, cs] = v

    def chunk(c, carry):
        rows = pl.ds(pl.multiple_of(c * C, C), C)
        for p in range(NP):
            cs = slice(p * LANES, (p + 1) * LANES)
            cg = cg_s[rows, cs]
            gam = jnp.exp(cg)
            gam_inv = jnp.exp(-cg)
            gam_prev = jnp.exp(cg - lw_s[rows, cs])
            gam_last = gam[C - 1:C, :]
            kn = kn_s[rows, cs]
            a_t = _bd(-kn * gam_prev, lo).astype(BF16)
            b_raw = kn * a_s[rows, cs] * gam_inv
            k_raw = k_s[rows, cs] * gam_inv
            b_t = _bd(b_raw, lo).astype(BF16)
            k_t = _bd(k_raw, lo).astype(BF16)
            r_t = _bd(r_s[rows, cs] * gam, lo).astype(BF16)
            v_f = _bd(v_s[rows, cs], lo)
            v_t = v_f.astype(BF16)
            a_ab = jnp.where(strict, _dot_nt(a_t, b_t), 0.0)
            a_ak = jnp.where(strict, _dot_nt(a_t, k_t), 0.0)
            r_ab = jnp.where(incl, _dot_nt(r_t, b_t), 0.0)
            r_ak = jnp.where(incl, _dot_nt(r_t, k_t), 0.0)
            t_inv = _unit_lower_inverse(a_ab, eye, 5)
            s_old = state[p]
            s_bf = s_old.astype(BF16)
            u = _dot(t_inv, _dot_nt(a_t, s_bf) + _dot(a_ak.astype(BF16), v_t), HI)
            u_bf = u.astype(BF16)
            y_bd = _dot_nt(r_t, s_bf) + _dot(r_ab.astype(BF16), u_bf) + _dot(r_ak.astype(BF16), v_t)
            y_s[rows, cs] = y_bd[:C] + y_bd[C:]
            b_end = _bd(b_raw * gam_last, lo).astype(BF16)
            k_end = _bd(k_raw * gam_last, lo).astype(BF16)
            state[p] = (s_old * gam_last + _dot(u.T.astype(BF16), b_end)
                        + _dot(v_f.T.astype(BF16), k_end))
        return carry

    lax.fori_loop(0, TB // C, chunk, 0)

    for p in range(NP):
        cs = slice(p * LANES, (p + 1) * LANES)
        y = y_s[:, cs]
        mean = _dot(y, seg_ones, HI) * (1.0 / RWKV_N)
        d = y - mean
        var = _dot(d * d, seg_ones, HI) * (1.0 / RWKV_N)
        yn = d * lax.rsqrt(var + RWKV_GN_EPS) * gnw_ref[:, cs] + gnb_ref[:, cs]
        bonus = _dot(r_s[:, cs] * k_s[:, cs] * rk_ref[:, cs], seg_ones, HI) * v_s[:, cs]
        o_ref[:, cs] = ((yn + bonus) * g_s[:, cs]).astype(o_ref.dtype)


def _rwkv(z, B, L, mu, w0, w_up, a0, a_up, g_up, k_k, k_a, r_k, gn_w, gn_b):
    TB = RWKV_TB
    nb = L // TB
    W = RWKV_W
    mu_r, mu_k, mu_v, mu_wl, mu_al, mu_gl = jnp.split(
        mu, np.cumsum([W, W, W, RWKV_LORA, RWKV_LORA])[:].tolist())
    pad = LANES - RWKV_LORA
    mu_rkv = jnp.concatenate([mu_r, mu_k, mu_v]).reshape(1, 3 * W)
    mu_wa = jnp.concatenate([jnp.pad(mu_wl, (0, pad)), jnp.pad(mu_al, (0, pad))]).reshape(1, 2 * LANES)
    wup = jnp.pad(w_up, ((0, pad), (0, 0))).astype(BF16)
    aup = jnp.pad(a_up, ((0, pad), (0, 0))).astype(BF16)
    row = lambda t: t.reshape(1, W)
    vec = lambda n: pl.BlockSpec((1, n), lambda b, i: (0, 0))
    big = lambda: pltpu.VMEM((TB, W), F32)
    return pl.pallas_call(
        _rwkv_body,
        grid=(B, nb),
        in_specs=[pl.BlockSpec((TB, 3 * W), lambda b, i: (b * nb + i, 0)),
                  pl.BlockSpec((TB, RWKV_GATE), lambda b, i: (b * nb + i, AB_GL // RWKV_GATE)),
                  pl.BlockSpec((TB, 2 * LANES), lambda b, i: (b * nb + i, AB_WL // (2 * LANES))),
                  vec(3 * W), vec(RWKV_GATE), vec(2 * LANES), vec(W),
                  pl.BlockSpec((LANES, W), lambda b, i: (0, 0)),
                  vec(W),
                  pl.BlockSpec((LANES, W), lambda b, i: (0, 0)),
                  pl.BlockSpec((RWKV_GATE, W), lambda b, i: (0, 0)),
                  vec(W), vec(W), vec(W), vec(W), vec(W)],
        out_specs=pl.BlockSpec((TB, W), lambda b, i: (b * nb + i, 0)),
        out_shape=jax.ShapeDtypeStruct((B * L, W), BF16),
        scratch_shapes=[pltpu.VMEM((1, 3 * W), F32), pltpu.VMEM((1, RWKV_GATE), F32),
                        pltpu.VMEM((1, 2 * LANES), F32),
                        pltpu.VMEM((W // LANES, LANES, LANES), F32),
                        big(), big(), big(), big(), big(), big(), big(), big(), big()],
        compiler_params=_cp(("arbitrary", "arbitrary"), 48),
        name="rwkv7",
    )(z, z, z, mu_rkv, mu_gl.reshape(1, RWKV_GATE), mu_wa, row(w0), wup, row(a0), aup,
      g_up.astype(BF16), row(k_k), row(k_a), row(r_k), row(gn_w), row(gn_b))


def _t5_bucket_np(dist):
    n = np.maximum(dist, 0)
    exact = REL_BUCKETS // 2
    ratio = np.log(np.maximum(n, 1).astype(np.float32) / np.float32(exact)) / np.float32(
        math.log(REL_MAX_DIST / exact))
    large = exact + (ratio.astype(np.float32) * np.float32(REL_BUCKETS - exact)).astype(np.int32)
    return np.where(n < exact, n, np.minimum(large, REL_BUCKETS - 1)).astype(np.int32)


def _near_buckets():
    kl = np.arange(QB)[:, None]
    ql = np.arange(QB)[None, :]
    return np.stack([_t5_bucket_np(ql - kl), _t5_bucket_np(QB + ql - kl)])


def _dsa_body(tbl_ref, bkt_ref, zq_ref, zqi_ref, zc_ref, zki_ref, zwi_ref, cn_ref, wukT_ref,
              wuvT_ref, o_ref, c_all, cT_all, kibd_all, sc, qiT, qlatT, bias, m_s, l_s, oT, *, topk):
    b = pl.program_id(0)
    qb = pl.program_id(1)
    lo = _iota((1, LANES), 1) < IDX_D
    krow = _iota((QB, QB), 0)
    qcol = _iota((QB, QB), 1)
    hsl = [slice(h * QB, (h + 1) * QB) for h in range(DSA_H)]

    @pl.when((b == 0) & (qb == 0))
    def _():
        for t in range(2):
            bk = bkt_ref[t]
            for h in range(DSA_H):
                far = tbl_ref[REL_BUCKETS - 1, h]
                acc = jnp.zeros((QB, QB), F32)
                for bb in range(REL_BUCKETS - 1):
                    acc = jnp.where(bk == bb, tbl_ref[bb, h] - far, acc)
                bias[t, h] = acc

    c_new = _rms(zc_ref[...], cn_ref[...])
    c_all[qb] = c_new.astype(BF16)
    cT_all[qb] = c_new.T.astype(BF16)
    kibd_all[qb] = _bd(zki_ref[...], lo).astype(BF16)

    for p in range(IDX_H // 2):
        qiT[p] = zqi_ref[:, p * LANES:(p + 1) * LANES].T.astype(BF16)
    wT = zwi_ref[...].T * (IDX_D ** -0.5 * IDX_H ** -0.5)
    w_rows = [wT[h:h + 1, :] for h in range(IDX_H)]
    for h in range(DSA_H):
        qhT = zq_ref[:, hsl[h]].T.astype(BF16)
        qlatT[:, hsl[h]] = (_dot(wukT_ref[h], qhT) * DSA_D ** -0.5).astype(BF16)

    def score_body(j, carry):
        kb = kibd_all[j]
        acc = jnp.zeros((QB, QB), F32)
        for p in range(IDX_H // 2):
            s = _dot(kb, qiT[p])
            acc = (acc + w_rows[2 * p] * jnp.maximum(s[:QB], 0.0)
                   + w_rows[2 * p + 1] * jnp.maximum(s[QB:], 0.0))
        bits = lax.bitcast_convert_type(acc, I32)
        key = bits ^ ((bits >> 31) & 0x7FFFFFFF)
        sc[j] = jnp.where((j == qb) & (krow > qcol), INT_MIN, key)
        return carry

    lax.fori_loop(0, qb + 1, score_body, 0)

    def bis_body(i, t):
        cand = t ^ jnp.left_shift(jnp.int32(1), 31 - i)

        def cnt_body(j, a):
            return a + jnp.where(sc[j] >= cand, 1, 0)

        cnt = lax.fori_loop(0, qb + 1, cnt_body, jnp.zeros((QB, QB), I32))
        return jnp.where(jnp.sum(cnt, axis=0, keepdims=True) >= topk, cand, t)

    thr = lax.fori_loop(0, 32, bis_body, jnp.full((1, QB), INT_MIN, I32))
    thr = jnp.maximum(thr, INT_MIN + 1)

    m_s[...] = jnp.full(m_s.shape, NEG, F32)
    l_s[...] = jnp.zeros_like(l_s)
    oT[...] = jnp.zeros_like(oT)

    def attend(j, near):
        sel = sc[j] >= thr
        lg = _dot(c_all[j], qlatT[...])
        cT = cT_all[j]
        for h in range(DSA_H):
            lgh = lg[:, hsl[h]]
            if near is not None:
                lgh = lgh + bias[near, h]
            lgh = jnp.where(sel, lgh, NEG)
            m_old = m_s[:, hsl[h]]
            m_new = jnp.maximum(m_old, jnp.max(lgh, axis=0, keepdims=True))
            pr = jnp.exp(lgh - m_new)
            alpha = jnp.exp(m_old - m_new)
            l_s[:, hsl[h]] = alpha * l_s[:, hsl[h]] + jnp.sum(pr, axis=0, keepdims=True)
            m_s[:, hsl[h]] = m_new
            oT[:, hsl[h]] = alpha * oT[:, hsl[h]] + _dot(cT, pr.astype(BF16))

    def far_body(j, carry):
        attend(j, None)
        return carry

    lax.fori_loop(0, jnp.maximum(qb - 1, 0), far_body, 0)

    @pl.when(qb >= 1)
    def _():
        attend(qb - 1, 1)

    attend(qb, 0)

    inv_l = 1.0 / l_s[...]
    for h in range(DSA_H):
        oh = (oT[:, hsl[h]] * inv_l[:, hsl[h]]).astype(BF16)
        o_ref[:, hsl[h]] = _dot(wuvT_ref[h], oh).T.astype(o_ref.dtype)


def _dsa(z, B, L, ckv_norm, w_uk, w_uv, rel_bias):
    nq = L // QB
    topk = min(TOPK_MAX, L // 4)
    wukT = jnp.swapaxes(w_uk, 1, 2).astype(BF16)
    wuvT = jnp.swapaxes(w_uv, 1, 2).astype(BF16)
    blk = lambda w, off: pl.BlockSpec((QB, w), lambda b, q: (b * nq + q, off // w))
    full = lambda shape: pl.BlockSpec(shape, lambda b, q: (0,) * len(shape))
    W = DSA_H * DSA_D
    return pl.pallas_call(
        functools.partial(_dsa_body, topk=topk),
        grid=(B, nq),
        in_specs=[pl.BlockSpec(memory_space=pltpu.SMEM),
                  full((2, QB, QB)),
                  blk(W, AB_ZQ), blk(IDX_H * IDX_D, AB_ZQI), blk(DSA_C, AB_ZC),
                  blk(LANES, AB_ZKI), blk(LANES, AB_ZWI),
                  full((1, DSA_C)), full((DSA_H, DSA_C, DSA_D)), full((DSA_H, DSA_D, DSA_C))],
        out_specs=pl.BlockSpec((QB, W), lambda b, q: (b * nq + q, 0)),
        out_shape=jax.ShapeDtypeStruct((B * L, W), BF16),
        scratch_shapes=[pltpu.VMEM((nq, QB, DSA_C), BF16), pltpu.VMEM((nq, DSA_C, QB), BF16),
                        pltpu.VMEM((nq, 2 * QB, LANES), BF16), pltpu.VMEM((nq, QB, QB), I32),
                        pltpu.VMEM((IDX_H // 2, LANES, QB), BF16), pltpu.VMEM((DSA_C, W), BF16),
                        pltpu.VMEM((2, DSA_H, QB, QB), F32),
                        pltpu.VMEM((1, W), F32), pltpu.VMEM((1, W), F32), pltpu.VMEM((DSA_C, W), F32)],
        compiler_params=_cp(("arbitrary", "arbitrary"), 48),
        name="dsa",
    )(rel_bias, jnp.asarray(_near_buckets()), z, z, z, z, z, ckv_norm.reshape(1, DSA_C), wukT, wuvT)


def _gdn_gates_body(z_ref, alog_ref, dtb_ref, o_ref):
    TB = z_ref.shape[0]
    C = GDN_CHUNK
    z = z_ref[...]
    sub = _iota((1, LANES), 1) % 8
    beta = _sigmoid(z)
    g = -jnp.exp(alog_ref[...]) * _softplus(z + dtb_ref[...])
    rt = _iota((TB, TB), 0)
    ct = _iota((TB, TB), 1)
    same = (rt // C) == (ct // C)
    cum = _dot(jnp.where(same & (ct <= rt), 1.0, 0.0).astype(F32), g, HI)
    tot = _dot(jnp.where(same, 1.0, 0.0).astype(F32), g, HI)
    tile = jnp.where(sub < 2, beta, jnp.where(sub < 4, g, jnp.where(sub < 6, cum, tot)))
    o_ref[...] = tile.T


def _gdn_gates(z, B, L, a_log, dt_bias):
    TB = GDN_TB
    nb = L // TB
    spread = lambda t: jnp.zeros((GDN_QK_H, 8), F32).at[:, 2:].set(
        jnp.tile(t.reshape(GDN_QK_H, 2), (1, 3))).reshape(1, LANES)
    return pl.pallas_call(
        _gdn_gates_body,
        grid=(B, nb),
        in_specs=[pl.BlockSpec((TB, LANES), lambda b, i: (b * nb + i, GD_GATES // LANES)),
                  pl.BlockSpec((1, LANES), lambda b, i: (0, 0)),
                  pl.BlockSpec((1, LANES), lambda b, i: (0, 0))],
        out_specs=pl.BlockSpec((None, LANES, TB), lambda b, i: (b, 0, i)),
        out_shape=jax.ShapeDtypeStruct((B, LANES, L), F32),
        compiler_params=_cp(("parallel", "parallel"), 32),
        name="gdn_gates",
    )(z, spread(a_log), spread(dt_bias))


def _gdn_body(zq_ref, zk_ref, zv_ref, zg_ref, gates_ref, cwq_ref, cwk_ref, cwv_ref, on_ref, o_ref,
              xbuf, state):
    TB = zq_ref.shape[0]
    C = GDN_CHUNK
    D = GDN_D
    KW = GDN_CONV_W

    @pl.when(pl.program_id(2) == 0)
    def _():
        xbuf[0:8, :] = jnp.zeros((8, 4 * D), F32)
        state[...] = jnp.zeros_like(state)

    xbuf[8:TB + 8, 0:D] = zq_ref[...]
    xbuf[8:TB + 8, D:2 * D] = zk_ref[...]
    xbuf[8:TB + 8, 2 * D:4 * D] = zv_ref[...]
    cw = jnp.concatenate([cwq_ref[...], cwk_ref[...], cwv_ref[...]], axis=1)
    acc = jnp.zeros((TB, 4 * D), F32)
    for j in range(KW):
        acc = acc + cw[j:j + 1, :] * xbuf[8 - (KW - 1) + j:8 - (KW - 1) + j + TB, :]
    xbuf[0:8, :] = xbuf[TB:TB + 8, :]
    qkv = _silu(acc)
    l2 = lambda t: t * lax.rsqrt(jnp.sum(t * t, axis=-1, keepdims=True) + 1e-6)
    q = l2(qkv[:, 0:D]) * D ** -0.5
    k = l2(qkv[:, D:2 * D])
    v = qkv[:, 2 * D:4 * D]

    lane = _iota((1, LANES), 1)
    lo = lane < C
    r128 = _iota((LANES, LANES), 0)
    c128 = _iota((LANES, LANES), 1)
    same = (r128 // C) == (c128 // C)
    eye_m = r128 == c128
    eye = jnp.where(eye_m, 1.0, 0.0).astype(F32)
    strict = same & (c128 < r128)
    incl = same & (c128 <= r128)
    upper = jnp.where(same & (r128 > c128), 1.0, 0.0).astype(F32)

    for c in range(TB // C):
        rows = slice(c * C, (c + 1) * C)
        win = gates_ref[:, (c // 2) * LANES:(c // 2 + 1) * LANES]
        win_sw = pltpu.roll(win, C, axis=1)
        first, second = (win, win_sw) if c % 2 == 0 else (win_sw, win)
        st = lambda r: jnp.where(lo, first[r:r + 1, :], second[r + 1:r + 2, :])
        beta_st, g_st, cum_st, tot_st = st(0), st(2), st(4), st(6)
        e_st = jnp.exp(cum_st)
        fb_st = jnp.exp(tot_st - cum_st) * beta_st
        etot = jnp.exp(tot_st)
        etot_sw = pltpu.roll(etot, C, axis=1)
        etot_cat = jnp.concatenate([jnp.where(lo, etot, etot_sw), jnp.where(lo, etot_sw, etot)], axis=1)

        kc = k[rows]
        qc = q[rows]
        kk2 = jnp.concatenate([kc, kc], axis=0).astype(BF16)
        qq2 = jnp.concatenate([qc, qc], axis=0).astype(BF16)
        kk_m = _dot_nt(kk2, kk2)
        qk_m = _dot_nt(qq2, kk2)
        dec = jnp.exp(_dot(jnp.where(incl, g_st, 0.0), upper, HI))
        ab = jnp.where(strict, kk_m * dec, 0.0) * beta_st
        t_inv = _unit_lower_inverse(-ab, eye, 5)
        kq = jnp.concatenate([kc, qc], axis=0).astype(BF16)
        proj = _dot(kq, state[...].astype(BF16))
        ks_st = jnp.concatenate([proj[:C, :D], proj[:C, D:]], axis=0)
        qs_st = jnp.concatenate([proj[C:, :D], proj[C:, D:]], axis=0)
        v_st = jnp.concatenate([v[rows, :D], v[rows, D:]], axis=0)
        vn = _dot(jnp.concatenate([t_inv, -(t_inv * e_st)], axis=1).astype(BF16),
                  jnp.concatenate([v_st, ks_st], axis=0).astype(BF16))
        attn = jnp.where(incl, qk_m * dec, 0.0) * beta_st
        o_st = _dot(jnp.concatenate([eye * e_st, attn], axis=1).astype(BF16),
                    jnp.concatenate([qs_st, vn], axis=0).astype(BF16))
        fv = _dot((eye * fb_st).astype(BF16), vn.astype(BF16))
        state[...] = state[...] * etot_cat + _dot(
            kc.T.astype(BF16), jnp.concatenate([fv[:C], fv[C:]], axis=1).astype(BF16))

        for h in range(2):
            oh = o_st[h * C:(h + 1) * C]
            gate = zg_ref[rows, h * D:(h + 1) * D]
            o_ref[rows, h * D:(h + 1) * D] = (_rms(oh, on_ref[...]) * _silu(gate)).astype(o_ref.dtype)


GDN_CONV_W = 4


def _gdn(z, gates, B, L, conv_w, out_norm):
    TB = GDN_TB
    nb = L // TB
    D = GDN_D
    zspec = lambda w, off: pl.BlockSpec((TB, w), lambda b, h, i: (b * nb + i, off // w + h))
    cspec = lambda w, off: pl.BlockSpec((GDN_CONV_W, w), lambda b, h, i: (0, off // w + h))
    return pl.pallas_call(
        _gdn_body,
        grid=(B, GDN_QK_H, nb),
        in_specs=[zspec(D, GD_Q), zspec(D, GD_K), zspec(2 * D, GD_V), zspec(2 * D, GD_ZG),
                  pl.BlockSpec((None, 8, TB), lambda b, h, i: (b, h, i)),
                  cspec(D, GD_Q), cspec(D, GD_K), cspec(2 * D, GD_V),
                  pl.BlockSpec((1, D), lambda b, h, i: (0, 0))],
        out_specs=pl.BlockSpec((TB, 2 * D), lambda b, h, i: (b * nb + i, h)),
        out_shape=jax.ShapeDtypeStruct((B * L, GDN_VW), BF16),
        scratch_shapes=[pltpu.VMEM((TB + 8, 4 * D), F32), pltpu.VMEM((D, 2 * D), F32)],
        compiler_params=_cp(("parallel", "parallel", "arbitrary"), 32),
        name="gdn",
    )(z, z, z, z, gates, conv_w, conv_w, conv_w, out_norm.reshape(1, D))


def _gd_in_weight(w_in):
    qkv, zg, b, a = jnp.split(w_in, np.cumsum([2 * GDN_KW + GDN_VW, GDN_VW, GDN_V_H]).tolist(), axis=1)
    D = w_in.shape[0]
    pair = lambda t: t.reshape(D, GDN_QK_H, 2)
    gates = jnp.concatenate([pair(b), pair(a), pair(a), pair(a)], axis=2).reshape(D, LANES)
    pad = jnp.zeros((D, GD_COLS - GD_GATES - LANES), w_in.dtype)
    return jnp.concatenate([qkv, zg, gates, pad], axis=1).astype(BF16)


def _router_body(x_ref, g_ref, wr_ref, o_ref):
    logits = _dot(_rms(x_ref[...], g_ref[...]), wr_ref[...], HI)
    lane = _iota(logits.shape, 1)
    logits = jnp.where(lane < N_EXPERTS, logits, -jnp.inf)
    m1 = jnp.max(logits, axis=-1, keepdims=True)
    i1 = jnp.min(jnp.where(logits == m1, lane, LANES), axis=-1, keepdims=True)
    rest = jnp.where(lane == i1, -jnp.inf, logits)
    m2 = jnp.max(rest, axis=-1, keepdims=True)
    i2 = jnp.min(jnp.where(rest == m2, lane, LANES), axis=-1, keepdims=True)
    e = jnp.exp(m2 - m1)
    w1 = 1.0 / (1.0 + e)
    o_ref[...] = jnp.where(lane == 0, i1.astype(F32),
                           jnp.where(lane == 1, i2.astype(F32),
                                     jnp.where(lane == 2, w1, jnp.where(lane == 3, e * w1, 0.0))))


def _router(x, g, w_router, *, tm):
    T, D = x.shape
    wr = jnp.pad(w_router, ((0, 0), (0, LANES - N_EXPERTS)))
    return pl.pallas_call(
        _router_body,
        grid=(T // tm,),
        in_specs=[pl.BlockSpec((tm, D), lambda i: (i, 0)),
                  pl.BlockSpec((1, D), lambda i: (0, 0)),
                  pl.BlockSpec((D, LANES), lambda i: (0, 0))],
        out_specs=pl.BlockSpec((tm, LANES), lambda i: (i, 0)),
        out_shape=jax.ShapeDtypeStruct((T, LANES), F32),
        compiler_params=_cp(("parallel",), 32),
        name="router",
    )(x, g.reshape(1, D), wr)


def _row_copy(src_hbm, row, dst, r, sem):
    return pltpu.make_async_copy(src_hbm.at[pl.ds(row, 1), :], dst.at[pl.ds(r, 1), :], sem)


def _experts_body(tok_ref, be_ref, nu_ref, x_hbm, g_ref, wg_ref, wu_ref, wd_ref, o_ref,
                  xbuf, xn_ref, sem):
    i = pl.program_id(0)
    f = pl.program_id(1)
    MB = xbuf.shape[0]
    active = i < nu_ref[0]

    @pl.when(f == 0)
    def _():
        o_ref[...] = jnp.zeros_like(o_ref)

    @pl.when((f == 0) & active)
    def _():
        def issue(r, c):
            _row_copy(x_hbm, tok_ref[i * MB + r], xbuf, r, sem).start()
            return c

        lax.fori_loop(0, MB, issue, 0)

        def wait(r, c):
            _row_copy(x_hbm, 0, xbuf, r, sem).wait()
            return c

        lax.fori_loop(0, MB, wait, 0)
        xn_ref[...] = _rms(xbuf[...], g_ref[...]).astype(BF16)

    @pl.when(active)
    def _():
        xn = xn_ref[...]
        h = (_silu(_dot(xn, wg_ref[...])) * _dot(xn, wu_ref[...])).astype(BF16)
        o_ref[...] += _dot(h, wd_ref[...])


def _experts(x, g, tok, blk_e, n_used, wg, wu, wd, *, tf):
    T, D = x.shape
    MB = MOE_BLOCK
    n_blk = tok.shape[0] // MB
    Fh = wg.shape[2]
    fe = lambda i, f, nu: jnp.where(i < nu[0], f, 0)
    return pl.pallas_call(
        _experts_body,
        grid_spec=pltpu.PrefetchScalarGridSpec(
            num_scalar_prefetch=3,
            grid=(n_blk, Fh // tf),
            in_specs=[pl.BlockSpec(memory_space=pl.ANY),
                      pl.BlockSpec((1, D), lambda i, f, tk, be, nu: (0, 0)),
                      pl.BlockSpec((None, D, tf), lambda i, f, tk, be, nu: (be[i], 0, fe(i, f, nu))),
                      pl.BlockSpec((None, D, tf), lambda i, f, tk, be, nu: (be[i], 0, fe(i, f, nu))),
                      pl.BlockSpec((None, tf, D), lambda i, f, tk, be, nu: (be[i], fe(i, f, nu), 0))],
            out_specs=pl.BlockSpec((MB, D), lambda i, f, tk, be, nu: (i, 0)),
            scratch_shapes=[pltpu.VMEM((MB, D), F32), pltpu.VMEM((MB, D), BF16),
                            pltpu.SemaphoreType.DMA(())]),
        out_shape=jax.ShapeDtypeStruct((n_blk * MB, D), F32),
        compiler_params=_cp(("arbitrary", "arbitrary"), 48),
        name="experts",
    )(tok, blk_e, n_used, x, g.reshape(1, D), wg, wu, wd)


def _combine_body(slot_ref, y_hbm, x_ref, r_ref, o_ref, y0, y1, sem):
    i = pl.program_id(0)
    tm = x_ref.shape[0]

    def issue(r, c):
        a = (i * tm + r) * TOP_K
        _row_copy(y_hbm, slot_ref[a], y0, r, sem).start()
        _row_copy(y_hbm, slot_ref[a + 1], y1, r, sem).start()
        return c

    lax.fori_loop(0, tm, issue, 0)

    def wait(r, c):
        _row_copy(y_hbm, 0, y0, r, sem).wait()
        _row_copy(y_hbm, 0, y1, r, sem).wait()
        return c

    lax.fori_loop(0, tm, wait, 0)
    route = r_ref[...]
    o_ref[...] = x_ref[...] + y0[...] * route[:, 2:3] + y1[...] * route[:, 3:4]


def _combine(x, y, route, slots, *, tm):
    T, D = x.shape
    return pl.pallas_call(
        _combine_body,
        grid_spec=pltpu.PrefetchScalarGridSpec(
            num_scalar_prefetch=1,
            grid=(T // tm,),
            in_specs=[pl.BlockSpec(memory_space=pl.ANY),
                      pl.BlockSpec((tm, D), lambda i, s: (i, 0)),
                      pl.BlockSpec((tm, LANES), lambda i, s: (i, 0))],
            out_specs=pl.BlockSpec((tm, D), lambda i, s: (i, 0)),
            scratch_shapes=[pltpu.VMEM((tm, D), F32), pltpu.VMEM((tm, D), F32),
                            pltpu.SemaphoreType.DMA(())]),
        out_shape=jax.ShapeDtypeStruct((T, D), F32),
        compiler_params=_cp(("arbitrary",), 32),
        name="combine",
    )(slots, y, x, route)


def _moe(x, g, w_router, wg, wu, wd):
    T, D = x.shape
    MB = MOE_BLOCK
    A = T * TOP_K
    route = _router(x, g, w_router, tm=512)
    flat_e = route[:, :TOP_K].astype(I32).reshape(A)
    onehot = (flat_e[:, None] == jnp.arange(N_EXPERTS, dtype=I32)[None, :]).astype(I32)
    csum = jnp.cumsum(onehot, axis=0)
    rank = jnp.take_along_axis(csum, flat_e[:, None], axis=1)[:, 0] - 1
    padded = (csum[-1] + MB - 1) // MB * MB
    pend = jnp.cumsum(padded)
    slots = (pend - padded)[flat_e] + rank
    n_blk = A // MB + N_EXPERTS
    tok = jnp.zeros((n_blk * MB,), I32).at[slots].set(jnp.arange(A, dtype=I32) // TOP_K)
    blk_start = jnp.arange(n_blk, dtype=I32) * MB
    blk_e = jnp.minimum(jnp.sum(blk_start[:, None] >= pend[None, :], axis=1), N_EXPERTS - 1).astype(I32)
    n_used = (pend[-1:] // MB).astype(I32)
    y = _experts(x, g, tok, blk_e, n_used, wg, wu, wd, tf=512)
    return _combine(x, y, route, slots.astype(I32), tm=256)


def _ab_in_weight(w_in):
    r, k, v, wl, al, gl, zq, zc, zqi, zki, zwi = jnp.split(
        w_in, np.cumsum([1024, 1024, 1024, 96, 96, 256, 1024, 256, 1024, 64]).tolist(), axis=1)
    D = w_in.shape[0]
    z = lambda n: jnp.zeros((D, n), w_in.dtype)
    cols = [r, k, v, zq, zqi, gl, zc, wl, z(32), al, z(32), zki, zki, zwi, z(AB_COLS - AB_ZWI - 16)]
    return jnp.concatenate(cols, axis=1).astype(BF16)


def kernel(x, p, norm_mix, norm_ffn, ab_w_in, ab_mu, rwkv_w0, rwkv_w_up, rwkv_a0, rwkv_a_up, rwkv_g_up, rwkv_k_k, rwkv_k_a, rwkv_r_k, rwkv_gn_w, rwkv_gn_b, dsa_ckv_norm, dsa_w_uk, dsa_w_uv, ab_w_out, rel_bias, ffn_w_gate, ffn_w_up, ffn_w_down, gdn_w_in, gdn_conv, gdn_a_log, gdn_dt_bias, gdn_out_norm, gdn_w_out, moe_router, moe_w_gate, moe_w_up, moe_w_down, ple_norm, ple_w_gate, ple_w_proj, final_norm):
    B, L, D = x.shape
    T = B * L
    xf = x.reshape(T, D)
    bf = lambda w: w.astype(BF16)

    z = _norm_mm(xf, norm_mix[0], _ab_in_weight(ab_w_in[0]), tm=1024, tn=1024)
    y_a = _rwkv(z, B, L, ab_mu[0], rwkv_w0[0], rwkv_w_up[0], rwkv_a0[0], rwkv_a_up[0], rwkv_g_up[0],
                rwkv_k_k[0], rwkv_k_a[0], rwkv_r_k[0], rwkv_gn_w[0], rwkv_gn_b[0])
    y_b = _dsa(z, B, L, dsa_ckv_norm[0], dsa_w_uk[0], dsa_w_uv[0], rel_bias)
    xf = _mm_res(jnp.concatenate([y_a, y_b], axis=1), bf(ab_w_out[0]), xf, tm=1024, tn=1024)
    xf = _ffn(xf, norm_ffn[0], bf(ffn_w_gate[0]), bf(ffn_w_up[0]), bf(ffn_w_down[0]), tm=512, tf=512)
    xf = _ple(xf, ple_norm[0], bf(ple_w_gate[0]), p[0].reshape(T, PLE_DIM), bf(ple_w_proj[0]),
              final_norm, final=False, tm=512)

    z = _norm_mm(xf, norm_mix[1], _gd_in_weight(gdn_w_in[0]), tm=1024, tn=512)
    gates = _gdn_gates(z, B, L, gdn_a_log[0], gdn_dt_bias[0])
    o = _gdn(z, gates, B, L, gdn_conv[0], gdn_out_norm[0])
    xf = _mm_res(o, bf(gdn_w_out[0]), xf, tm=512, tn=1024)
    xf = _moe(xf, norm_ffn[1], moe_router[0], bf(moe_w_gate[0]), bf(moe_w_up[0]), bf(moe_w_down[0]))
    xf = _ple(xf, ple_norm[1], bf(ple_w_gate[1]), p[1].reshape(T, PLE_DIM), bf(ple_w_proj[1]),
              final_norm, final=True, tm=512)
    return xf.reshape(B, L, D)
```

```python
import functools
import math

import numpy as np
import jax
import jax.numpy as jnp
from jax import lax
from jax.experimental import pallas as pl
from jax.experimental.pallas import tpu as pltpu

F32 = jnp.float32
BF16 = jnp.bfloat16
I32 = jnp.int32
HI = lax.Precision.HIGHEST

EPS = 1e-6
LANES = 128
MIB = 1024 * 1024

RWKV_H, RWKV_N = 16, 64
RWKV_W = RWKV_H * RWKV_N
RWKV_LORA = 96
RWKV_GATE = 256
RWKV_GN_EPS = 6.4e-4
RWKV_CHUNK = 64
RWKV_TB = 256

DSA_H, DSA_D, DSA_C = 8, 128, 256
IDX_H, IDX_D = 16, 64
TOPK_MAX = 256
QB = 128
KEY_GROUP = 4
REL_BUCKETS, REL_MAX_DIST = 32, 128
NEG = -1e30
INT_MIN = -(2 ** 31)

GDN_QK_H, GDN_V_H, GDN_D = 16, 32, 128
GDN_KW = GDN_QK_H * GDN_D
GDN_VW = GDN_V_H * GDN_D
GDN_CHUNK = 64
GDN_TB = 256
GDN_CONV_W = 4
GDN_HEADS_PER_STEP = 4

N_EXPERTS, TOP_K = 8, 2
MOE_BLOCK = 512
PLE_DIM = 256

AB_R, AB_K, AB_V = 0, 1024, 2048
AB_ZQ, AB_ZQI = 3072, 4096
AB_GL, AB_ZC = 5120, 5376
AB_WL, AB_AL = 5632, 5760
AB_ZKI, AB_ZWI = 5888, 6016
AB_COLS = 6144
GD_Q, GD_K, GD_V, GD_ZG, GD_GATES = 0, 2048, 4096, 8192, 12288
GD_COLS = 12800


def _cp(sem, vmem_mib):
    return pltpu.CompilerParams(dimension_semantics=sem, vmem_limit_bytes=vmem_mib * MIB)


def _dot(a, b, precision=None):
    return jnp.dot(a, b, preferred_element_type=F32, precision=precision)


def _dot_nt(a, b, precision=None):
    return lax.dot_general(a, b, (((1,), (1,)), ((), ())), preferred_element_type=F32,
                           precision=precision)


def _split(x):
    hi = x.astype(BF16)
    return hi, (x - hi.astype(F32)).astype(BF16)


def _seg_dot(x, ones_bf):
    hi, lo = _split(x)
    return _dot(hi, ones_bf) + _dot(lo, ones_bf)


def _rms(x, g):
    ms = jnp.mean(x * x, axis=-1, keepdims=True)
    return x * lax.rsqrt(ms + EPS) * g


def _sigmoid(x):
    return 1.0 / (1.0 + jnp.exp(-x))


def _silu(x):
    return x * _sigmoid(x)


def _softplus(x):
    return jnp.maximum(x, 0.0) + jnp.log1p(jnp.exp(-jnp.abs(x)))


def _iota(shape, dim):
    return lax.broadcasted_iota(I32, shape, dim)


def _norm_mm_body(x_ref, g_ref, w_ref, o_ref, xn_ref):
    @pl.when(pl.program_id(1) == 0)
    def _():
        xn_ref[...] = _rms(x_ref[...], g_ref[...]).astype(BF16)

    o_ref[...] = _dot(xn_ref[...], w_ref[...]).astype(o_ref.dtype)


def _norm_mm(x, g, w, *, tm, tn):
    T, K = x.shape
    N = w.shape[1]
    return pl.pallas_call(
        _norm_mm_body,
        grid=(T // tm, N // tn),
        in_specs=[pl.BlockSpec((tm, K), lambda i, j: (i, 0)),
                  pl.BlockSpec((1, K), lambda i, j: (0, 0)),
                  pl.BlockSpec((K, tn), lambda i, j: (0, j))],
        out_specs=pl.BlockSpec((tm, tn), lambda i, j: (i, j)),
        out_shape=jax.ShapeDtypeStruct((T, N), F32),
        scratch_shapes=[pltpu.VMEM((tm, K), BF16)],
        compiler_params=_cp(("parallel", "arbitrary"), 48),
        name="norm_mm",
    )(x, g.reshape(1, K), w)


def _mm_res_body(a_ref, w_ref, r_ref, o_ref):
    o_ref[...] = r_ref[...] + _dot(a_ref[...], w_ref[...])


def _mm_res(a, w, res, *, tm, tn):
    T, K = a.shape
    N = w.shape[1]
    return pl.pallas_call(
        _mm_res_body,
        grid=(T // tm, N // tn),
        in_specs=[pl.BlockSpec((tm, K), lambda i, j: (i, 0)),
                  pl.BlockSpec((K, tn), lambda i, j: (0, j)),
                  pl.BlockSpec((tm, tn), lambda i, j: (i, j))],
        out_specs=pl.BlockSpec((tm, tn), lambda i, j: (i, j)),
        out_shape=jax.ShapeDtypeStruct((T, N), F32),
        compiler_params=_cp(("parallel", "arbitrary"), 48),
        name="mm_res",
    )(a, w, res)


def _ple_body(x_ref, g_ref, wg_ref, p_ref, wp_ref, fg_ref, o_ref, *, final):
    x = x_ref[...]
    gate = _sigmoid(_dot(_rms(x, g_ref[...]).astype(BF16), wg_ref[...]))
    y = x + gate * _dot(p_ref[...].astype(BF16), wp_ref[...])
    if final:
        y = _rms(y, fg_ref[...])
    o_ref[...] = y


def _ple(x, g, wg, p, wp, fg, *, final, tm):
    T, D = x.shape
    P = p.shape[1]
    return pl.pallas_call(
        functools.partial(_ple_body, final=final),
        grid=(T // tm,),
        in_specs=[pl.BlockSpec((tm, D), lambda i: (i, 0)),
                  pl.BlockSpec((1, D), lambda i: (0, 0)),
                  pl.BlockSpec((D, D), lambda i: (0, 0)),
                  pl.BlockSpec((tm, P), lambda i: (i, 0)),
                  pl.BlockSpec((P, D), lambda i: (0, 0)),
                  pl.BlockSpec((1, D), lambda i: (0, 0))],
        out_specs=pl.BlockSpec((tm, D), lambda i: (i, 0)),
        out_shape=jax.ShapeDtypeStruct((T, D), F32),
        compiler_params=_cp(("parallel",), 48),
        name="ple",
    )(x, g.reshape(1, D), wg, p, wp, fg.reshape(1, D))


def _ffn_body(x_ref, g_ref, wg_ref, wu_ref, wd_ref, o_ref, xn_ref):
    @pl.when(pl.program_id(1) == 0)
    def _():
        x = x_ref[...]
        xn_ref[...] = _rms(x, g_ref[...]).astype(BF16)
        o_ref[...] = x

    xn = xn_ref[...]
    h = (_silu(_dot(xn, wg_ref[...])) * _dot(xn, wu_ref[...])).astype(BF16)
    o_ref[...] += _dot(h, wd_ref[...])


def _ffn(x, g, wg, wu, wd, *, tm, tf):
    T, D = x.shape
    Fh = wg.shape[1]
    return pl.pallas_call(
        _ffn_body,
        grid=(T // tm, Fh // tf),
        in_specs=[pl.BlockSpec((tm, D), lambda i, f: (i, 0)),
                  pl.BlockSpec((1, D), lambda i, f: (0, 0)),
                  pl.BlockSpec((D, tf), lambda i, f: (0, f)),
                  pl.BlockSpec((D, tf), lambda i, f: (0, f)),
                  pl.BlockSpec((tf, D), lambda i, f: (f, 0))],
        out_specs=pl.BlockSpec((tm, D), lambda i, f: (i, 0)),
        out_shape=jax.ShapeDtypeStruct((T, D), F32),
        scratch_shapes=[pltpu.VMEM((tm, D), BF16)],
        compiler_params=_cp(("parallel", "arbitrary"), 48),
        name="ffn",
    )(x, g.reshape(1, D), wg, wu, wd)


def _shift_mix(x, prev_row, mu):
    xs = pltpu.roll(x, 1, axis=0)
    xs = jnp.where(_iota(x.shape, 0) == 0, prev_row, xs)
    return x + (xs - x) * mu


def _bd(x, lo):
    return jnp.concatenate([jnp.where(lo, x, 0.0), jnp.where(lo, 0.0, x)], axis=0)


def _unit_lower_inverses(xs, eye, steps):
    ps = [eye + x for x in xs]
    xps = list(xs)
    for _ in range(steps):
        xbs = [xp.astype(BF16) for xp in xps]
        xps = [_dot(xb, xb) for xb in xbs]
        ps = [_dot(p.astype(BF16), (eye + xp).astype(BF16)) for p, xp in zip(ps, xps)]
    return ps


def _rwkv_body(rkv_ref, gl_ref, wa_ref, mu_rkv_ref, mu_gl_ref, mu_wa_ref, w0_ref, wup_ref,
               a0_ref, aup_ref, gup_ref, kk_ref, ka_ref, rk_ref, gnw_ref, gnb_ref, o_ref,
               prev_rkv, prev_gl, prev_wa, state, r_s, k_s, v_s, kn_s, a_s, lw_s, cg_s, g_s, y_s):
    TB = rkv_ref.shape[0]
    C = RWKV_CHUNK
    NP = RWKV_W // LANES

    @pl.when(pl.program_id(1) == 0)
    def _():
        prev_rkv[...] = jnp.zeros_like(prev_rkv)
        prev_gl[...] = jnp.zeros_like(prev_gl)
        prev_wa[...] = jnp.zeros_like(prev_wa)
        state[...] = jnp.zeros_like(state)

    lane = _iota((1, LANES), 1)
    lo = lane < RWKV_N
    r128 = _iota((LANES, LANES), 0)
    c128 = _iota((LANES, LANES), 1)
    same = (r128 // C) == (c128 // C)
    seg_ones = jnp.where(same, 1.0, 0.0).astype(BF16)
    eye = jnp.where(r128 == c128, 1.0, 0.0).astype(F32)
    strict = same & (c128 < r128)
    incl = same & (c128 <= r128)
    rt = _iota((TB, TB), 0)
    ct = _iota((TB, TB), 1)
    tri_chunks = jnp.where(((rt // C) == (ct // C)) & (ct <= rt), 1.0, 0.0).astype(BF16)

    wa_raw = wa_ref[...]
    wa = _shift_mix(wa_raw, prev_wa[...], mu_wa_ref[...])
    prev_wa[...] = wa_raw[TB - 1:TB, :]
    wl = jnp.tanh(wa[:, :LANES]).astype(BF16)
    al = wa[:, LANES:].astype(BF16)
    w = -_softplus(-(w0_ref[...] + _dot(wl, wup_ref[...]))) - 0.5
    lw = -jnp.exp(w)
    lw_s[...] = lw
    lw_hi, lw_lo = _split(lw)
    cg_s[...] = _dot(tri_chunks, lw_hi) + _dot(tri_chunks, lw_lo)
    a_s[...] = _sigmoid(a0_ref[...] + _dot(al, aup_ref[...]))
    gl_raw = gl_ref[...]
    gl = _shift_mix(gl_raw, prev_gl[...], mu_gl_ref[...])
    prev_gl[...] = gl_raw[TB - 1:TB, :]
    g_s[...] = _dot(_sigmoid(gl).astype(BF16), gup_ref[...])

    for p in range(NP):
        cs = slice(p * LANES, (p + 1) * LANES)
        cols = [slice(off + p * LANES, off + (p + 1) * LANES) for off in (AB_R, AB_K, AB_V)]
        mixed = []
        for c in cols:
            raw = rkv_ref[:, c]
            mixed.append(_shift_mix(raw, prev_rkv[:, c], mu_rkv_ref[:, c]))
            prev_rkv[:, c] = raw[TB - 1:TB, :]
        r, k, v = mixed
        kk = k * kk_ref[:, cs]
        kn_s[:, cs] = kk * lax.rsqrt(_seg_dot(kk * kk, seg_ones) + 1e-6)
        r_s[:, cs] = r
        k_s[:, cs] = k * (1.0 + (a_s[:, cs] - 1.0) * ka_ref[:, cs])
        v_s[:, cs] = v

    def chunk(c, carry):
        rows = pl.ds(pl.multiple_of(c * C, C), C)
        pairs = range(NP)
        css = [slice(p * LANES, (p + 1) * LANES) for p in pairs]
        gam_last, ar, bk, bk_end, v_f, v_t = [], [], [], [], [], []
        for cs in css:
            cg = cg_s[rows, cs]
            gam = jnp.exp(cg)
            gam_inv = jnp.exp(-cg)
            gam_prev = jnp.exp(cg - lw_s[rows, cs])
            gl_ = gam[C - 1:C, :]
            kn = kn_s[rows, cs]
            b_raw = kn * a_s[rows, cs] * gam_inv
            k_raw = k_s[rows, cs] * gam_inv
            gam_last.append(gl_)
            ar.append(jnp.concatenate([_bd(-kn * gam_prev, lo), _bd(r_s[rows, cs] * gam, lo)],
                                      axis=0).astype(BF16))
            bk.append(jnp.concatenate([_bd(b_raw, lo), _bd(k_raw, lo)], axis=0).astype(BF16))
            bk_end.append(jnp.concatenate([_bd(b_raw * gl_, lo), _bd(k_raw * gl_, lo)],
                                          axis=0).astype(BF16))
            vf = _bd(v_s[rows, cs], lo)
            v_f.append(vf)
            v_t.append(vf.astype(BF16))
        score = [_dot_nt(ar[p], bk[p]) for p in pairs]
        a_ab = [jnp.where(strict, s[:LANES, :LANES], 0.0) for s in score]
        a_akv = [_dot(jnp.where(strict, score[p][:LANES, LANES:], 0.0).astype(BF16), v_t[p])
                 for p in pairs]
        r_abk = [jnp.concatenate([jnp.where(incl, s[LANES:, :LANES], 0.0),
                                  jnp.where(incl, s[LANES:, LANES:], 0.0)], axis=1).astype(BF16)
                 for s in score]
        t_inv = [t.astype(BF16) for t in _unit_lower_inverses(a_ab, eye, 5)]
        s_old = [state[p] for p in pairs]
        sproj = [_dot_nt(ar[p], s_old[p].astype(BF16)) for p in pairs]
        u = [_dot(t_inv[p], (sproj[p][:LANES] + a_akv[p]).astype(BF16)) for p in pairs]
        y_bd = [sproj[p][LANES:] + _dot(r_abk[p], jnp.concatenate([u[p].astype(BF16), v_t[p]], axis=0))
                for p in pairs]
        for p in pairs:
            y_s[rows, css[p]] = y_bd[p][:C] + y_bd[p][C:]
        uvT = [jnp.concatenate([u[p].T, v_f[p].T], axis=1).astype(BF16) for p in pairs]
        for p in pairs:
            state[p] = s_old[p] * gam_last[p] + _dot(uvT[p], bk_end[p])
        return carry

    lax.fori_loop(0, TB // C, chunk, 0)

    for p in range(NP):
        cs = slice(p * LANES, (p + 1) * LANES)
        y = y_s[:, cs]
        mean = _seg_dot(y, seg_ones) * (1.0 / RWKV_N)
        d = y - mean
        var = _seg_dot(d * d, seg_ones) * (1.0 / RWKV_N)
        yn = d * lax.rsqrt(var + RWKV_GN_EPS) * gnw_ref[:, cs] + gnb_ref[:, cs]
        bonus = _seg_dot(r_s[:, cs] * k_s[:, cs] * rk_ref[:, cs], seg_ones) * v_s[:, cs]
        o_ref[:, cs] = ((yn + bonus) * g_s[:, cs]).astype(o_ref.dtype)


def _rwkv(z, B, L, mu, w0, w_up, a0, a_up, g_up, k_k, k_a, r_k, gn_w, gn_b):
    TB = RWKV_TB
    nb = L // TB
    W = RWKV_W
    mu_r, mu_k, mu_v, mu_wl, mu_al, mu_gl = jnp.split(
        mu, np.cumsum([W, W, W, RWKV_LORA, RWKV_LORA])[:].tolist())
    pad = LANES - RWKV_LORA
    mu_rkv = jnp.concatenate([mu_r, mu_k, mu_v]).reshape(1, 3 * W)
    mu_wa = jnp.concatenate([jnp.pad(mu_wl, (0, pad)), jnp.pad(mu_al, (0, pad))]).reshape(1, 2 * LANES)
    wup = jnp.pad(w_up, ((0, pad), (0, 0))).astype(BF16)
    aup = jnp.pad(a_up, ((0, pad), (0, 0))).astype(BF16)
    row = lambda t: t.reshape(1, W)
    vec = lambda n: pl.BlockSpec((1, n), lambda b, i: (0, 0))
    big = lambda: pltpu.VMEM((TB, W), F32)
    return pl.pallas_call(
        _rwkv_body,
        grid=(B, nb),
        in_specs=[pl.BlockSpec((TB, 3 * W), lambda b, i: (b * nb + i, 0)),
                  pl.BlockSpec((TB, RWKV_GATE), lambda b, i: (b * nb + i, AB_GL // RWKV_GATE)),
                  pl.BlockSpec((TB, 2 * LANES), lambda b, i: (b * nb + i, AB_WL // (2 * LANES))),
                  vec(3 * W), vec(RWKV_GATE), vec(2 * LANES), vec(W),
                  pl.BlockSpec((LANES, W), lambda b, i: (0, 0)),
                  vec(W),
                  pl.BlockSpec((LANES, W), lambda b, i: (0, 0)),
                  pl.BlockSpec((RWKV_GATE, W), lambda b, i: (0, 0)),
                  vec(W), vec(W), vec(W), vec(W), vec(W)],
        out_specs=pl.BlockSpec((TB, W), lambda b, i: (b * nb + i, 0)),
        out_shape=jax.ShapeDtypeStruct((B * L, W), BF16),
        scratch_shapes=[pltpu.VMEM((1, 3 * W), F32), pltpu.VMEM((1, RWKV_GATE), F32),
                        pltpu.VMEM((1, 2 * LANES), F32),
                        pltpu.VMEM((W // LANES, LANES, LANES), F32),
                        big(), big(), big(), big(), big(), big(), big(), big(), big()],
        compiler_params=_cp(("arbitrary", "arbitrary"), 48),
        name="rwkv7",
    )(z, z, z, mu_rkv, mu_gl.reshape(1, RWKV_GATE), mu_wa, row(w0), wup, row(a0), aup,
      g_up.astype(BF16), row(k_k), row(k_a), row(r_k), row(gn_w), row(gn_b))


def _t5_bucket_np(dist):
    n = np.maximum(dist, 0)
    exact = REL_BUCKETS // 2
    ratio = np.log(np.maximum(n, 1).astype(np.float32) / np.float32(exact)) / np.float32(
        math.log(REL_MAX_DIST / exact))
    large = exact + (ratio.astype(np.float32) * np.float32(REL_BUCKETS - exact)).astype(np.int32)
    return np.where(n < exact, n, np.minimum(large, REL_BUCKETS - 1)).astype(np.int32)


def _near_buckets():
    kl = np.arange(QB)[:, None]
    ql = np.arange(QB)[None, :]
    return np.stack([_t5_bucket_np(ql - kl), _t5_bucket_np(QB + ql - kl)])


def _dsa_body(tbl_ref, bkt_ref, zq_ref, zqi_ref, zc_ref, zki_ref, zwi_ref, cn_ref, wukT_ref,
              wuvT_ref, o_ref, c_all, cT_all, kibd_all, sc, qiT, qlatT, bias, m_s, l_s, oT, *, topk):
    b = pl.program_id(0)
    qb = pl.program_id(1)
    lo = _iota((1, LANES), 1) < IDX_D
    krow = _iota((QB, QB), 0)
    qcol = _iota((QB, QB), 1)
    hsl = [slice(h * QB, (h + 1) * QB) for h in range(DSA_H)]

    @pl.when((b == 0) & (qb == 0))
    def _():
        for t in range(2):
            bk = bkt_ref[t]
            for h in range(DSA_H):
                far = tbl_ref[REL_BUCKETS - 1, h]
                acc = jnp.zeros((QB, QB), F32)
                for bb in range(REL_BUCKETS - 1):
                    acc = jnp.where(bk == bb, tbl_ref[bb, h] - far, acc)
                bias[t, h] = acc

    c_new = _rms(zc_ref[...], cn_ref[...])
    c_all[qb] = c_new.astype(BF16)
    cT_all[qb] = c_new.T.astype(BF16)
    kibd_all[qb] = _bd(zki_ref[...], lo).astype(BF16)

    for p in range(IDX_H // 2):
        qiT[:, p * QB:(p + 1) * QB] = zqi_ref[:, p * LANES:(p + 1) * LANES].T.astype(BF16)
    wT = zwi_ref[...].T * (IDX_D ** -0.5 * IDX_H ** -0.5)
    w_rows = [wT[h:h + 1, :] for h in range(IDX_H)]
    for h in range(DSA_H):
        qhT = zq_ref[:, hsl[h]].T.astype(BF16)
        qlatT[:, hsl[h]] = (_dot(wukT_ref[h], qhT) * DSA_D ** -0.5).astype(BF16)

    def score_blocks(j, nk):
        kb = kibd_all[pl.ds(j, nk)].reshape(nk * 2 * QB, LANES)
        acc = [jnp.zeros((QB, QB), F32) for _ in range(nk)]
        for pp in range(IDX_H // 4):
            s = _dot(kb, qiT[:, 2 * pp * QB:2 * (pp + 1) * QB])
            for k in range(nk):
                even = s[2 * k * QB:(2 * k + 1) * QB]
                odd = s[(2 * k + 1) * QB:(2 * k + 2) * QB]
                acc[k] = (acc[k] + w_rows[4 * pp] * jnp.maximum(even[:, :QB], 0.0)
                          + w_rows[4 * pp + 1] * jnp.maximum(odd[:, :QB], 0.0)
                          + w_rows[4 * pp + 2] * jnp.maximum(even[:, QB:], 0.0)
                          + w_rows[4 * pp + 3] * jnp.maximum(odd[:, QB:], 0.0))
        for k in range(nk):
            bits = lax.bitcast_convert_type(acc[k], I32)
            key = bits ^ ((bits >> 31) & 0x7FFFFFFF)
            sc[j + k] = jnp.where((j + k == qb) & (krow > qcol), INT_MIN, key)

    def score_group(jj, carry):
        score_blocks(jj * KEY_GROUP, KEY_GROUP)
        return carry

    def score_single(j, carry):
        score_blocks(j, 1)
        return carry

    n_sgroup = (qb + 1) // KEY_GROUP
    lax.fori_loop(0, n_sgroup, score_group, 0)
    lax.fori_loop(n_sgroup * KEY_GROUP, qb + 1, score_single, 0)

    def bis_body(i, t):
        cand = t ^ jnp.left_shift(jnp.int32(1), 31 - i)

        def cnt_quad(jj, a):
            blk = sc[pl.ds(jj * KEY_GROUP, KEY_GROUP)]
            for k in range(KEY_GROUP):
                a = a + jnp.where(blk[k] >= cand, 1, 0)
            return a

        def cnt_body(j, a):
            return a + jnp.where(sc[j] >= cand, 1, 0)

        n_quad = (qb + 1) // KEY_GROUP
        cnt = lax.fori_loop(0, n_quad, cnt_quad, jnp.zeros((QB, QB), I32))
        cnt = lax.fori_loop(n_quad * KEY_GROUP, qb + 1, cnt_body, cnt)
        return jnp.where(jnp.sum(cnt, axis=0, keepdims=True) >= topk, cand, t)

    thr = lax.fori_loop(0, 32, bis_body, jnp.full((1, QB), INT_MIN, I32))
    thr = jnp.maximum(thr, INT_MIN + 1)

    m_s[...] = jnp.full(m_s.shape, NEG, F32)
    l_s[...] = jnp.zeros_like(l_s)
    oT[...] = jnp.zeros_like(oT)

    def attend(j, nk, near):
        sel = sc[pl.ds(j, nk)].reshape(nk * QB, QB) >= thr
        lg = _dot(c_all[pl.ds(j, nk)].reshape(nk * QB, DSA_C), qlatT[...])
        cT = jnp.concatenate([cT_all[j + k] for k in range(nk)], axis=1)
        for h in range(DSA_H):
            lgh = lg[:, hsl[h]]
            if near is not None:
                lgh = lgh + bias[near, h]
            lgh = jnp.where(sel, lgh, NEG)
            m_old = m_s[:, hsl[h]]
            m_new = jnp.maximum(m_old, jnp.max(lgh, axis=0, keepdims=True))
            pr = jnp.exp(lgh - m_new)
            alpha = jnp.exp(m_old - m_new)
            l_s[:, hsl[h]] = alpha * l_s[:, hsl[h]] + jnp.sum(pr, axis=0, keepdims=True)
            m_s[:, hsl[h]] = m_new
            oT[:, hsl[h]] = alpha * oT[:, hsl[h]] + _dot(cT, pr.astype(BF16))

    def far_group(jj, carry):
        attend(jj * KEY_GROUP, KEY_GROUP, None)
        return carry

    def far_single(j, carry):
        attend(j, 1, None)
        return carry

    n_far = jnp.maximum(qb - 1, 0)
    n_group = n_far // KEY_GROUP
    lax.fori_loop(0, n_group, far_group, 0)
    lax.fori_loop(n_group * KEY_GROUP, n_far, far_single, 0)

    @pl.when(qb >= 1)
    def _():
        attend(qb - 1, 1, 1)

    attend(qb, 1, 0)

    inv_l = 1.0 / l_s[...]
    for h in range(DSA_H):
        oh = (oT[:, hsl[h]] * inv_l[:, hsl[h]]).astype(BF16)
        o_ref[:, hsl[h]] = _dot(wuvT_ref[h], oh).T.astype(o_ref.dtype)


def _dsa(z, B, L, ckv_norm, w_uk, w_uv, rel_bias):
    nq = L // QB
    topk = min(TOPK_MAX, L // 4)
    wukT = jnp.swapaxes(w_uk, 1, 2).astype(BF16)
    wuvT = jnp.swapaxes(w_uv, 1, 2).astype(BF16)
    blk = lambda w, off: pl.BlockSpec((QB, w), lambda b, q: (b * nq + q, off // w))
    full = lambda shape: pl.BlockSpec(shape, lambda b, q: (0,) * len(shape))
    W = DSA_H * DSA_D
    return pl.pallas_call(
        functools.partial(_dsa_body, topk=topk),
        grid=(B, nq),
        in_specs=[pl.BlockSpec(memory_space=pltpu.SMEM),
                  full((2, QB, QB)),
                  blk(W, AB_ZQ), blk(IDX_H * IDX_D, AB_ZQI), blk(DSA_C, AB_ZC),
                  blk(LANES, AB_ZKI), blk(LANES, AB_ZWI),
                  full((1, DSA_C)), full((DSA_H, DSA_C, DSA_D)), full((DSA_H, DSA_D, DSA_C))],
        out_specs=pl.BlockSpec((QB, W), lambda b, q: (b * nq + q, 0)),
        out_shape=jax.ShapeDtypeStruct((B * L, W), BF16),
        scratch_shapes=[pltpu.VMEM((nq, QB, DSA_C), BF16), pltpu.VMEM((nq, DSA_C, QB), BF16),
                        pltpu.VMEM((nq, 2 * QB, LANES), BF16), pltpu.VMEM((nq, QB, QB), I32),
                        pltpu.VMEM((LANES, IDX_H // 2 * QB), BF16), pltpu.VMEM((DSA_C, W), BF16),
                        pltpu.VMEM((2, DSA_H, QB, QB), F32),
                        pltpu.VMEM((1, W), F32), pltpu.VMEM((1, W), F32), pltpu.VMEM((DSA_C, W), F32)],
        compiler_params=_cp(("arbitrary", "arbitrary"), 48),
        name="dsa",
    )(rel_bias, jnp.asarray(_near_buckets()), z, z, z, z, z, ckv_norm.reshape(1, DSA_C), wukT, wuvT)


def _gdn_gates_body(z_ref, alog_ref, dtb_ref, o_ref):
    TB = z_ref.shape[0]
    C = GDN_CHUNK
    z = z_ref[...]
    sub = _iota((1, LANES), 1) % 8
    beta = _sigmoid(z)
    g = -jnp.exp(alog_ref[...]) * _softplus(z + dtb_ref[...])
    rt = _iota((TB, TB), 0)
    ct = _iota((TB, TB), 1)
    same = (rt // C) == (ct // C)
    cum = _dot(jnp.where(same & (ct <= rt), 1.0, 0.0).astype(F32), g, HI)
    tot = _dot(jnp.where(same, 1.0, 0.0).astype(F32), g, HI)
    tile = jnp.where(sub < 2, beta, jnp.where(sub < 4, g, jnp.where(sub < 6, cum, tot)))
    o_ref[...] = tile.T


def _gdn_gates(z, B, L, a_log, dt_bias):
    TB = GDN_TB
    nb = L // TB
    spread = lambda t: jnp.zeros((GDN_QK_H, 8), F32).at[:, 2:].set(
        jnp.tile(t.reshape(GDN_QK_H, 2), (1, 3))).reshape(1, LANES)
    return pl.pallas_call(
        _gdn_gates_body,
        grid=(B, nb),
        in_specs=[pl.BlockSpec((TB, LANES), lambda b, i: (b * nb + i, GD_GATES // LANES)),
                  pl.BlockSpec((1, LANES), lambda b, i: (0, 0)),
                  pl.BlockSpec((1, LANES), lambda b, i: (0, 0))],
        out_specs=pl.BlockSpec((None, LANES, TB), lambda b, i: (b, 0, i)),
        out_shape=jax.ShapeDtypeStruct((B, LANES, L), F32),
        compiler_params=_cp(("parallel", "parallel"), 32),
        name="gdn_gates",
    )(z, spread(a_log), spread(dt_bias))


def _gdn_body(zq_ref, zk_ref, zv_ref, zg_ref, gates_ref, cwq_ref, cwk_ref, cwv_ref, on_ref, o_ref,
              xbuf, state, q_s, k_s, v_s, *, hg):
    TB = zq_ref.shape[0]
    C = GDN_CHUNK
    D = GDN_D
    KW = GDN_CONV_W
    W = 4 * D * hg

    @pl.when(pl.program_id(2) == 0)
    def _():
        xbuf[0:8, :] = jnp.zeros((8, W), F32)
        state[...] = jnp.zeros_like(state)

    xbuf[8:TB + 8, 0:D * hg] = zq_ref[...]
    xbuf[8:TB + 8, D * hg:2 * D * hg] = zk_ref[...]
    xbuf[8:TB + 8, 2 * D * hg:W] = zv_ref[...]
    cw = jnp.concatenate([cwq_ref[...], cwk_ref[...], cwv_ref[...]], axis=1)
    l2 = lambda t: t * lax.rsqrt(jnp.sum(t * t, axis=-1, keepdims=True) + 1e-6)
    for g in range(W // D):
        cols = slice(g * D, (g + 1) * D)
        acc = jnp.zeros((TB, D), F32)
        for j in range(KW):
            acc = acc + cw[j:j + 1, cols] * xbuf[8 - (KW - 1) + j:8 - (KW - 1) + j + TB, cols]
        act = _silu(acc)
        if g < hg:
            q_s[:, cols] = l2(act) * D ** -0.5
        elif g < 2 * hg:
            k_s[:, (g - hg) * D:(g - hg + 1) * D] = l2(act)
        else:
            v_s[:, (g - 2 * hg) * D:(g - 2 * hg + 1) * D] = act
    xbuf[0:8, :] = xbuf[TB:TB + 8, :]

    lane = _iota((1, LANES), 1)
    lo = lane < C
    r128 = _iota((LANES, LANES), 0)
    c128 = _iota((LANES, LANES), 1)
    same = (r128 // C) == (c128 // C)
    eye_m = r128 == c128
    eye = jnp.where(eye_m, 1.0, 0.0).astype(F32)
    strict = same & (c128 < r128)
    incl = same & (c128 <= r128)

    heads = range(hg)
    for c in range(TB // C):
        rows = slice(c * C, (c + 1) * C)
        beta_st, e_st, fb_st, etot_cat, kc, qc, dec = [], [], [], [], [], [], []
        for h in heads:
            win = gates_ref[8 * h:8 * h + 8, (c // 2) * LANES:(c // 2 + 1) * LANES]
            win_sw = pltpu.roll(win, C, axis=1)
            first, second = (win, win_sw) if c % 2 == 0 else (win_sw, win)
            st = lambda r: jnp.where(lo, first[r:r + 1, :], second[r + 1:r + 2, :])
            beta, cum, tot = st(0), st(4), st(6)
            etot = jnp.exp(tot)
            etot_sw = pltpu.roll(etot, C, axis=1)
            beta_st.append(beta)
            e_st.append(jnp.exp(cum))
            fb_st.append(jnp.exp(tot - cum) * beta)
            etot_cat.append(jnp.concatenate([jnp.where(lo, etot, etot_sw),
                                             jnp.where(lo, etot_sw, etot)], axis=1))
            kc.append(k_s[rows, h * D:(h + 1) * D])
            qc.append(q_s[rows, h * D:(h + 1) * D])
            cum_b = jnp.broadcast_to(cum, (LANES, LANES))
            dec.append(jnp.exp(jnp.where(incl, cum_b.T - cum_b, 0.0)))
        score = [_dot_nt(jnp.concatenate([kc[h], kc[h], qc[h], qc[h]], axis=0).astype(BF16),
                         jnp.concatenate([kc[h], kc[h]], axis=0).astype(BF16))
                 for h in heads]
        ab = [jnp.where(strict, score[h][:LANES] * dec[h], 0.0) * beta_st[h] for h in heads]
        t_inv = _unit_lower_inverses([-a for a in ab], eye, 5)
        t_cat = [jnp.concatenate([t_inv[h], -(t_inv[h] * e_st[h])], axis=1).astype(BF16) for h in heads]
        a_cat = [jnp.concatenate([eye * e_st[h],
                                  jnp.where(incl, score[h][LANES:] * dec[h], 0.0) * beta_st[h]],
                                 axis=1).astype(BF16) for h in heads]
        s_old = [state[h] for h in heads]
        proj = [_dot(jnp.concatenate([kc[h], qc[h]], axis=0).astype(BF16), s_old[h].astype(BF16))
                for h in heads]
        rhs = [jnp.concatenate([v_s[rows, 2 * h * D:(2 * h + 1) * D],
                                v_s[rows, (2 * h + 1) * D:(2 * h + 2) * D],
                                proj[h][:C, :D], proj[h][:C, D:]], axis=0).astype(BF16) for h in heads]
        vn = [_dot(t_cat[h], rhs[h]) for h in heads]
        vn_bf = [x.astype(BF16) for x in vn]
        o_st = [_dot(a_cat[h], jnp.concatenate(
            [proj[h][C:, :D].astype(BF16), proj[h][C:, D:].astype(BF16), vn_bf[h]], axis=0))
            for h in heads]
        fv = [_dot((eye * fb_st[h]).astype(BF16), vn_bf[h]) for h in heads]
        for h in heads:
            state[h] = s_old[h] * etot_cat[h] + _dot(
                kc[h].T.astype(BF16), jnp.concatenate([fv[h][:C], fv[h][C:]], axis=1).astype(BF16))
        for h in heads:
            for u in range(2):
                cols = slice((2 * h + u) * D, (2 * h + u + 1) * D)
                oh = o_st[h][u * C:(u + 1) * C]
                o_ref[rows, cols] = (_rms(oh, on_ref[...]) * _silu(zg_ref[rows, cols])).astype(o_ref.dtype)


def _gdn(z, gates, B, L, conv_w, out_norm):
    TB = GDN_TB
    nb = L // TB
    D = GDN_D
    hg = GDN_HEADS_PER_STEP
    zspec = lambda w, off: pl.BlockSpec((TB, w), lambda b, h, i: (b * nb + i, off // w + h))
    cspec = lambda w, off: pl.BlockSpec((GDN_CONV_W, w), lambda b, h, i: (0, off // w + h))
    return pl.pallas_call(
        functools.partial(_gdn_body, hg=hg),
        grid=(B, GDN_QK_H // hg, nb),
        in_specs=[zspec(D * hg, GD_Q), zspec(D * hg, GD_K), zspec(2 * D * hg, GD_V),
                  zspec(2 * D * hg, GD_ZG),
                  pl.BlockSpec((None, 8 * hg, TB), lambda b, h, i: (b, h, i)),
                  cspec(D * hg, GD_Q), cspec(D * hg, GD_K), cspec(2 * D * hg, GD_V),
                  pl.BlockSpec((1, D), lambda b, h, i: (0, 0))],
        out_specs=pl.BlockSpec((TB, 2 * D * hg), lambda b, h, i: (b * nb + i, h)),
        out_shape=jax.ShapeDtypeStruct((B * L, GDN_VW), BF16),
        scratch_shapes=[pltpu.VMEM((TB + 8, 4 * D * hg), F32), pltpu.VMEM((hg, D, 2 * D), F32),
                        pltpu.VMEM((TB, D * hg), F32), pltpu.VMEM((TB, D * hg), F32),
                        pltpu.VMEM((TB, 2 * D * hg), F32)],
        compiler_params=_cp(("parallel", "parallel", "arbitrary"), 40),
        name="gdn",
    )(z, z, z, z, gates, conv_w, conv_w, conv_w, out_norm.reshape(1, D))


def _gd_in_weight(w_in):
    qkv, zg, b, a = jnp.split(w_in, np.cumsum([2 * GDN_KW + GDN_VW, GDN_VW, GDN_V_H]).tolist(), axis=1)
    D = w_in.shape[0]
    pair = lambda t: t.reshape(D, GDN_QK_H, 2)
    gates = jnp.concatenate([pair(b), pair(a), pair(a), pair(a)], axis=2).reshape(D, LANES)
    pad = jnp.zeros((D, GD_COLS - GD_GATES - LANES), w_in.dtype)
    return jnp.concatenate([qkv, zg, gates, pad], axis=1).astype(BF16)


def _router_body(x_ref, g_ref, wr_ref, o_ref):
    logits = _dot(_rms(x_ref[...], g_ref[...]), wr_ref[...], HI)
    lane = _iota(logits.shape, 1)
    logits = jnp.where(lane < N_EXPERTS, logits, -jnp.inf)
    m1 = jnp.max(logits, axis=-1, keepdims=True)
    i1 = jnp.min(jnp.where(logits == m1, lane, LANES), axis=-1, keepdims=True)
    rest = jnp.where(lane == i1, -jnp.inf, logits)
    m2 = jnp.max(rest, axis=-1, keepdims=True)
    i2 = jnp.min(jnp.where(rest == m2, lane, LANES), axis=-1, keepdims=True)
    e = jnp.exp(m2 - m1)
    w1 = 1.0 / (1.0 + e)
    o_ref[...] = jnp.where(lane == 0, i1.astype(F32),
                           jnp.where(lane == 1, i2.astype(F32),
                                     jnp.where(lane == 2, w1, jnp.where(lane == 3, e * w1, 0.0))))


def _router(x, g, w_router, *, tm):
    T, D = x.shape
    wr = jnp.pad(w_router, ((0, 0), (0, LANES - N_EXPERTS)))
    return pl.pallas_call(
        _router_body,
        grid=(T // tm,),
        in_specs=[pl.BlockSpec((tm, D), lambda i: (i, 0)),
                  pl.BlockSpec((1, D), lambda i: (0, 0)),
                  pl.BlockSpec((D, LANES), lambda i: (0, 0))],
        out_specs=pl.BlockSpec((tm, LANES), lambda i: (i, 0)),
        out_shape=jax.ShapeDtypeStruct((T, LANES), F32),
        compiler_params=_cp(("parallel",), 32),
        name="router",
    )(x, g.reshape(1, D), wr)


def _row_copy(src_hbm, row, dst, r, sem):
    return pltpu.make_async_copy(src_hbm.at[pl.ds(row, 1), :], dst.at[pl.ds(r, 1), :], sem)


def _experts_body(tok_ref, be_ref, nu_ref, x_hbm, g_ref, wg_ref, wu_ref, wd_ref, o_ref,
                  xbuf, xn_ref, sem):
    i = pl.program_id(0)
    f = pl.program_id(1)
    MB = xbuf.shape[1]
    active = i < nu_ref[0]
    slot = i % 2

    def gather(blk, s):
        def issue(r, c):
            _row_copy(x_hbm, tok_ref[blk * MB + r], xbuf.at[s], r, sem.at[s]).start()
            return c

        lax.fori_loop(0, MB, issue, 0)

    @pl.when(f == 0)
    def _():
        o_ref[...] = jnp.zeros_like(o_ref)

    @pl.when((f == 0) & (i == 0) & active)
    def _():
        gather(0, 0)

    @pl.when((f == 0) & active)
    def _():
        def wait(r, c):
            _row_copy(x_hbm, 0, xbuf.at[slot], r, sem.at[slot]).wait()
            return c

        lax.fori_loop(0, MB, wait, 0)
        xn_ref[...] = _rms(xbuf[slot], g_ref[...]).astype(BF16)

    @pl.when((f == 1) & (i + 1 < nu_ref[0]))
    def _():
        gather(i + 1, 1 - slot)

    @pl.when(active)
    def _():
        xn = xn_ref[...]
        h = (_silu(_dot(xn, wg_ref[...])) * _dot(xn, wu_ref[...])).astype(BF16)
        o_ref[...] += _dot(h, wd_ref[...])


def _experts(x, g, tok, blk_e, n_used, wg, wu, wd, *, tf):
    T, D = x.shape
    MB = MOE_BLOCK
    n_blk = tok.shape[0] // MB
    Fh = wg.shape[2]
    assert Fh // tf >= 2, "the next block's rows are requested during hidden chunk 1"
    fe = lambda i, f, nu: jnp.where(i < nu[0], f, 0)
    return pl.pallas_call(
        _experts_body,
        grid_spec=pltpu.PrefetchScalarGridSpec(
            num_scalar_prefetch=3,
            grid=(n_blk, Fh // tf),
            in_specs=[pl.BlockSpec(memory_space=pl.ANY),
                      pl.BlockSpec((1, D), lambda i, f, tk, be, nu: (0, 0)),
                      pl.BlockSpec((None, D, tf), lambda i, f, tk, be, nu: (be[i], 0, fe(i, f, nu))),
                      pl.BlockSpec((None, D, tf), lambda i, f, tk, be, nu: (be[i], 0, fe(i, f, nu))),
                      pl.BlockSpec((None, tf, D), lambda i, f, tk, be, nu: (be[i], fe(i, f, nu), 0))],
            out_specs=pl.BlockSpec((MB, D), lambda i, f, tk, be, nu: (i, 0)),
            scratch_shapes=[pltpu.VMEM((2, MB, D), F32), pltpu.VMEM((MB, D), BF16),
                            pltpu.SemaphoreType.DMA((2,))]),
        out_shape=jax.ShapeDtypeStruct((n_blk * MB, D), F32),
        compiler_params=_cp(("arbitrary", "arbitrary"), 52),
        name="experts",
    )(tok, blk_e, n_used, x, g.reshape(1, D), wg, wu, wd)


def _combine_body(slot_ref, y_hbm, x_ref, r_ref, o_ref, y0, y1, sem):
    i = pl.program_id(0)
    tm = x_ref.shape[0]

    def issue(r, c):
        a = (i * tm + r) * TOP_K
        _row_copy(y_hbm, slot_ref[a], y0, r, sem).start()
        _row_copy(y_hbm, slot_ref[a + 1], y1, r, sem).start()
        return c

    lax.fori_loop(0, tm, issue, 0)

    def wait(r, c):
        _row_copy(y_hbm, 0, y0, r, sem).wait()
        _row_copy(y_hbm, 0, y1, r, sem).wait()
        return c

    lax.fori_loop(0, tm, wait, 0)
    route = r_ref[...]
    o_ref[...] = x_ref[...] + y0[...] * route[:, 2:3] + y1[...] * route[:, 3:4]


def _combine(x, y, route, slots, *, tm):
    T, D = x.shape
    return pl.pallas_call(
        _combine_body,
        grid_spec=pltpu.PrefetchScalarGridSpec(
            num_scalar_prefetch=1,
            grid=(T // tm,),
            in_specs=[pl.BlockSpec(memory_space=pl.ANY),
                      pl.BlockSpec((tm, D), lambda i, s: (i, 0)),
                      pl.BlockSpec((tm, LANES), lambda i, s: (i, 0))],
            out_specs=pl.BlockSpec((tm, D), lambda i, s: (i, 0)),
            scratch_shapes=[pltpu.VMEM((tm, D), F32), pltpu.VMEM((tm, D), F32),
                            pltpu.SemaphoreType.DMA(())]),
        out_shape=jax.ShapeDtypeStruct((T, D), F32),
        compiler_params=_cp(("arbitrary",), 32),
        name="combine",
    )(slots, y, x, route)


def _moe(x, g, w_router, wg, wu, wd):
    T, D = x.shape
    MB = MOE_BLOCK
    A = T * TOP_K
    route = _router(x, g, w_router, tm=512)
    flat_e = route[:, :TOP_K].astype(I32).reshape(A)
    onehot = (flat_e[:, None] == jnp.arange(N_EXPERTS, dtype=I32)[None, :]).astype(I32)
    csum = jnp.cumsum(onehot, axis=0)
    rank = jnp.take_along_axis(csum, flat_e[:, None], axis=1)[:, 0] - 1
    padded = (csum[-1] + MB - 1) // MB * MB
    pend = jnp.cumsum(padded)
    slots = (pend - padded)[flat_e] + rank
    n_blk = A // MB + N_EXPERTS
    tok = jnp.zeros((n_blk * MB,), I32).at[slots].set(jnp.arange(A, dtype=I32) // TOP_K)
    blk_start = jnp.arange(n_blk, dtype=I32) * MB
    blk_e = jnp.minimum(jnp.sum(blk_start[:, None] >= pend[None, :], axis=1), N_EXPERTS - 1).astype(I32)
    n_used = (pend[-1:] // MB).astype(I32)
    y = _experts(x, g, tok, blk_e, n_used, wg, wu, wd, tf=min(1024, wg.shape[2]))
    return _combine(x, y, route, slots.astype(I32), tm=256)


def _ab_in_weight(w_in):
    r, k, v, wl, al, gl, zq, zc, zqi, zki, zwi = jnp.split(
        w_in, np.cumsum([1024, 1024, 1024, 96, 96, 256, 1024, 256, 1024, 64]).tolist(), axis=1)
    D = w_in.shape[0]
    z = lambda n: jnp.zeros((D, n), w_in.dtype)
    cols = [r, k, v, zq, zqi, gl, zc, wl, z(32), al, z(32), zki, zki, zwi, z(AB_COLS - AB_ZWI - 16)]
    return jnp.concatenate(cols, axis=1).astype(BF16)


def kernel(x, p, norm_mix, norm_ffn, ab_w_in, ab_mu, rwkv_w0, rwkv_w_up, rwkv_a0, rwkv_a_up, rwkv_g_up, rwkv_k_k, rwkv_k_a, rwkv_r_k, rwkv_gn_w, rwkv_gn_b, dsa_ckv_norm, dsa_w_uk, dsa_w_uv, ab_w_out, rel_bias, ffn_w_gate, ffn_w_up, ffn_w_down, gdn_w_in, gdn_conv, gdn_a_log, gdn_dt_bias, gdn_out_norm, gdn_w_out, moe_router, moe_w_gate, moe_w_up, moe_w_down, ple_norm, ple_w_gate, ple_w_proj, final_norm):
    B, L, D = x.shape
    T = B * L
    xf = x.reshape(T, D)
    bf = lambda w: w.astype(BF16)

    z = _norm_mm(xf, norm_mix[0], _ab_in_weight(ab_w_in[0]), tm=1024, tn=1024)
    y_a = _rwkv(z, B, L, ab_mu[0], rwkv_w0[0], rwkv_w_up[0], rwkv_a0[0], rwkv_a_up[0], rwkv_g_up[0],
                rwkv_k_k[0], rwkv_k_a[0], rwkv_r_k[0], rwkv_gn_w[0], rwkv_gn_b[0])
    y_b = _dsa(z, B, L, dsa_ckv_norm[0], dsa_w_uk[0], dsa_w_uv[0], rel_bias)
    xf = _mm_res(jnp.concatenate([y_a, y_b], axis=1), bf(ab_w_out[0]), xf, tm=1024, tn=1024)
    xf = _ffn(xf, norm_ffn[0], bf(ffn_w_gate[0]), bf(ffn_w_up[0]), bf(ffn_w_down[0]), tm=512, tf=512)
    xf = _ple(xf, ple_norm[0], bf(ple_w_gate[0]), p[0].reshape(T, PLE_DIM), bf(ple_w_proj[0]),
              final_norm, final=False, tm=512)

    z = _norm_mm(xf, norm_mix[1], _gd_in_weight(gdn_w_in[0]), tm=1024, tn=512)
    gates = _gdn_gates(z, B, L, gdn_a_log[0], gdn_dt_bias[0])
    o = _gdn(z, gates, B, L, gdn_conv[0], gdn_out_norm[0])
    xf = _mm_res(o, bf(gdn_w_out[0]), xf, tm=512, tn=1024)
    xf = _moe(xf, norm_ffn[1], moe_router[0], bf(moe_w_gate[0]), bf(moe_w_up[0]), bf(moe_w_down[0]))
    xf = _ple(xf, ple_norm[1], bf(ple_w_gate[1]), p[1].reshape(T, PLE_DIM), bf(ple_w_proj[1]),
              final_norm, final=True, tm=512)
    return xf.reshape(B, L, D)
```

```python
import functools
import math

import numpy as np
import jax
import jax.numpy as jnp
from jax import lax
from jax.experimental import pallas as pl
from jax.experimental.pallas import tpu as pltpu

F32 = jnp.float32
BF16 = jnp.bfloat16
I32 = jnp.int32
HI = lax.Precision.HIGHEST

EPS = 1e-6
LANES = 128
MIB = 1024 * 1024

RWKV_H, RWKV_N = 16, 64
RWKV_W = RWKV_H * RWKV_N
RWKV_LORA = 96
RWKV_GATE = 256
RWKV_GN_EPS = 6.4e-4
RWKV_CHUNK = 64
RWKV_TB = 256

DSA_H, DSA_D, DSA_C = 8, 128, 256
IDX_H, IDX_D = 16, 64
TOPK_MAX = 256
QB = 128
KEY_GROUP = 4
REL_BUCKETS, REL_MAX_DIST = 32, 128
NEG = -1e30
INT_MIN = -(2 ** 31)

GDN_QK_H, GDN_V_H, GDN_D = 16, 32, 128
GDN_KW = GDN_QK_H * GDN_D
GDN_VW = GDN_V_H * GDN_D
GDN_CHUNK = 64
GDN_TB = 256
GDN_CONV_W = 4
GDN_HEADS_PER_STEP = 4

N_EXPERTS, TOP_K = 8, 2
MOE_BLOCK = 512
PLE_DIM = 256

AB_R, AB_K, AB_V = 0, 1024, 2048
AB_ZQ, AB_ZQI = 3072, 4096
AB_GL, AB_ZC = 5120, 5376
AB_WL, AB_AL = 5632, 5760
AB_ZKI, AB_ZWI = 5888, 6016
AB_COLS = 6144
GD_Q, GD_K, GD_V, GD_ZG, GD_GATES = 0, 2048, 4096, 8192, 12288
GD_COLS = 12800


def _cp(sem, vmem_mib):
    return pltpu.CompilerParams(dimension_semantics=sem, vmem_limit_bytes=vmem_mib * MIB)


def _dot(a, b, precision=None):
    return jnp.dot(a, b, preferred_element_type=F32, precision=precision)


def _dot_nt(a, b, precision=None):
    return lax.dot_general(a, b, (((1,), (1,)), ((), ())), preferred_element_type=F32,
                           precision=precision)


def _split(x):
    hi = x.astype(BF16)
    return hi, (x - hi.astype(F32)).astype(BF16)


def _seg_dot(x, ones_bf):
    hi, lo = _split(x)
    return _dot(hi, ones_bf) + _dot(lo, ones_bf)


def _rms(x, g):
    ms = jnp.mean(x * x, axis=-1, keepdims=True)
    return x * lax.rsqrt(ms + EPS) * g


def _sigmoid(x):
    return 1.0 / (1.0 + jnp.exp(-x))


def _silu(x):
    return x * _sigmoid(x)


def _softplus(x):
    return jnp.maximum(x, 0.0) + jnp.log1p(jnp.exp(-jnp.abs(x)))


def _iota(shape, dim):
    return lax.broadcasted_iota(I32, shape, dim)


def _norm_mm_body(x_ref, g_ref, w_ref, o_ref, xn_ref):
    @pl.when(pl.program_id(1) == 0)
    def _():
        xn_ref[...] = _rms(x_ref[...], g_ref[...]).astype(BF16)

    o_ref[...] = _dot(xn_ref[...], w_ref[...]).astype(o_ref.dtype)


def _norm_mm(x, g, w, *, tm, tn):
    T, K = x.shape
    N = w.shape[1]
    return pl.pallas_call(
        _norm_mm_body,
        grid=(T // tm, N // tn),
        in_specs=[pl.BlockSpec((tm, K), lambda i, j: (i, 0)),
                  pl.BlockSpec((1, K), lambda i, j: (0, 0)),
                  pl.BlockSpec((K, tn), lambda i, j: (0, j))],
        out_specs=pl.BlockSpec((tm, tn), lambda i, j: (i, j)),
        out_shape=jax.ShapeDtypeStruct((T, N), F32),
        scratch_shapes=[pltpu.VMEM((tm, K), BF16)],
        compiler_params=_cp(("parallel", "arbitrary"), 48),
        name="norm_mm",
    )(x, g.reshape(1, K), w)


def _mm_res_body(*refs):
    *a_refs, w_ref, r_ref, o_ref = refs
    acc = r_ref[...]
    off = 0
    for a_ref in a_refs:
        k = a_ref.shape[1]
        acc = acc + _dot(a_ref[...], w_ref[off:off + k, :])
        off += k
    o_ref[...] = acc


def _mm_res(parts, w, res, *, tm, tn):
    T = res.shape[0]
    K, N = w.shape
    assert sum(a.shape[1] for a in parts) == K
    return pl.pallas_call(
        _mm_res_body,
        grid=(T // tm, N // tn),
        in_specs=[pl.BlockSpec((tm, a.shape[1]), lambda i, j: (i, 0)) for a in parts]
        + [pl.BlockSpec((K, tn), lambda i, j: (0, j)),
           pl.BlockSpec((tm, tn), lambda i, j: (i, j))],
        out_specs=pl.BlockSpec((tm, tn), lambda i, j: (i, j)),
        out_shape=jax.ShapeDtypeStruct((T, N), F32),
        compiler_params=_cp(("parallel", "arbitrary"), 48),
        name="mm_res",
    )(*parts, w, res)


def _ple_body(x_ref, g_ref, wg_ref, p_ref, wp_ref, fg_ref, o_ref, *, final):
    x = x_ref[...]
    gate = _sigmoid(_dot(_rms(x, g_ref[...]).astype(BF16), wg_ref[...]))
    y = x + gate * _dot(p_ref[...].astype(BF16), wp_ref[...])
    if final:
        y = _rms(y, fg_ref[...])
    o_ref[...] = y


def _ple(x, g, wg, p, wp, fg, *, final, tm):
    T, D = x.shape
    P = p.shape[1]
    return pl.pallas_call(
        functools.partial(_ple_body, final=final),
        grid=(T // tm,),
        in_specs=[pl.BlockSpec((tm, D), lambda i: (i, 0)),
                  pl.BlockSpec((1, D), lambda i: (0, 0)),
                  pl.BlockSpec((D, D), lambda i: (0, 0)),
                  pl.BlockSpec((tm, P), lambda i: (i, 0)),
                  pl.BlockSpec((P, D), lambda i: (0, 0)),
                  pl.BlockSpec((1, D), lambda i: (0, 0))],
        out_specs=pl.BlockSpec((tm, D), lambda i: (i, 0)),
        out_shape=jax.ShapeDtypeStruct((T, D), F32),
        compiler_params=_cp(("parallel",), 48),
        name="ple",
    )(x, g.reshape(1, D), wg, p, wp, fg.reshape(1, D))


def _ffn_body(x_ref, g_ref, wg_ref, wu_ref, wd_ref, o_ref, xn_ref):
    @pl.when(pl.program_id(1) == 0)
    def _():
        x = x_ref[...]
        xn_ref[...] = _rms(x, g_ref[...]).astype(BF16)
        o_ref[...] = x

    xn = xn_ref[...]
    h = (_silu(_dot(xn, wg_ref[...])) * _dot(xn, wu_ref[...])).astype(BF16)
    o_ref[...] += _dot(h, wd_ref[...])


def _ffn(x, g, wg, wu, wd, *, tm, tf):
    T, D = x.shape
    Fh = wg.shape[1]
    return pl.pallas_call(
        _ffn_body,
        grid=(T // tm, Fh // tf),
        in_specs=[pl.BlockSpec((tm, D), lambda i, f: (i, 0)),
                  pl.BlockSpec((1, D), lambda i, f: (0, 0)),
                  pl.BlockSpec((D, tf), lambda i, f: (0, f)),
                  pl.BlockSpec((D, tf), lambda i, f: (0, f)),
                  pl.BlockSpec((tf, D), lambda i, f: (f, 0))],
        out_specs=pl.BlockSpec((tm, D), lambda i, f: (i, 0)),
        out_shape=jax.ShapeDtypeStruct((T, D), F32),
        scratch_shapes=[pltpu.VMEM((tm, D), BF16)],
        compiler_params=_cp(("parallel", "arbitrary"), 48),
        name="ffn",
    )(x, g.reshape(1, D), wg, wu, wd)


def _shift_mix(x, prev_row, mu):
    xs = pltpu.roll(x, 1, axis=0)
    xs = jnp.where(_iota(x.shape, 0) == 0, prev_row, xs)
    return x + (xs - x) * mu


def _bd(x, lo):
    return jnp.concatenate([jnp.where(lo, x, 0.0), jnp.where(lo, 0.0, x)], axis=0)


def _unit_lower_inverses(xs, eye, steps):
    ps = [eye + x for x in xs]
    xps = list(xs)
    for _ in range(steps):
        xbs = [xp.astype(BF16) for xp in xps]
        xps = [_dot(xb, xb) for xb in xbs]
        ps = [_dot(p.astype(BF16), (eye + xp).astype(BF16)) for p, xp in zip(ps, xps)]
    return ps


def _rwkv_body(rkv_ref, gl_ref, wa_ref, mu_rkv_ref, mu_gl_ref, mu_wa_ref, w0_ref, wup_ref,
               a0_ref, aup_ref, gup_ref, kk_ref, ka_ref, rk_ref, gnw_ref, gnb_ref, o_ref,
               prev_rkv, prev_gl, prev_wa, state, r_s, k_s, v_s, kn_s, a_s, lw_s, cg_s, g_s, y_s):
    TB = rkv_ref.shape[0]
    C = RWKV_CHUNK
    NP = RWKV_W // LANES

    @pl.when(pl.program_id(1) == 0)
    def _():
        prev_rkv[...] = jnp.zeros_like(prev_rkv)
        prev_gl[...] = jnp.zeros_like(prev_gl)
        prev_wa[...] = jnp.zeros_like(prev_wa)
        state[...] = jnp.zeros_like(state)

    lane = _iota((1, LANES), 1)
    lo = lane < RWKV_N
    r128 = _iota((LANES, LANES), 0)
    c128 = _iota((LANES, LANES), 1)
    same = (r128 // C) == (c128 // C)
    seg_ones = jnp.where(same, 1.0, 0.0).astype(BF16)
    eye = jnp.where(r128 == c128, 1.0, 0.0).astype(F32)
    strict = same & (c128 < r128)
    incl = same & (c128 <= r128)
    rt = _iota((TB, TB), 0)
    ct = _iota((TB, TB), 1)
    tri_chunks = jnp.where(((rt // C) == (ct // C)) & (ct <= rt), 1.0, 0.0).astype(BF16)

    wa_raw = wa_ref[...]
    wa = _shift_mix(wa_raw, prev_wa[...], mu_wa_ref[...])
    prev_wa[...] = wa_raw[TB - 1:TB, :]
    wl = jnp.tanh(wa[:, :LANES]).astype(BF16)
    al = wa[:, LANES:].astype(BF16)
    w = -_softplus(-(w0_ref[...] + _dot(wl, wup_ref[...]))) - 0.5
    lw = -jnp.exp(w)
    lw_s[...] = lw
    lw_hi, lw_lo = _split(lw)
    cg_s[...] = _dot(tri_chunks, lw_hi) + _dot(tri_chunks, lw_lo)
    a_s[...] = _sigmoid(a0_ref[...] + _dot(al, aup_ref[...]))
    gl_raw = gl_ref[...]
    gl = _shift_mix(gl_raw, prev_gl[...], mu_gl_ref[...])
    prev_gl[...] = gl_raw[TB - 1:TB, :]
    g_s[...] = _dot(_sigmoid(gl).astype(BF16), gup_ref[...])

    for p in range(NP):
        cs = slice(p * LANES, (p + 1) * LANES)
        cols = [slice(off + p * LANES, off + (p + 1) * LANES) for off in (AB_R, AB_K, AB_V)]
        mixed = []
        for c in cols:
            raw = rkv_ref[:, c]
            mixed.append(_shift_mix(raw, prev_rkv[:, c], mu_rkv_ref[:, c]))
            prev_rkv[:, c] = raw[TB - 1:TB, :]
        r, k, v = mixed
        kk = k * kk_ref[:, cs]
        kn_s[:, cs] = kk * lax.rsqrt(_seg_dot(kk * kk, seg_ones) + 1e-6)
        r_s[:, cs] = r
        k_s[:, cs] = k * (1.0 + (a_s[:, cs] - 1.0) * ka_ref[:, cs])
        v_s[:, cs] = v

    def chunk(c, carry):
        rows = pl.ds(pl.multiple_of(c * C, C), C)
        pairs = range(NP)
        css = [slice(p * LANES, (p + 1) * LANES) for p in pairs]
        gam_last, ar, bk, bk_end, v_f, v_t = [], [], [], [], [], []
        for cs in css:
            cg = cg_s[rows, cs]
            gam = jnp.exp(cg)
            gam_inv = jnp.exp(-cg)
            gam_prev = jnp.exp(cg - lw_s[rows, cs])
            gl_ = gam[C - 1:C, :]
            kn = kn_s[rows, cs]
            b_raw = kn * a_s[rows, cs] * gam_inv
            k_raw = k_s[rows, cs] * gam_inv
            gam_last.append(gl_)
            ar.append(jnp.concatenate([_bd(-kn * gam_prev, lo), _bd(r_s[rows, cs] * gam, lo)],
                                      axis=0).astype(BF16))
            bk.append(jnp.concatenate([_bd(b_raw, lo), _bd(k_raw, lo)], axis=0).astype(BF16))
            bk_end.append(jnp.concatenate([_bd(b_raw * gl_, lo), _bd(k_raw * gl_, lo)],
                                          axis=0).astype(BF16))
            vf = _bd(v_s[rows, cs], lo)
            v_f.append(vf)
            v_t.append(vf.astype(BF16))
        score = [_dot_nt(ar[p], bk[p]) for p in pairs]
        a_ab = [jnp.where(strict, s[:LANES, :LANES], 0.0) for s in score]
        a_akv = [_dot(jnp.where(strict, score[p][:LANES, LANES:], 0.0).astype(BF16), v_t[p])
                 for p in pairs]
        r_abk = [jnp.concatenate([jnp.where(incl, s[LANES:, :LANES], 0.0),
                                  jnp.where(incl, s[LANES:, LANES:], 0.0)], axis=1).astype(BF16)
                 for s in score]
        t_inv = [t.astype(BF16) for t in _unit_lower_inverses(a_ab, eye, 5)]
        s_old = [state[p] for p in pairs]
        sproj = [_dot_nt(ar[p], s_old[p].astype(BF16)) for p in pairs]
        u = [_dot(t_inv[p], (sproj[p][:LANES] + a_akv[p]).astype(BF16)) for p in pairs]
        y_bd = [sproj[p][LANES:] + _dot(r_abk[p], jnp.concatenate([u[p].astype(BF16), v_t[p]], axis=0))
                for p in pairs]
        for p in pairs:
            y_s[rows, css[p]] = y_bd[p][:C] + y_bd[p][C:]
        uvT = [jnp.concatenate([u[p].T, v_f[p].T], axis=1).astype(BF16) for p in pairs]
        for p in pairs:
            state[p] = s_old[p] * gam_last[p] + _dot(uvT[p], bk_end[p])
        return carry

    lax.fori_loop(0, TB // C, chunk, 0)

    for p in range(NP):
        cs = slice(p * LANES, (p + 1) * LANES)
        y = y_s[:, cs]
        mean = _seg_dot(y, seg_ones) * (1.0 / RWKV_N)
        d = y - mean
        var = _seg_dot(d * d, seg_ones) * (1.0 / RWKV_N)
        yn = d * lax.rsqrt(var + RWKV_GN_EPS) * gnw_ref[:, cs] + gnb_ref[:, cs]
        bonus = _seg_dot(r_s[:, cs] * k_s[:, cs] * rk_ref[:, cs], seg_ones) * v_s[:, cs]
        o_ref[:, cs] = ((yn + bonus) * g_s[:, cs]).astype(o_ref.dtype)


def _rwkv(z, B, L, mu, w0, w_up, a0, a_up, g_up, k_k, k_a, r_k, gn_w, gn_b):
    TB = RWKV_TB
    nb = L // TB
    W = RWKV_W
    mu_r, mu_k, mu_v, mu_wl, mu_al, mu_gl = jnp.split(
        mu, np.cumsum([W, W, W, RWKV_LORA, RWKV_LORA])[:].tolist())
    pad = LANES - RWKV_LORA
    mu_rkv = jnp.concatenate([mu_r, mu_k, mu_v]).reshape(1, 3 * W)
    mu_wa = jnp.concatenate([jnp.pad(mu_wl, (0, pad)), jnp.pad(mu_al, (0, pad))]).reshape(1, 2 * LANES)
    wup = jnp.pad(w_up, ((0, pad), (0, 0))).astype(BF16)
    aup = jnp.pad(a_up, ((0, pad), (0, 0))).astype(BF16)
    row = lambda t: t.reshape(1, W)
    vec = lambda n: pl.BlockSpec((1, n), lambda b, i: (0, 0))
    big = lambda: pltpu.VMEM((TB, W), F32)
    return pl.pallas_call(
        _rwkv_body,
        grid=(B, nb),
        in_specs=[pl.BlockSpec((TB, 3 * W), lambda b, i: (b * nb + i, 0)),
                  pl.BlockSpec((TB, RWKV_GATE), lambda b, i: (b * nb + i, AB_GL // RWKV_GATE)),
                  pl.BlockSpec((TB, 2 * LANES), lambda b, i: (b * nb + i, AB_WL // (2 * LANES))),
                  vec(3 * W), vec(RWKV_GATE), vec(2 * LANES), vec(W),
                  pl.BlockSpec((LANES, W), lambda b, i: (0, 0)),
                  vec(W),
                  pl.BlockSpec((LANES, W), lambda b, i: (0, 0)),
                  pl.BlockSpec((RWKV_GATE, W), lambda b, i: (0, 0)),
                  vec(W), vec(W), vec(W), vec(W), vec(W)],
        out_specs=pl.BlockSpec((TB, W), lambda b, i: (b * nb + i, 0)),
        out_shape=jax.ShapeDtypeStruct((B * L, W), BF16),
        scratch_shapes=[pltpu.VMEM((1, 3 * W), F32), pltpu.VMEM((1, RWKV_GATE), F32),
                        pltpu.VMEM((1, 2 * LANES), F32),
                        pltpu.VMEM((W // LANES, LANES, LANES), F32),
                        big(), big(), big(), big(), big(), big(), big(), big(), big()],
        compiler_params=_cp(("arbitrary", "arbitrary"), 48),
        name="rwkv7",
    )(z, z, z, mu_rkv, mu_gl.reshape(1, RWKV_GATE), mu_wa, row(w0), wup, row(a0), aup,
      g_up.astype(BF16), row(k_k), row(k_a), row(r_k), row(gn_w), row(gn_b))


def _t5_bucket_np(dist):
    n = np.maximum(dist, 0)
    exact = REL_BUCKETS // 2
    ratio = np.log(np.maximum(n, 1).astype(np.float32) / np.float32(exact)) / np.float32(
        math.log(REL_MAX_DIST / exact))
    large = exact + (ratio.astype(np.float32) * np.float32(REL_BUCKETS - exact)).astype(np.int32)
    return np.where(n < exact, n, np.minimum(large, REL_BUCKETS - 1)).astype(np.int32)


def _near_buckets():
    kl = np.arange(QB)[:, None]
    ql = np.arange(QB)[None, :]
    return np.stack([_t5_bucket_np(ql - kl), _t5_bucket_np(QB + ql - kl)])


def _dsa_body(tbl_ref, bkt_ref, zq_ref, zqi_ref, zc_ref, zki_ref, zwi_ref, cn_ref, wukT_ref,
              wuvT_ref, o_ref, c_all, cT_all, kibd_all, sc, qiT, qlatT, bias, m_s, l_s, alpha_s,
              oT, *, topk):
    b = pl.program_id(0)
    qb = pl.program_id(1)
    lo = _iota((1, LANES), 1) < IDX_D
    krow = _iota((QB, QB), 0)
    qcol = _iota((QB, QB), 1)
    hsl = [slice(h * QB, (h + 1) * QB) for h in range(DSA_H)]

    @pl.when((b == 0) & (qb == 0))
    def _():
        for t in range(2):
            bk = bkt_ref[t]
            for h in range(DSA_H):
                far = tbl_ref[REL_BUCKETS - 1, h]
                acc = jnp.zeros((QB, QB), F32)
                for bb in range(REL_BUCKETS - 1):
                    acc = jnp.where(bk == bb, tbl_ref[bb, h] - far, acc)
                bias[t, h] = acc

    c_new = _rms(zc_ref[...], cn_ref[...])
    c_all[qb] = c_new.astype(BF16)
    cT_all[qb] = c_new.T.astype(BF16)
    kibd_all[qb] = _bd(zki_ref[...], lo).astype(BF16)

    for p in range(IDX_H // 2):
        qiT[:, p * QB:(p + 1) * QB] = zqi_ref[:, p * LANES:(p + 1) * LANES].T.astype(BF16)
    wT = zwi_ref[...].T * (IDX_D ** -0.5 * IDX_H ** -0.5)
    w_rows = [wT[h:h + 1, :] for h in range(IDX_H)]
    for h in range(DSA_H):
        qhT = zq_ref[:, hsl[h]].T.astype(BF16)
        qlatT[:, hsl[h]] = (_dot(wukT_ref[h], qhT) * DSA_D ** -0.5).astype(BF16)

    def score_blocks(j, nk):
        kb = kibd_all[pl.ds(j, nk)].reshape(nk * 2 * QB, LANES)
        acc = [jnp.zeros((QB, QB), F32) for _ in range(nk)]
        for pp in range(IDX_H // 4):
            s = _dot(kb, qiT[:, 2 * pp * QB:2 * (pp + 1) * QB])
            for k in range(nk):
                even = s[2 * k * QB:(2 * k + 1) * QB]
                odd = s[(2 * k + 1) * QB:(2 * k + 2) * QB]
                acc[k] = (acc[k] + w_rows[4 * pp] * jnp.maximum(even[:, :QB], 0.0)
                          + w_rows[4 * pp + 1] * jnp.maximum(odd[:, :QB], 0.0)
                          + w_rows[4 * pp + 2] * jnp.maximum(even[:, QB:], 0.0)
                          + w_rows[4 * pp + 3] * jnp.maximum(odd[:, QB:], 0.0))
        for k in range(nk):
            bits = lax.bitcast_convert_type(acc[k], I32)
            key = bits ^ ((bits >> 31) & 0x7FFFFFFF)
            sc[j + k] = jnp.where((j + k == qb) & (krow > qcol), INT_MIN, key)

    def score_group(jj, carry):
        score_blocks(jj * KEY_GROUP, KEY_GROUP)
        return carry

    def score_single(j, carry):
        score_blocks(j, 1)
        return carry

    n_sgroup = (qb + 1) // KEY_GROUP
    lax.fori_loop(0, n_sgroup, score_group, 0)
    lax.fori_loop(n_sgroup * KEY_GROUP, qb + 1, score_single, 0)

    def bis_body(i, t):
        cand = t ^ jnp.left_shift(jnp.int32(1), 31 - i)

        def cnt_quad(jj, a):
            blk = sc[pl.ds(jj * KEY_GROUP, KEY_GROUP)]
            for k in range(KEY_GROUP):
                a = a + jnp.where(blk[k] >= cand, 1, 0)
            return a

        def cnt_body(j, a):
            return a + jnp.where(sc[j] >= cand, 1, 0)

        n_quad = (qb + 1) // KEY_GROUP
        cnt = lax.fori_loop(0, n_quad, cnt_quad, jnp.zeros((QB, QB), I32))
        cnt = lax.fori_loop(n_quad * KEY_GROUP, qb + 1, cnt_body, cnt)
        return jnp.where(jnp.sum(cnt, axis=0, keepdims=True) >= topk, cand, t)

    thr = lax.fori_loop(0, 32, bis_body, jnp.full((1, QB), INT_MIN, I32))
    thr = jnp.maximum(thr, INT_MIN + 1)

    m_s[...] = jnp.full(m_s.shape, NEG, F32)
    l_s[...] = jnp.zeros_like(l_s)
    oT[...] = jnp.zeros_like(oT)

    def attend(j, nk, near):
        sel = sc[pl.ds(j, nk)].reshape(nk * QB, QB) >= thr
        lg = _dot(c_all[pl.ds(j, nk)].reshape(nk * QB, DSA_C), qlatT[...])
        cT = jnp.concatenate([cT_all[j + k] for k in range(nk)], axis=1)
        prs = []
        for h in range(DSA_H):
            lgh = lg[:, hsl[h]]
            if near is not None:
                lgh = lgh + bias[near, h]
            lgh = jnp.where(sel, lgh, NEG)
            m_old = m_s[:, hsl[h]]
            m_new = jnp.maximum(m_old, jnp.max(lgh, axis=0, keepdims=True))
            pr = jnp.exp(lgh - m_new)
            alpha_s[:, hsl[h]] = jnp.exp(m_old - m_new)
            l_s[:, hsl[h]] = alpha_s[:, hsl[h]] * l_s[:, hsl[h]] + jnp.sum(pr, axis=0, keepdims=True)
            m_s[:, hsl[h]] = m_new
            prs.append(pr.astype(BF16))
        oT[...] = alpha_s[...] * oT[...] + _dot(cT, jnp.concatenate(prs, axis=1))

    def far_group(jj, carry):
        attend(jj * KEY_GROUP, KEY_GROUP, None)
        return carry

    def far_single(j, carry):
        attend(j, 1, None)
        return carry

    n_far = jnp.maximum(qb - 1, 0)
    n_group = n_far // KEY_GROUP
    lax.fori_loop(0, n_group, far_group, 0)
    lax.fori_loop(n_group * KEY_GROUP, n_far, far_single, 0)

    @pl.when(qb >= 1)
    def _():
        attend(qb - 1, 1, 1)

    attend(qb, 1, 0)

    inv_l = 1.0 / l_s[...]
    for h in range(DSA_H):
        oh = (oT[:, hsl[h]] * inv_l[:, hsl[h]]).astype(BF16)
        o_ref[:, hsl[h]] = _dot(wuvT_ref[h], oh).T.astype(o_ref.dtype)


def _dsa(z, B, L, ckv_norm, w_uk, w_uv, rel_bias):
    nq = L // QB
    topk = min(TOPK_MAX, L // 4)
    wukT = jnp.swapaxes(w_uk, 1, 2).astype(BF16)
    wuvT = jnp.swapaxes(w_uv, 1, 2).astype(BF16)
    blk = lambda w, off: pl.BlockSpec((QB, w), lambda b, q: (b * nq + q, off // w))
    full = lambda shape: pl.BlockSpec(shape, lambda b, q: (0,) * len(shape))
    W = DSA_H * DSA_D
    return pl.pallas_call(
        functools.partial(_dsa_body, topk=topk),
        grid=(B, nq),
        in_specs=[pl.BlockSpec(memory_space=pltpu.SMEM),
                  full((2, QB, QB)),
                  blk(W, AB_ZQ), blk(IDX_H * IDX_D, AB_ZQI), blk(DSA_C, AB_ZC),
                  blk(LANES, AB_ZKI), blk(LANES, AB_ZWI),
                  full((1, DSA_C)), full((DSA_H, DSA_C, DSA_D)), full((DSA_H, DSA_D, DSA_C))],
        out_specs=pl.BlockSpec((QB, W), lambda b, q: (b * nq + q, 0)),
        out_shape=jax.ShapeDtypeStruct((B * L, W), BF16),
        scratch_shapes=[pltpu.VMEM((nq, QB, DSA_C), BF16), pltpu.VMEM((nq, DSA_C, QB), BF16),
                        pltpu.VMEM((nq, 2 * QB, LANES), BF16), pltpu.VMEM((nq, QB, QB), I32),
                        pltpu.VMEM((LANES, IDX_H // 2 * QB), BF16), pltpu.VMEM((DSA_C, W), BF16),
                        pltpu.VMEM((2, DSA_H, QB, QB), F32),
                        pltpu.VMEM((1, W), F32), pltpu.VMEM((1, W), F32), pltpu.VMEM((1, W), F32),
                        pltpu.VMEM((DSA_C, W), F32)],
        compiler_params=_cp(("arbitrary", "arbitrary"), 48),
        name="dsa",
    )(rel_bias, jnp.asarray(_near_buckets()), z, z, z, z, z, ckv_norm.reshape(1, DSA_C), wukT, wuvT)


def _gdn_gates_body(z_ref, alog_ref, dtb_ref, o_ref):
    TB = z_ref.shape[0]
    C = GDN_CHUNK
    z = z_ref[...]
    sub = _iota((1, LANES), 1) % 8
    beta = _sigmoid(z)
    g = -jnp.exp(alog_ref[...]) * _softplus(z + dtb_ref[...])
    rt = _iota((TB, TB), 0)
    ct = _iota((TB, TB), 1)
    same = (rt // C) == (ct // C)
    cum = _dot(jnp.where(same & (ct <= rt), 1.0, 0.0).astype(F32), g, HI)
    tot = _dot(jnp.where(same, 1.0, 0.0).astype(F32), g, HI)
    tile = jnp.where(sub < 2, beta, jnp.where(sub < 4, g, jnp.where(sub < 6, cum, tot)))
    o_ref[...] = tile.T


def _gdn_gates(z, B, L, a_log, dt_bias):
    TB = GDN_TB
    nb = L // TB
    spread = lambda t: jnp.zeros((GDN_QK_H, 8), F32).at[:, 2:].set(
        jnp.tile(t.reshape(GDN_QK_H, 2), (1, 3))).reshape(1, LANES)
    return pl.pallas_call(
        _gdn_gates_body,
        grid=(B, nb),
        in_specs=[pl.BlockSpec((TB, LANES), lambda b, i: (b * nb + i, GD_GATES // LANES)),
                  pl.BlockSpec((1, LANES), lambda b, i: (0, 0)),
                  pl.BlockSpec((1, LANES), lambda b, i: (0, 0))],
        out_specs=pl.BlockSpec((None, LANES, TB), lambda b, i: (b, 0, i)),
        out_shape=jax.ShapeDtypeStruct((B, LANES, L), F32),
        compiler_params=_cp(("parallel", "parallel"), 32),
        name="gdn_gates",
    )(z, spread(a_log), spread(dt_bias))


def _gdn_body(zq_ref, zk_ref, zv_ref, zg_ref, gates_ref, cwq_ref, cwk_ref, cwv_ref, on_ref, o_ref,
              xbuf, state, q_s, k_s, v_s, *, hg):
    TB = zq_ref.shape[0]
    C = GDN_CHUNK
    D = GDN_D
    KW = GDN_CONV_W
    W = 4 * D * hg

    @pl.when(pl.program_id(2) == 0)
    def _():
        xbuf[0:8, :] = jnp.zeros((8, W), F32)
        state[...] = jnp.zeros_like(state)

    xbuf[8:TB + 8, 0:D * hg] = zq_ref[...]
    xbuf[8:TB + 8, D * hg:2 * D * hg] = zk_ref[...]
    xbuf[8:TB + 8, 2 * D * hg:W] = zv_ref[...]
    cw = jnp.concatenate([cwq_ref[...], cwk_ref[...], cwv_ref[...]], axis=1)
    l2 = lambda t: t * lax.rsqrt(jnp.sum(t * t, axis=-1, keepdims=True) + 1e-6)
    for g in range(W // D):
        cols = slice(g * D, (g + 1) * D)
        acc = jnp.zeros((TB, D), F32)
        for j in range(KW):
            acc = acc + cw[j:j + 1, cols] * xbuf[8 - (KW - 1) + j:8 - (KW - 1) + j + TB, cols]
        act = _silu(acc)
        if g < hg:
            q_s[:, cols] = l2(act) * D ** -0.5
        elif g < 2 * hg:
            k_s[:, (g - hg) * D:(g - hg + 1) * D] = l2(act)
        else:
            v_s[:, (g - 2 * hg) * D:(g - 2 * hg + 1) * D] = act
    xbuf[0:8, :] = xbuf[TB:TB + 8, :]

    lane = _iota((1, LANES), 1)
    lo = lane < C
    r128 = _iota((LANES, LANES), 0)
    c128 = _iota((LANES, LANES), 1)
    same = (r128 // C) == (c128 // C)
    eye_m = r128 == c128
    eye = jnp.where(eye_m, 1.0, 0.0).astype(F32)
    strict = same & (c128 < r128)
    incl = same & (c128 <= r128)

    heads = range(hg)
    for c in range(TB // C):
        rows = slice(c * C, (c + 1) * C)
        beta_st, e_st, fb_st, etot_cat, kc, qc, dec = [], [], [], [], [], [], []
        for h in heads:
            win = gates_ref[8 * h:8 * h + 8, (c // 2) * LANES:(c // 2 + 1) * LANES]
            win_sw = pltpu.roll(win, C, axis=1)
            first, second = (win, win_sw) if c % 2 == 0 else (win_sw, win)
            st = lambda r: jnp.where(lo, first[r:r + 1, :], second[r + 1:r + 2, :])
            beta, cum, tot = st(0), st(4), st(6)
            etot = jnp.exp(tot)
            etot_sw = pltpu.roll(etot, C, axis=1)
            beta_st.append(beta)
            e_st.append(jnp.exp(cum))
            fb_st.append(jnp.exp(tot - cum) * beta)
            etot_cat.append(jnp.concatenate([jnp.where(lo, etot, etot_sw),
                                             jnp.where(lo, etot_sw, etot)], axis=1))
            kc.append(k_s[rows, h * D:(h + 1) * D])
            qc.append(q_s[rows, h * D:(h + 1) * D])
            cum_b = jnp.broadcast_to(cum, (LANES, LANES))
            dec.append(jnp.exp(jnp.where(incl, cum_b.T - cum_b, 0.0)))
        score = [_dot_nt(jnp.concatenate([kc[h], kc[h], qc[h], qc[h]], axis=0).astype(BF16),
                         jnp.concatenate([kc[h], kc[h]], axis=0).astype(BF16))
                 for h in heads]
        ab = [jnp.where(strict, score[h][:LANES] * dec[h], 0.0) * beta_st[h] for h in heads]
        t_inv = _unit_lower_inverses([-a for a in ab], eye, 5)
        t_cat = [jnp.concatenate([t_inv[h], -(t_inv[h] * e_st[h])], axis=1).astype(BF16) for h in heads]
        a_cat = [jnp.concatenate([eye * e_st[h],
                                  jnp.where(incl, score[h][LANES:] * dec[h], 0.0) * beta_st[h]],
                                 axis=1).astype(BF16) for h in heads]
        s_old = [state[h] for h in heads]
        proj = [_dot(jnp.concatenate([kc[h], qc[h]], axis=0).astype(BF16), s_old[h].astype(BF16))
                for h in heads]
        rhs = [jnp.concatenate([v_s[rows, 2 * h * D:(2 * h + 1) * D],
                                v_s[rows, (2 * h + 1) * D:(2 * h + 2) * D],
                                proj[h][:C, :D], proj[h][:C, D:]], axis=0).astype(BF16) for h in heads]
        vn = [_dot(t_cat[h], rhs[h]) for h in heads]
        vn_bf = [x.astype(BF16) for x in vn]
        o_st = [_dot(a_cat[h], jnp.concatenate(
            [proj[h][C:, :D].astype(BF16), proj[h][C:, D:].astype(BF16), vn_bf[h]], axis=0))
            for h in heads]
        fv = [_dot((eye * fb_st[h]).astype(BF16), vn_bf[h]) for h in heads]
        for h in heads:
            state[h] = s_old[h] * etot_cat[h] + _dot(
                kc[h].T.astype(BF16), jnp.concatenate([fv[h][:C], fv[h][C:]], axis=1).astype(BF16))
        for h in heads:
            for u in range(2):
                cols = slice((2 * h + u) * D, (2 * h + u + 1) * D)
                oh = o_st[h][u * C:(u + 1) * C]
                o_ref[rows, cols] = (_rms(oh, on_ref[...]) * _silu(zg_ref[rows, cols])).astype(o_ref.dtype)


def _gdn(z, gates, B, L, conv_w, out_norm):
    TB = GDN_TB
    nb = L // TB
    D = GDN_D
    hg = GDN_HEADS_PER_STEP
    zspec = lambda w, off: pl.BlockSpec((TB, w), lambda b, h, i: (b * nb + i, off // w + h))
    cspec = lambda w, off: pl.BlockSpec((GDN_CONV_W, w), lambda b, h, i: (0, off // w + h))
    return pl.pallas_call(
        functools.partial(_gdn_body, hg=hg),
        grid=(B, GDN_QK_H // hg, nb),
        in_specs=[zspec(D * hg, GD_Q), zspec(D * hg, GD_K), zspec(2 * D * hg, GD_V),
                  zspec(2 * D * hg, GD_ZG),
                  pl.BlockSpec((None, 8 * hg, TB), lambda b, h, i: (b, h, i)),
                  cspec(D * hg, GD_Q), cspec(D * hg, GD_K), cspec(2 * D * hg, GD_V),
                  pl.BlockSpec((1, D), lambda b, h, i: (0, 0))],
        out_specs=pl.BlockSpec((TB, 2 * D * hg), lambda b, h, i: (b * nb + i, h)),
        out_shape=jax.ShapeDtypeStruct((B * L, GDN_VW), BF16),
        scratch_shapes=[pltpu.VMEM((TB + 8, 4 * D * hg), F32), pltpu.VMEM((hg, D, 2 * D), F32),
                        pltpu.VMEM((TB, D * hg), F32), pltpu.VMEM((TB, D * hg), F32),
                        pltpu.VMEM((TB, 2 * D * hg), F32)],
        compiler_params=_cp(("parallel", "parallel", "arbitrary"), 40),
        name="gdn",
    )(z, z, z, z, gates, conv_w, conv_w, conv_w, out_norm.reshape(1, D))


def _gd_in_weight(w_in):
    qkv, zg, b, a = jnp.split(w_in, np.cumsum([2 * GDN_KW + GDN_VW, GDN_VW, GDN_V_H]).tolist(), axis=1)
    D = w_in.shape[0]
    pair = lambda t: t.reshape(D, GDN_QK_H, 2)
    gates = jnp.concatenate([pair(b), pair(a), pair(a), pair(a)], axis=2).reshape(D, LANES)
    pad = jnp.zeros((D, GD_COLS - GD_GATES - LANES), w_in.dtype)
    return jnp.concatenate([qkv, zg, gates, pad], axis=1).astype(BF16)


def _router_body(x_ref, g_ref, wr_ref, o_ref):
    logits = _dot(_rms(x_ref[...], g_ref[...]), wr_ref[...], HI)
    lane = _iota(logits.shape, 1)
    logits = jnp.where(lane < N_EXPERTS, logits, -jnp.inf)
    m1 = jnp.max(logits, axis=-1, keepdims=True)
    i1 = jnp.min(jnp.where(logits == m1, lane, LANES), axis=-1, keepdims=True)
    rest = jnp.where(lane == i1, -jnp.inf, logits)
    m2 = jnp.max(rest, axis=-1, keepdims=True)
    i2 = jnp.min(jnp.where(rest == m2, lane, LANES), axis=-1, keepdims=True)
    e = jnp.exp(m2 - m1)
    w1 = 1.0 / (1.0 + e)
    o_ref[...] = jnp.where(lane == 0, i1.astype(F32),
                           jnp.where(lane == 1, i2.astype(F32),
                                     jnp.where(lane == 2, w1, jnp.where(lane == 3, e * w1, 0.0))))


def _router(x, g, w_router, *, tm):
    T, D = x.shape
    wr = jnp.pad(w_router, ((0, 0), (0, LANES - N_EXPERTS)))
    return pl.pallas_call(
        _router_body,
        grid=(T // tm,),
        in_specs=[pl.BlockSpec((tm, D), lambda i: (i, 0)),
                  pl.BlockSpec((1, D), lambda i: (0, 0)),
                  pl.BlockSpec((D, LANES), lambda i: (0, 0))],
        out_specs=pl.BlockSpec((tm, LANES), lambda i: (i, 0)),
        out_shape=jax.ShapeDtypeStruct((T, LANES), F32),
        compiler_params=_cp(("parallel",), 32),
        name="router",
    )(x, g.reshape(1, D), wr)


def _row_copy(src_hbm, row, dst, r, sem):
    return pltpu.make_async_copy(src_hbm.at[pl.ds(row, 1), :], dst.at[pl.ds(r, 1), :], sem)


def _experts_body(tok_ref, be_ref, nu_ref, x_hbm, g_ref, wg_ref, wu_ref, wd_ref, o_ref,
                  xbuf, xn_ref, sem):
    i = pl.program_id(0)
    f = pl.program_id(1)
    MB = xbuf.shape[1]
    active = i < nu_ref[0]
    slot = i % 2

    def gather(blk, s):
        def issue(r, c):
            _row_copy(x_hbm, tok_ref[blk * MB + r], xbuf.at[s], r, sem.at[s]).start()
            return c

        lax.fori_loop(0, MB, issue, 0, unroll=8)

    @pl.when(f == 0)
    def _():
        o_ref[...] = jnp.zeros_like(o_ref)

    @pl.when((f == 0) & (i == 0) & active)
    def _():
        gather(0, 0)

    @pl.when((f == 0) & active)
    def _():
        pltpu.make_async_copy(x_hbm.at[pl.ds(0, MB), :], xbuf.at[slot], sem.at[slot]).wait()
        xn_ref[...] = _rms(xbuf[slot], g_ref[...]).astype(BF16)

    @pl.when((f == 1) & (i + 1 < nu_ref[0]))
    def _():
        gather(i + 1, 1 - slot)

    @pl.when(active)
    def _():
        xn = xn_ref[...]
        h = (_silu(_dot(xn, wg_ref[...])) * _dot(xn, wu_ref[...])).astype(BF16)
        o_ref[...] += _dot(h, wd_ref[...])


def _experts(x, g, tok, blk_e, n_used, wg, wu, wd, *, tf):
    T, D = x.shape
    MB = MOE_BLOCK
    n_blk = tok.shape[0] // MB
    Fh = wg.shape[2]
    assert Fh // tf >= 2, "the next block's rows are requested during hidden chunk 1"
    fe = lambda i, f, nu: jnp.where(i < nu[0], f, 0)
    return pl.pallas_call(
        _experts_body,
        grid_spec=pltpu.PrefetchScalarGridSpec(
            num_scalar_prefetch=3,
            grid=(n_blk, Fh // tf),
            in_specs=[pl.BlockSpec(memory_space=pl.ANY),
                      pl.BlockSpec((1, D), lambda i, f, tk, be, nu: (0, 0)),
                      pl.BlockSpec((None, D, tf), lambda i, f, tk, be, nu: (be[i], 0, fe(i, f, nu))),
                      pl.BlockSpec((None, D, tf), lambda i, f, tk, be, nu: (be[i], 0, fe(i, f, nu))),
                      pl.BlockSpec((None, tf, D), lambda i, f, tk, be, nu: (be[i], fe(i, f, nu), 0))],
            out_specs=pl.BlockSpec((MB, D), lambda i, f, tk, be, nu: (i, 0)),
            scratch_shapes=[pltpu.VMEM((2, MB, D), F32), pltpu.VMEM((MB, D), BF16),
                            pltpu.SemaphoreType.DMA((2,))]),
        out_shape=jax.ShapeDtypeStruct((n_blk * MB, D), F32),
        compiler_params=_cp(("arbitrary", "arbitrary"), 52),
        name="experts",
    )(tok, blk_e, n_used, x, g.reshape(1, D), wg, wu, wd)


def _combine_body(slot_ref, y_hbm, x_ref, r_ref, o_ref, y0, y1, sem):
    i = pl.program_id(0)
    tm = x_ref.shape[0]

    def issue(r, c):
        a = (i * tm + r) * TOP_K
        _row_copy(y_hbm, slot_ref[a], y0, r, sem).start()
        _row_copy(y_hbm, slot_ref[a + 1], y1, r, sem).start()
        return c

    lax.fori_loop(0, tm, issue, 0, unroll=8)
    pltpu.make_async_copy(y_hbm.at[pl.ds(0, tm), :], y0, sem).wait()
    pltpu.make_async_copy(y_hbm.at[pl.ds(0, tm), :], y1, sem).wait()
    route = r_ref[...]
    o_ref[...] = x_ref[...] + y0[...] * route[:, 2:3] + y1[...] * route[:, 3:4]


def _combine(x, y, route, slots, *, tm):
    T, D = x.shape
    return pl.pallas_call(
        _combine_body,
        grid_spec=pltpu.PrefetchScalarGridSpec(
            num_scalar_prefetch=1,
            grid=(T // tm,),
            in_specs=[pl.BlockSpec(memory_space=pl.ANY),
                      pl.BlockSpec((tm, D), lambda i, s: (i, 0)),
                      pl.BlockSpec((tm, LANES), lambda i, s: (i, 0))],
            out_specs=pl.BlockSpec((tm, D), lambda i, s: (i, 0)),
            scratch_shapes=[pltpu.VMEM((tm, D), F32), pltpu.VMEM((tm, D), F32),
                            pltpu.SemaphoreType.DMA(())]),
        out_shape=jax.ShapeDtypeStruct((T, D), F32),
        compiler_params=_cp(("arbitrary",), 32),
        name="combine",
    )(slots, y, x, route)


def _moe(x, g, w_router, wg, wu, wd):
    T, D = x.shape
    MB = MOE_BLOCK
    A = T * TOP_K
    route = _router(x, g, w_router, tm=512)
    flat_e = route[:, :TOP_K].astype(I32).reshape(A)
    onehot = (flat_e[:, None] == jnp.arange(N_EXPERTS, dtype=I32)[None, :]).astype(I32)
    csum = jnp.cumsum(onehot, axis=0)
    rank = jnp.take_along_axis(csum, flat_e[:, None], axis=1)[:, 0] - 1
    padded = (csum[-1] + MB - 1) // MB * MB
    pend = jnp.cumsum(padded)
    slots = (pend - padded)[flat_e] + rank
    n_blk = A // MB + N_EXPERTS
    tok = jnp.zeros((n_blk * MB,), I32).at[slots].set(jnp.arange(A, dtype=I32) // TOP_K)
    blk_start = jnp.arange(n_blk, dtype=I32) * MB
    blk_e = jnp.minimum(jnp.sum(blk_start[:, None] >= pend[None, :], axis=1), N_EXPERTS - 1).astype(I32)
    n_used = (pend[-1:] // MB).astype(I32)
    y = _experts(x, g, tok, blk_e, n_used, wg, wu, wd, tf=min(1024, wg.shape[2]))
    return _combine(x, y, route, slots.astype(I32), tm=256)


def _ab_in_weight(w_in):
    r, k, v, wl, al, gl, zq, zc, zqi, zki, zwi = jnp.split(
        w_in, np.cumsum([1024, 1024, 1024, 96, 96, 256, 1024, 256, 1024, 64]).tolist(), axis=1)
    D = w_in.shape[0]
    z = lambda n: jnp.zeros((D, n), w_in.dtype)
    cols = [r, k, v, zq, zqi, gl, zc, wl, z(32), al, z(32), zki, zki, zwi, z(AB_COLS - AB_ZWI - 16)]
    return jnp.concatenate(cols, axis=1).astype(BF16)


def kernel(x, p, norm_mix, norm_ffn, ab_w_in, ab_mu, rwkv_w0, rwkv_w_up, rwkv_a0, rwkv_a_up, rwkv_g_up, rwkv_k_k, rwkv_k_a, rwkv_r_k, rwkv_gn_w, rwkv_gn_b, dsa_ckv_norm, dsa_w_uk, dsa_w_uv, ab_w_out, rel_bias, ffn_w_gate, ffn_w_up, ffn_w_down, gdn_w_in, gdn_conv, gdn_a_log, gdn_dt_bias, gdn_out_norm, gdn_w_out, moe_router, moe_w_gate, moe_w_up, moe_w_down, ple_norm, ple_w_gate, ple_w_proj, final_norm):
    B, L, D = x.shape
    T = B * L
    xf = x.reshape(T, D)
    bf = lambda w: w.astype(BF16)

    z = _norm_mm(xf, norm_mix[0], _ab_in_weight(ab_w_in[0]), tm=1024, tn=1024)
    y_a = _rwkv(z, B, L, ab_mu[0], rwkv_w0[0], rwkv_w_up[0], rwkv_a0[0], rwkv_a_up[0], rwkv_g_up[0],
                rwkv_k_k[0], rwkv_k_a[0], rwkv_r_k[0], rwkv_gn_w[0], rwkv_gn_b[0])
    y_b = _dsa(z, B, L, dsa_ckv_norm[0], dsa_w_uk[0], dsa_w_uv[0], rel_bias)
    xf = _mm_res([y_a, y_b], bf(ab_w_out[0]), xf, tm=1024, tn=1024)
    xf = _ffn(xf, norm_ffn[0], bf(ffn_w_gate[0]), bf(ffn_w_up[0]), bf(ffn_w_down[0]), tm=512, tf=512)
    xf = _ple(xf, ple_norm[0], bf(ple_w_gate[0]), p[0].reshape(T, PLE_DIM), bf(ple_w_proj[0]),
              final_norm, final=False, tm=512)

    z = _norm_mm(xf, norm_mix[1], _gd_in_weight(gdn_w_in[0]), tm=1024, tn=512)
    gates = _gdn_gates(z, B, L, gdn_a_log[0], gdn_dt_bias[0])
    o = _gdn(z, gates, B, L, gdn_conv[0], gdn_out_norm[0])
    xf = _mm_res([o], bf(gdn_w_out[0]), xf, tm=512, tn=1024)
    xf = _moe(xf, norm_ffn[1], moe_router[0], bf(moe_w_gate[0]), bf(moe_w_up[0]), bf(moe_w_down[0]))
    xf = _ple(xf, ple_norm[1], bf(ple_w_gate[1]), p[1].reshape(T, PLE_DIM), bf(ple_w_proj[1]),
              final_norm, final=True, tm=512)
    return xf.reshape(B, L, D)
```

```python
import functools
import math

import numpy as np
import jax
import jax.numpy as jnp
from jax import lax
from jax.experimental import pallas as pl
from jax.experimental.pallas import tpu as pltpu

F32 = jnp.float32
BF16 = jnp.bfloat16
I32 = jnp.int32
I16 = jnp.int16
HALF16 = 1 << 15
HI = lax.Precision.HIGHEST

EPS = 1e-6
LANES = 128
MIB = 1024 * 1024

RWKV_H, RWKV_N = 16, 64
RWKV_W = RWKV_H * RWKV_N
RWKV_LORA = 96
RWKV_GATE = 256
RWKV_GN_EPS = 6.4e-4
RWKV_CHUNK = 64
RWKV_TB = 256

DSA_H, DSA_D, DSA_C = 8, 128, 256
IDX_H, IDX_D = 16, 64
TOPK_MAX = 256
QB = 128
KEY_GROUP = 4
REL_BUCKETS, REL_MAX_DIST = 32, 128
NEG = -1e30
INT_MIN = -(2 ** 31)

GDN_QK_H, GDN_V_H, GDN_D = 16, 32, 128
GDN_KW = GDN_QK_H * GDN_D
GDN_VW = GDN_V_H * GDN_D
GDN_CHUNK = 64
GDN_TB = 256
GDN_CONV_W = 4
GDN_HEADS_PER_STEP = 4

N_EXPERTS, TOP_K = 8, 2
MOE_BLOCK = 512
PLE_DIM = 256

AB_R, AB_K, AB_V = 0, 1024, 2048
AB_ZQ, AB_ZQI = 3072, 4096
AB_GL, AB_ZC = 5120, 5376
AB_WL, AB_AL = 5632, 5760
AB_ZKI, AB_ZWI = 5888, 6016
AB_COLS = 6144
GD_Q, GD_K, GD_V, GD_ZG, GD_GATES = 0, 2048, 4096, 8192, 12288
GD_COLS = 12800


def _cp(sem, vmem_mib):
    return pltpu.CompilerParams(dimension_semantics=sem, vmem_limit_bytes=vmem_mib * MIB)


def _dot(a, b, precision=None):
    return jnp.dot(a, b, preferred_element_type=F32, precision=precision)


def _dot_nt(a, b, precision=None):
    return lax.dot_general(a, b, (((1,), (1,)), ((), ())), preferred_element_type=F32,
                           precision=precision)


def _split(x):
    hi = x.astype(BF16)
    return hi, (x - hi.astype(F32)).astype(BF16)


def _seg_dot(x, ones_bf):
    hi, lo = _split(x)
    return _dot(hi, ones_bf) + _dot(lo, ones_bf)


def _rms(x, g):
    ms = jnp.mean(x * x, axis=-1, keepdims=True)
    return x * lax.rsqrt(ms + EPS) * g


def _sigmoid(x):
    return 0.5 * jnp.tanh(0.5 * x) + 0.5


def _silu(x):
    return x * _sigmoid(x)


def _softplus(x):
    return jnp.maximum(x, 0.0) + jnp.log1p(jnp.exp(-jnp.abs(x)))


def _iota(shape, dim):
    return lax.broadcasted_iota(I32, shape, dim)


def _norm_mm_body(x_ref, g_ref, w_ref, o_ref, xn_ref):
    @pl.when(pl.program_id(1) == 0)
    def _():
        xn_ref[...] = _rms(x_ref[...], g_ref[...]).astype(BF16)

    o_ref[...] = _dot(xn_ref[...], w_ref[...]).astype(o_ref.dtype)


def _norm_mm(x, g, w, *, tm, tn):
    T, K = x.shape
    N = w.shape[1]
    return pl.pallas_call(
        _norm_mm_body,
        grid=(T // tm, N // tn),
        in_specs=[pl.BlockSpec((tm, K), lambda i, j: (i, 0)),
                  pl.BlockSpec((1, K), lambda i, j: (0, 0)),
                  pl.BlockSpec((K, tn), lambda i, j: (0, j))],
        out_specs=pl.BlockSpec((tm, tn), lambda i, j: (i, j)),
        out_shape=jax.ShapeDtypeStruct((T, N), F32),
        scratch_shapes=[pltpu.VMEM((tm, K), BF16)],
        compiler_params=_cp(("parallel", "arbitrary"), 48),
        name="norm_mm",
    )(x, g.reshape(1, K), w)


def _mm_res_body(*refs):
    *a_refs, w_ref, r_ref, o_ref = refs
    acc = r_ref[...]
    off = 0
    for a_ref in a_refs:
        k = a_ref.shape[1]
        acc = acc + _dot(a_ref[...], w_ref[off:off + k, :])
        off += k
    o_ref[...] = acc


def _mm_res(parts, w, res, *, tm, tn):
    T = res.shape[0]
    K, N = w.shape
    assert sum(a.shape[1] for a in parts) == K
    return pl.pallas_call(
        _mm_res_body,
        grid=(T // tm, N // tn),
        in_specs=[pl.BlockSpec((tm, a.shape[1]), lambda i, j: (i, 0)) for a in parts]
        + [pl.BlockSpec((K, tn), lambda i, j: (0, j)),
           pl.BlockSpec((tm, tn), lambda i, j: (i, j))],
        out_specs=pl.BlockSpec((tm, tn), lambda i, j: (i, j)),
        out_shape=jax.ShapeDtypeStruct((T, N), F32),
        compiler_params=_cp(("parallel", "arbitrary"), 48),
        name="mm_res",
    )(*parts, w, res)


def _ple_body(x_ref, g_ref, wg_ref, p_ref, wp_ref, fg_ref, o_ref, *, final):
    x = x_ref[...]
    gate = _sigmoid(_dot(_rms(x, g_ref[...]).astype(BF16), wg_ref[...]))
    y = x + gate * _dot(p_ref[...].astype(BF16), wp_ref[...])
    if final:
        y = _rms(y, fg_ref[...])
    o_ref[...] = y


def _ple(x, g, wg, p, wp, fg, *, final, tm):
    T, D = x.shape
    P = p.shape[1]
    return pl.pallas_call(
        functools.partial(_ple_body, final=final),
        grid=(T // tm,),
        in_specs=[pl.BlockSpec((tm, D), lambda i: (i, 0)),
                  pl.BlockSpec((1, D), lambda i: (0, 0)),
                  pl.BlockSpec((D, D), lambda i: (0, 0)),
                  pl.BlockSpec((tm, P), lambda i: (i, 0)),
                  pl.BlockSpec((P, D), lambda i: (0, 0)),
                  pl.BlockSpec((1, D), lambda i: (0, 0))],
        out_specs=pl.BlockSpec((tm, D), lambda i: (i, 0)),
        out_shape=jax.ShapeDtypeStruct((T, D), F32),
        compiler_params=_cp(("parallel",), 48),
        name="ple",
    )(x, g.reshape(1, D), wg, p, wp, fg.reshape(1, D))


def _ffn_body(x_ref, g_ref, wg_ref, wu_ref, wd_ref, o_ref, xn_ref):
    @pl.when(pl.program_id(1) == 0)
    def _():
        x = x_ref[...]
        xn_ref[...] = _rms(x, g_ref[...]).astype(BF16)
        o_ref[...] = x

    xn = xn_ref[...]
    h = (_silu(_dot(xn, wg_ref[...])) * _dot(xn, wu_ref[...])).astype(BF16)
    o_ref[...] += _dot(h, wd_ref[...])


def _ffn(x, g, wg, wu, wd, *, tm, tf):
    T, D = x.shape
    Fh = wg.shape[1]
    return pl.pallas_call(
        _ffn_body,
        grid=(T // tm, Fh // tf),
        in_specs=[pl.BlockSpec((tm, D), lambda i, f: (i, 0)),
                  pl.BlockSpec((1, D), lambda i, f: (0, 0)),
                  pl.BlockSpec((D, tf), lambda i, f: (0, f)),
                  pl.BlockSpec((D, tf), lambda i, f: (0, f)),
                  pl.BlockSpec((tf, D), lambda i, f: (f, 0))],
        out_specs=pl.BlockSpec((tm, D), lambda i, f: (i, 0)),
        out_shape=jax.ShapeDtypeStruct((T, D), F32),
        scratch_shapes=[pltpu.VMEM((tm, D), BF16)],
        compiler_params=_cp(("parallel", "arbitrary"), 48),
        name="ffn",
    )(x, g.reshape(1, D), wg, wu, wd)


def _shift_mix(x, prev_row, mu):
    xs = pltpu.roll(x, 1, axis=0)
    xs = jnp.where(_iota(x.shape, 0) == 0, prev_row, xs)
    return x + (xs - x) * mu


def _bd(x, lo):
    return jnp.concatenate([jnp.where(lo, x, 0.0), jnp.where(lo, 0.0, x)], axis=0)


def _unit_lower_inverses(xs, eye, steps):
    ps = [eye + x for x in xs]
    xps = list(xs)
    for _ in range(steps):
        xbs = [xp.astype(BF16) for xp in xps]
        xps = [_dot(xb, xb) for xb in xbs]
        ps = [_dot(p.astype(BF16), (eye + xp).astype(BF16)) for p, xp in zip(ps, xps)]
    return ps


def _rwkv_body(rkv_ref, gl_ref, wa_ref, mu_rkv_ref, mu_gl_ref, mu_wa_ref, w0_ref, wup_ref,
               a0_ref, aup_ref, gup_ref, kk_ref, ka_ref, rk_ref, gnw_ref, gnb_ref, o_ref,
               prev_rkv, prev_gl, prev_wa, state, r_s, k_s, v_s, kn_s, a_s, lw_s, cg_s, g_s, y_s):
    TB = rkv_ref.shape[0]
    C = RWKV_CHUNK
    NP = RWKV_W // LANES

    @pl.when(pl.program_id(1) == 0)
    def _():
        prev_rkv[...] = jnp.zeros_like(prev_rkv)
        prev_gl[...] = jnp.zeros_like(prev_gl)
        prev_wa[...] = jnp.zeros_like(prev_wa)
        state[...] = jnp.zeros_like(state)

    lane = _iota((1, LANES), 1)
    lo = lane < RWKV_N
    r128 = _iota((LANES, LANES), 0)
    c128 = _iota((LANES, LANES), 1)
    same = (r128 // C) == (c128 // C)
    seg_ones = jnp.where(same, 1.0, 0.0).astype(BF16)
    eye = jnp.where(r128 == c128, 1.0, 0.0).astype(F32)
    strict = same & (c128 < r128)
    incl = same & (c128 <= r128)
    rt = _iota((TB, TB), 0)
    ct = _iota((TB, TB), 1)
    tri_chunks = jnp.where(((rt // C) == (ct // C)) & (ct <= rt), 1.0, 0.0).astype(BF16)

    wa_raw = wa_ref[...]
    wa = _shift_mix(wa_raw, prev_wa[...], mu_wa_ref[...])
    prev_wa[...] = wa_raw[TB - 1:TB, :]
    wl = jnp.tanh(wa[:, :LANES]).astype(BF16)
    al = wa[:, LANES:].astype(BF16)
    w = -_softplus(-(w0_ref[...] + _dot(wl, wup_ref[...]))) - 0.5
    lw = -jnp.exp(w)
    lw_s[...] = lw
    lw_hi, lw_lo = _split(lw)
    cg_s[...] = _dot(tri_chunks, lw_hi) + _dot(tri_chunks, lw_lo)
    a_s[...] = _sigmoid(a0_ref[...] + _dot(al, aup_ref[...]))
    gl_raw = gl_ref[...]
    gl = _shift_mix(gl_raw, prev_gl[...], mu_gl_ref[...])
    prev_gl[...] = gl_raw[TB - 1:TB, :]
    g_s[...] = _dot(_sigmoid(gl).astype(BF16), gup_ref[...])

    for p in range(NP):
        cs = slice(p * LANES, (p + 1) * LANES)
        cols = [slice(off + p * LANES, off + (p + 1) * LANES) for off in (AB_R, AB_K, AB_V)]
        mixed = []
        for c in cols:
            raw = rkv_ref[:, c]
            mixed.append(_shift_mix(raw, prev_rkv[:, c], mu_rkv_ref[:, c]))
            prev_rkv[:, c] = raw[TB - 1:TB, :]
        r, k, v = mixed
        kk = k * kk_ref[:, cs]
        kn_s[:, cs] = kk * lax.rsqrt(_seg_dot(kk * kk, seg_ones) + 1e-6)
        r_s[:, cs] = r
        k_s[:, cs] = k * (1.0 + (a_s[:, cs] - 1.0) * ka_ref[:, cs])
        v_s[:, cs] = v

    def chunk(c, carry):
        rows = pl.ds(pl.multiple_of(c * C, C), C)
        pairs = range(NP)
        css = [slice(p * LANES, (p + 1) * LANES) for p in pairs]
        gam_last, ar, bk, bk_end, v_f, v_t = [], [], [], [], [], []
        for cs in css:
            cg = cg_s[rows, cs]
            gam = jnp.exp(cg)
            gam_inv = jnp.exp(-cg)
            gam_prev = jnp.exp(cg - lw_s[rows, cs])
            gl_ = gam[C - 1:C, :]
            kn = kn_s[rows, cs]
            b_raw = kn * a_s[rows, cs] * gam_inv
            k_raw = k_s[rows, cs] * gam_inv
            gam_last.append(gl_)
            ar.append(jnp.concatenate([_bd(-kn * gam_prev, lo), _bd(r_s[rows, cs] * gam, lo)],
                                      axis=0).astype(BF16))
            bk.append(jnp.concatenate([_bd(b_raw, lo), _bd(k_raw, lo)], axis=0).astype(BF16))
            bk_end.append(jnp.concatenate([_bd(b_raw * gl_, lo), _bd(k_raw * gl_, lo)],
                                          axis=0).astype(BF16))
            vf = _bd(v_s[rows, cs], lo)
            v_f.append(vf)
            v_t.append(vf.astype(BF16))
        score = [_dot_nt(ar[p], bk[p]) for p in pairs]
        a_ab = [jnp.where(strict, s[:LANES, :LANES], 0.0) for s in score]
        a_akv = [_dot(jnp.where(strict, score[p][:LANES, LANES:], 0.0).astype(BF16), v_t[p])
                 for p in pairs]
        r_abk = [jnp.concatenate([jnp.where(incl, s[LANES:, :LANES], 0.0),
                                  jnp.where(incl, s[LANES:, LANES:], 0.0)], axis=1).astype(BF16)
                 for s in score]
        t_inv = [t.astype(BF16) for t in _unit_lower_inverses(a_ab, eye, 5)]
        s_old = [state[p] for p in pairs]
        sproj = [_dot_nt(ar[p], s_old[p].astype(BF16)) for p in pairs]
        u = [_dot(t_inv[p], (sproj[p][:LANES] + a_akv[p]).astype(BF16)) for p in pairs]
        y_bd = [sproj[p][LANES:] + _dot(r_abk[p], jnp.concatenate([u[p].astype(BF16), v_t[p]], axis=0))
                for p in pairs]
        for p in pairs:
            y_s[rows, css[p]] = y_bd[p][:C] + y_bd[p][C:]
        uvT = [jnp.concatenate([u[p].T, v_f[p].T], axis=1).astype(BF16) for p in pairs]
        for p in pairs:
            state[p] = s_old[p] * gam_last[p] + _dot(uvT[p], bk_end[p])
        return carry

    lax.fori_loop(0, TB // C, chunk, 0)

    for p in range(NP):
        cs = slice(p * LANES, (p + 1) * LANES)
        y = y_s[:, cs]
        mean = _seg_dot(y, seg_ones) * (1.0 / RWKV_N)
        d = y - mean
        var = _seg_dot(d * d, seg_ones) * (1.0 / RWKV_N)
        yn = d * lax.rsqrt(var + RWKV_GN_EPS) * gnw_ref[:, cs] + gnb_ref[:, cs]
        bonus = _seg_dot(r_s[:, cs] * k_s[:, cs] * rk_ref[:, cs], seg_ones) * v_s[:, cs]
        o_ref[:, cs] = ((yn + bonus) * g_s[:, cs]).astype(o_ref.dtype)


def _rwkv(z, B, L, mu, w0, w_up, a0, a_up, g_up, k_k, k_a, r_k, gn_w, gn_b):
    TB = RWKV_TB
    nb = L // TB
    W = RWKV_W
    mu_r, mu_k, mu_v, mu_wl, mu_al, mu_gl = jnp.split(
        mu, np.cumsum([W, W, W, RWKV_LORA, RWKV_LORA])[:].tolist())
    pad = LANES - RWKV_LORA
    mu_rkv = jnp.concatenate([mu_r, mu_k, mu_v]).reshape(1, 3 * W)
    mu_wa = jnp.concatenate([jnp.pad(mu_wl, (0, pad)), jnp.pad(mu_al, (0, pad))]).reshape(1, 2 * LANES)
    wup = jnp.pad(w_up, ((0, pad), (0, 0))).astype(BF16)
    aup = jnp.pad(a_up, ((0, pad), (0, 0))).astype(BF16)
    row = lambda t: t.reshape(1, W)
    vec = lambda n: pl.BlockSpec((1, n), lambda b, i: (0, 0))
    big = lambda: pltpu.VMEM((TB, W), F32)
    return pl.pallas_call(
        _rwkv_body,
        grid=(B, nb),
        in_specs=[pl.BlockSpec((TB, 3 * W), lambda b, i: (b * nb + i, 0)),
                  pl.BlockSpec((TB, RWKV_GATE), lambda b, i: (b * nb + i, AB_GL // RWKV_GATE)),
                  pl.BlockSpec((TB, 2 * LANES), lambda b, i: (b * nb + i, AB_WL // (2 * LANES))),
                  vec(3 * W), vec(RWKV_GATE), vec(2 * LANES), vec(W),
                  pl.BlockSpec((LANES, W), lambda b, i: (0, 0)),
                  vec(W),
                  pl.BlockSpec((LANES, W), lambda b, i: (0, 0)),
                  pl.BlockSpec((RWKV_GATE, W), lambda b, i: (0, 0)),
                  vec(W), vec(W), vec(W), vec(W), vec(W)],
        out_specs=pl.BlockSpec((TB, W), lambda b, i: (b * nb + i, 0)),
        out_shape=jax.ShapeDtypeStruct((B * L, W), BF16),
        scratch_shapes=[pltpu.VMEM((1, 3 * W), F32), pltpu.VMEM((1, RWKV_GATE), F32),
                        pltpu.VMEM((1, 2 * LANES), F32),
                        pltpu.VMEM((W // LANES, LANES, LANES), F32),
                        big(), big(), big(), big(), big(), big(), big(), big(), big()],
        compiler_params=_cp(("arbitrary", "arbitrary"), 48),
        name="rwkv7",
    )(z, z, z, mu_rkv, mu_gl.reshape(1, RWKV_GATE), mu_wa, row(w0), wup, row(a0), aup,
      g_up.astype(BF16), row(k_k), row(k_a), row(r_k), row(gn_w), row(gn_b))


def _t5_bucket_np(dist):
    n = np.maximum(dist, 0)
    exact = REL_BUCKETS // 2
    ratio = np.log(np.maximum(n, 1).astype(np.float32) / np.float32(exact)) / np.float32(
        math.log(REL_MAX_DIST / exact))
    large = exact + (ratio.astype(np.float32) * np.float32(REL_BUCKETS - exact)).astype(np.int32)
    return np.where(n < exact, n, np.minimum(large, REL_BUCKETS - 1)).astype(np.int32)


def _near_buckets():
    kl = np.arange(QB)[:, None]
    ql = np.arange(QB)[None, :]
    return np.stack([_t5_bucket_np(ql - kl), _t5_bucket_np(QB + ql - kl)])


def _dsa_body(tbl_ref, bkt_ref, zq_ref, zqi_ref, zc_ref, zki_ref, zwi_ref, cn_ref, wukT_ref,
              wuvT_ref, o_ref, c_all, cT_all, kibd_all, sc, qiT, qlatT, bias, m_s, l_s, alpha_s,
              oT, hi_all, lo_all, lom_all, *, topk):
    b = pl.program_id(0)
    qb = pl.program_id(1)
    lo = _iota((1, LANES), 1) < IDX_D
    krow = _iota((QB, QB), 0)
    qcol = _iota((QB, QB), 1)
    hsl = [slice(h * QB, (h + 1) * QB) for h in range(DSA_H)]

    @pl.when((b == 0) & (qb == 0))
    def _():
        for t in range(2):
            bk = bkt_ref[t]
            for h in range(DSA_H):
                far = tbl_ref[REL_BUCKETS - 1, h]
                acc = jnp.zeros((QB, QB), F32)
                for bb in range(REL_BUCKETS - 1):
                    acc = jnp.where(bk == bb, tbl_ref[bb, h] - far, acc)
                bias[t, h] = acc

    c_new = _rms(zc_ref[...], cn_ref[...])
    c_all[qb] = c_new.astype(BF16)
    cT_all[qb] = c_new.T.astype(BF16)
    kibd_all[qb] = _bd(zki_ref[...], lo).astype(BF16)

    for p in range(IDX_H // 2):
        qiT[:, p * QB:(p + 1) * QB] = zqi_ref[:, p * LANES:(p + 1) * LANES].T.astype(BF16)
    wT = zwi_ref[...].T * (IDX_D ** -0.5 * IDX_H ** -0.5)
    w_rows = [wT[h:h + 1, :] for h in range(IDX_H)]
    for h in range(DSA_H):
        qhT = zq_ref[:, hsl[h]].T.astype(BF16)
        qlatT[:, hsl[h]] = (_dot(wukT_ref[h], qhT) * DSA_D ** -0.5).astype(BF16)

    def score_blocks(j, nk):
        kb = kibd_all[pl.ds(j, nk)].reshape(nk * 2 * QB, LANES)
        acc = [jnp.zeros((QB, QB), F32) for _ in range(nk)]
        for pp in range(IDX_H // 4):
            s = _dot(kb, qiT[:, 2 * pp * QB:2 * (pp + 1) * QB])
            for k in range(nk):
                even = s[2 * k * QB:(2 * k + 1) * QB]
                odd = s[(2 * k + 1) * QB:(2 * k + 2) * QB]
                acc[k] = (acc[k] + w_rows[4 * pp] * jnp.maximum(even[:, :QB], 0.0)
                          + w_rows[4 * pp + 1] * jnp.maximum(odd[:, :QB], 0.0)
                          + w_rows[4 * pp + 2] * jnp.maximum(even[:, QB:], 0.0)
                          + w_rows[4 * pp + 3] * jnp.maximum(odd[:, QB:], 0.0))
        for k in range(nk):
            bits = lax.bitcast_convert_type(acc[k], I32)
            key = bits ^ ((bits >> 31) & 0x7FFFFFFF)
            key = jnp.where((j + k == qb) & (krow > qcol), INT_MIN, key)
            sc[j + k] = key
            hi_all[j + k] = (key >> 16).astype(I16)
            lo_all[j + k] = ((key & 0xFFFF) - HALF16).astype(I16)

    def score_group(jj, carry):
        score_blocks(jj * KEY_GROUP, KEY_GROUP)
        return carry

    def score_single(j, carry):
        score_blocks(j, 1)
        return carry

    n_sgroup = (qb + 1) // KEY_GROUP
    lax.fori_loop(0, n_sgroup, score_group, 0)
    lax.fori_loop(n_sgroup * KEY_GROUP, qb + 1, score_single, 0)

    one16 = jnp.ones((), I16)
    zero16 = jnp.zeros((), I16)
    n_quad = (qb + 1) // KEY_GROUP

    def over_blocks(visit):
        def quad(jj, a):
            for k in range(KEY_GROUP):
                a = a + visit(jj * KEY_GROUP + k)
            return a

        def single(j, a):
            return a + visit(j)

        a = lax.fori_loop(0, n_quad, quad, jnp.zeros((QB, QB), I16))
        a = lax.fori_loop(n_quad * KEY_GROUP, qb + 1, single, a)
        return jnp.sum(a.astype(I32), axis=0, keepdims=True)

    def largest_digit(arr, need):
        def body(i, u):
            cand_u = u | jnp.left_shift(jnp.int32(1), 15 - i)
            c16 = (cand_u - HALF16).astype(I16)
            tot = over_blocks(lambda j: jnp.where(arr[j] >= c16, one16, zero16))
            return jnp.where(tot >= need, cand_u, u)

        return lax.fori_loop(0, 16, body, jnp.zeros((1, QB), I32)) - HALF16

    hi = largest_digit(hi_all, topk)
    hi16 = hi.astype(I16)

    def above_and_mask(j):
        blk = hi_all[j]
        lom_all[j] = jnp.where(blk == hi16, lo_all[j], jnp.full((), -HALF16, I16))
        return jnp.where(blk > hi16, one16, zero16)

    n_above_hi = over_blocks(above_and_mask)
    lo = largest_digit(lom_all, topk - n_above_hi)
    lo16 = lo.astype(I16)
    thr_raw = hi * (2 * HALF16) + (lo + HALF16)
    thr = jnp.maximum(thr_raw, INT_MIN + 1)

    n_gt = n_above_hi + over_blocks(lambda j: jnp.where(lom_all[j] > lo16, one16, zero16))
    n_ge = n_above_hi + over_blocks(lambda j: jnp.where(lom_all[j] >= lo16, one16, zero16))

    @pl.when(jnp.max(n_ge) > topk)
    def _():
        keep = (topk - n_gt).astype(F32)
        tri = jnp.where(qcol <= krow, 1.0, 0.0).astype(BF16)

        def drop(j, seen):
            key = sc[j]
            tied = key == thr_raw
            tied_bf = jnp.where(tied, 1.0, 0.0).astype(BF16)
            rank = seen + _dot(tri, tied_bf)
            sc[j] = jnp.where(tied & (rank > keep), INT_MIN, key)
            return seen + jnp.sum(tied_bf.astype(F32), axis=0, keepdims=True)

        lax.fori_loop(0, qb + 1, drop, jnp.zeros((1, QB), F32))

    m_s[...] = jnp.full(m_s.shape, NEG, F32)
    l_s[...] = jnp.zeros_like(l_s)
    oT[...] = jnp.zeros_like(oT)

    def attend(j, nk, near):
        sel = sc[pl.ds(j, nk)].reshape(nk * QB, QB) >= thr
        lg = _dot(c_all[pl.ds(j, nk)].reshape(nk * QB, DSA_C), qlatT[...])
        cT = jnp.concatenate([cT_all[j + k] for k in range(nk)], axis=1)
        prs = []
        for h in range(DSA_H):
            lgh = lg[:, hsl[h]]
            if near is not None:
                lgh = lgh + bias[near, h]
            lgh = jnp.where(sel, lgh, NEG)
            m_old = m_s[:, hsl[h]]
            m_new = jnp.maximum(m_old, jnp.max(lgh, axis=0, keepdims=True))
            pr = jnp.exp(lgh - m_new)
            alpha_s[:, hsl[h]] = jnp.exp(m_old - m_new)
            l_s[:, hsl[h]] = alpha_s[:, hsl[h]] * l_s[:, hsl[h]] + jnp.sum(pr, axis=0, keepdims=True)
            m_s[:, hsl[h]] = m_new
            prs.append(pr.astype(BF16))
        oT[...] = alpha_s[...] * oT[...] + _dot(cT, jnp.concatenate(prs, axis=1))

    def far_group(jj, carry):
        attend(jj * KEY_GROUP, KEY_GROUP, None)
        return carry

    def far_single(j, carry):
        attend(j, 1, None)
        return carry

    n_far = jnp.maximum(qb - 1, 0)
    n_group = n_far // KEY_GROUP
    lax.fori_loop(0, n_group, far_group, 0)
    lax.fori_loop(n_group * KEY_GROUP, n_far, far_single, 0)

    @pl.when(qb >= 1)
    def _():
        attend(qb - 1, 1, 1)

    attend(qb, 1, 0)

    inv_l = 1.0 / l_s[...]
    for h in range(DSA_H):
        oh = (oT[:, hsl[h]] * inv_l[:, hsl[h]]).astype(BF16)
        o_ref[:, hsl[h]] = _dot(wuvT_ref[h], oh).T.astype(o_ref.dtype)


def _dsa(z, B, L, ckv_norm, w_uk, w_uv, rel_bias):
    nq = L // QB
    topk = min(TOPK_MAX, L // 4)
    wukT = jnp.swapaxes(w_uk, 1, 2).astype(BF16)
    wuvT = jnp.swapaxes(w_uv, 1, 2).astype(BF16)
    blk = lambda w, off: pl.BlockSpec((QB, w), lambda b, q: (b * nq + q, off // w))
    full = lambda shape: pl.BlockSpec(shape, lambda b, q: (0,) * len(shape))
    W = DSA_H * DSA_D
    return pl.pallas_call(
        functools.partial(_dsa_body, topk=topk),
        grid=(B, nq),
        in_specs=[pl.BlockSpec(memory_space=pltpu.SMEM),
                  full((2, QB, QB)),
                  blk(W, AB_ZQ), blk(IDX_H * IDX_D, AB_ZQI), blk(DSA_C, AB_ZC),
                  blk(LANES, AB_ZKI), blk(LANES, AB_ZWI),
                  full((1, DSA_C)), full((DSA_H, DSA_C, DSA_D)), full((DSA_H, DSA_D, DSA_C))],
        out_specs=pl.BlockSpec((QB, W), lambda b, q: (b * nq + q, 0)),
        out_shape=jax.ShapeDtypeStruct((B * L, W), BF16),
        scratch_shapes=[pltpu.VMEM((nq, QB, DSA_C), BF16), pltpu.VMEM((nq, DSA_C, QB), BF16),
                        pltpu.VMEM((nq, 2 * QB, LANES), BF16), pltpu.VMEM((nq, QB, QB), I32),
                        pltpu.VMEM((LANES, IDX_H // 2 * QB), BF16), pltpu.VMEM((DSA_C, W), BF16),
                        pltpu.VMEM((2, DSA_H, QB, QB), F32),
                        pltpu.VMEM((1, W), F32), pltpu.VMEM((1, W), F32), pltpu.VMEM((1, W), F32),
                        pltpu.VMEM((DSA_C, W), F32),
                        pltpu.VMEM((nq, QB, QB), I16), pltpu.VMEM((nq, QB, QB), I16),
                        pltpu.VMEM((nq, QB, QB), I16)],
        compiler_params=_cp(("arbitrary", "arbitrary"), 48),
        name="dsa",
    )(rel_bias, jnp.asarray(_near_buckets()), z, z, z, z, z, ckv_norm.reshape(1, DSA_C), wukT, wuvT)


def _gdn_gates_body(z_ref, alog_ref, dtb_ref, o_ref):
    TB = z_ref.shape[0]
    C = GDN_CHUNK
    z = z_ref[...]
    sub = _iota((1, LANES), 1) % 8
    beta = _sigmoid(z)
    g = -jnp.exp(alog_ref[...]) * _softplus(z + dtb_ref[...])
    rt = _iota((TB, TB), 0)
    ct = _iota((TB, TB), 1)
    same = (rt // C) == (ct // C)
    cum = _dot(jnp.where(same & (ct <= rt), 1.0, 0.0).astype(F32), g, HI)
    tot = _dot(jnp.where(same, 1.0, 0.0).astype(F32), g, HI)
    tile = jnp.where(sub < 2, beta, jnp.where(sub < 4, g, jnp.where(sub < 6, cum, tot)))
    o_ref[...] = tile.T


def _gdn_gates(z, B, L, a_log, dt_bias):
    TB = GDN_TB
    nb = L // TB
    spread = lambda t: jnp.zeros((GDN_QK_H, 8), F32).at[:, 2:].set(
        jnp.tile(t.reshape(GDN_QK_H, 2), (1, 3))).reshape(1, LANES)
    return pl.pallas_call(
        _gdn_gates_body,
        grid=(B, nb),
        in_specs=[pl.BlockSpec((TB, LANES), lambda b, i: (b * nb + i, GD_GATES // LANES)),
                  pl.BlockSpec((1, LANES), lambda b, i: (0, 0)),
                  pl.BlockSpec((1, LANES), lambda b, i: (0, 0))],
        out_specs=pl.BlockSpec((None, LANES, TB), lambda b, i: (b, 0, i)),
        out_shape=jax.ShapeDtypeStruct((B, LANES, L), F32),
        compiler_params=_cp(("parallel", "parallel"), 32),
        name="gdn_gates",
    )(z, spread(a_log), spread(dt_bias))


def _gdn_body(zq_ref, zk_ref, zv_ref, zg_ref, gates_ref, cwq_ref, cwk_ref, cwv_ref, on_ref, o_ref,
              xbuf, state, q_s, k_s, v_s, *, hg):
    TB = zq_ref.shape[0]
    C = GDN_CHUNK
    D = GDN_D
    KW = GDN_CONV_W
    W = 4 * D * hg

    @pl.when(pl.program_id(2) == 0)
    def _():
        xbuf[0:8, :] = jnp.zeros((8, W), F32)
        state[...] = jnp.zeros_like(state)

    xbuf[8:TB + 8, 0:D * hg] = zq_ref[...]
    xbuf[8:TB + 8, D * hg:2 * D * hg] = zk_ref[...]
    xbuf[8:TB + 8, 2 * D * hg:W] = zv_ref[...]
    cw = jnp.concatenate([cwq_ref[...], cwk_ref[...], cwv_ref[...]], axis=1)
    l2 = lambda t: t * lax.rsqrt(jnp.sum(t * t, axis=-1, keepdims=True) + 1e-6)
    for g in range(W // D):
        cols = slice(g * D, (g + 1) * D)
        acc = jnp.zeros((TB, D), F32)
        for j in range(KW):
            acc = acc + cw[j:j + 1, cols] * xbuf[8 - (KW - 1) + j:8 - (KW - 1) + j + TB, cols]
        act = _silu(acc)
        if g < hg:
            q_s[:, cols] = l2(act) * D ** -0.5
        elif g < 2 * hg:
            k_s[:, (g - hg) * D:(g - hg + 1) * D] = l2(act)
        else:
            v_s[:, (g - 2 * hg) * D:(g - 2 * hg + 1) * D] = act
    xbuf[0:8, :] = xbuf[TB:TB + 8, :]

    lane = _iota((1, LANES), 1)
    lo = lane < C
    r128 = _iota((LANES, LANES), 0)
    c128 = _iota((LANES, LANES), 1)
    same = (r128 // C) == (c128 // C)
    eye_m = r128 == c128
    eye = jnp.where(eye_m, 1.0, 0.0).astype(F32)
    strict = same & (c128 < r128)
    incl = same & (c128 <= r128)

    heads = range(hg)
    for c in range(TB // C):
        rows = slice(c * C, (c + 1) * C)
        beta_st, e_st, fb_st, etot_cat, kc, qc, dec = [], [], [], [], [], [], []
        for h in heads:
            win = gates_ref[8 * h:8 * h + 8, (c // 2) * LANES:(c // 2 + 1) * LANES]
            win_sw = pltpu.roll(win, C, axis=1)
            first, second = (win, win_sw) if c % 2 == 0 else (win_sw, win)
            st = lambda r: jnp.where(lo, first[r:r + 1, :], second[r + 1:r + 2, :])
            beta, cum, tot = st(0), st(4), st(6)
            etot = jnp.exp(tot)
            etot_sw = pltpu.roll(etot, C, axis=1)
            beta_st.append(beta)
            e_st.append(jnp.exp(cum))
            fb_st.append(jnp.exp(tot - cum) * beta)
            etot_cat.append(jnp.concatenate([jnp.where(lo, etot, etot_sw),
                                             jnp.where(lo, etot_sw, etot)], axis=1))
            kc.append(k_s[rows, h * D:(h + 1) * D])
            qc.append(q_s[rows, h * D:(h + 1) * D])
            cum_b = jnp.broadcast_to(cum, (LANES, LANES))
            dec.append(jnp.exp(jnp.where(incl, cum_b.T - cum_b, 0.0)))
        score = [_dot_nt(jnp.concatenate([kc[h], kc[h], qc[h], qc[h]], axis=0).astype(BF16),
                         jnp.concatenate([kc[h], kc[h]], axis=0).astype(BF16))
                 for h in heads]
        ab = [jnp.where(strict, score[h][:LANES] * dec[h], 0.0) * beta_st[h] for h in heads]
        t_inv = _unit_lower_inverses([-a for a in ab], eye, 5)
        t_cat = [jnp.concatenate([t_inv[h], -(t_inv[h] * e_st[h])], axis=1).astype(BF16) for h in heads]
        a_cat = [jnp.concatenate([eye * e_st[h],
                                  jnp.where(incl, score[h][LANES:] * dec[h], 0.0) * beta_st[h]],
                                 axis=1).astype(BF16) for h in heads]
        s_old = [state[h] for h in heads]
        proj = [_dot(jnp.concatenate([kc[h], qc[h]], axis=0).astype(BF16), s_old[h].astype(BF16))
                for h in heads]
        rhs = [jnp.concatenate([v_s[rows, 2 * h * D:(2 * h + 1) * D],
                                v_s[rows, (2 * h + 1) * D:(2 * h + 2) * D],
                                proj[h][:C, :D], proj[h][:C, D:]], axis=0).astype(BF16) for h in heads]
        vn = [_dot(t_cat[h], rhs[h]) for h in heads]
        vn_bf = [x.astype(BF16) for x in vn]
        o_st = [_dot(a_cat[h], jnp.concatenate(
            [proj[h][C:, :D].astype(BF16), proj[h][C:, D:].astype(BF16), vn_bf[h]], axis=0))
            for h in heads]
        fv = [_dot((eye * fb_st[h]).astype(BF16), vn_bf[h]) for h in heads]
        for h in heads:
            state[h] = s_old[h] * etot_cat[h] + _dot(
                kc[h].T.astype(BF16), jnp.concatenate([fv[h][:C], fv[h][C:]], axis=1).astype(BF16))
        for h in heads:
            for u in range(2):
                cols = slice((2 * h + u) * D, (2 * h + u + 1) * D)
                oh = o_st[h][u * C:(u + 1) * C]
                o_ref[rows, cols] = (_rms(oh, on_ref[...]) * _silu(zg_ref[rows, cols])).astype(o_ref.dtype)


def _gdn(z, gates, B, L, conv_w, out_norm):
    TB = GDN_TB
    nb = L // TB
    D = GDN_D
    hg = GDN_HEADS_PER_STEP
    zspec = lambda w, off: pl.BlockSpec((TB, w), lambda b, h, i: (b * nb + i, off // w + h))
    cspec = lambda w, off: pl.BlockSpec((GDN_CONV_W, w), lambda b, h, i: (0, off // w + h))
    return pl.pallas_call(
        functools.partial(_gdn_body, hg=hg),
        grid=(B, GDN_QK_H // hg, nb),
        in_specs=[zspec(D * hg, GD_Q), zspec(D * hg, GD_K), zspec(2 * D * hg, GD_V),
                  zspec(2 * D * hg, GD_ZG),
                  pl.BlockSpec((None, 8 * hg, TB), lambda b, h, i: (b, h, i)),
                  cspec(D * hg, GD_Q), cspec(D * hg, GD_K), cspec(2 * D * hg, GD_V),
                  pl.BlockSpec((1, D), lambda b, h, i: (0, 0))],
        out_specs=pl.BlockSpec((TB, 2 * D * hg), lambda b, h, i: (b * nb + i, h)),
        out_shape=jax.ShapeDtypeStruct((B * L, GDN_VW), BF16),
        scratch_shapes=[pltpu.VMEM((TB + 8, 4 * D * hg), F32), pltpu.VMEM((hg, D, 2 * D), F32),
                        pltpu.VMEM((TB, D * hg), F32), pltpu.VMEM((TB, D * hg), F32),
                        pltpu.VMEM((TB, 2 * D * hg), F32)],
        compiler_params=_cp(("parallel", "parallel", "arbitrary"), 40),
        name="gdn",
    )(z, z, z, z, gates, conv_w, conv_w, conv_w, out_norm.reshape(1, D))


def _gd_in_weight(w_in):
    qkv, zg, b, a = jnp.split(w_in, np.cumsum([2 * GDN_KW + GDN_VW, GDN_VW, GDN_V_H]).tolist(), axis=1)
    D = w_in.shape[0]
    pair = lambda t: t.reshape(D, GDN_QK_H, 2)
    gates = jnp.concatenate([pair(b), pair(a), pair(a), pair(a)], axis=2).reshape(D, LANES)
    pad = jnp.zeros((D, GD_COLS - GD_GATES - LANES), w_in.dtype)
    return jnp.concatenate([qkv, zg, gates, pad], axis=1).astype(BF16)


def _router_body(x_ref, g_ref, wr_ref, o_ref):
    logits = _dot(_rms(x_ref[...], g_ref[...]), wr_ref[...], HI)
    lane = _iota(logits.shape, 1)
    logits = jnp.where(lane < N_EXPERTS, logits, -jnp.inf)
    m1 = jnp.max(logits, axis=-1, keepdims=True)
    i1 = jnp.min(jnp.where(logits == m1, lane, LANES), axis=-1, keepdims=True)
    rest = jnp.where(lane == i1, -jnp.inf, logits)
    m2 = jnp.max(rest, axis=-1, keepdims=True)
    i2 = jnp.min(jnp.where(rest == m2, lane, LANES), axis=-1, keepdims=True)
    e = jnp.exp(m2 - m1)
    w1 = 1.0 / (1.0 + e)
    o_ref[...] = jnp.where(lane == 0, i1.astype(F32),
                           jnp.where(lane == 1, i2.astype(F32),
                                     jnp.where(lane == 2, w1, jnp.where(lane == 3, e * w1, 0.0))))


def _router(x, g, w_router, *, tm):
    T, D = x.shape
    wr = jnp.pad(w_router, ((0, 0), (0, LANES - N_EXPERTS)))
    return pl.pallas_call(
        _router_body,
        grid=(T // tm,),
        in_specs=[pl.BlockSpec((tm, D), lambda i: (i, 0)),
                  pl.BlockSpec((1, D), lambda i: (0, 0)),
                  pl.BlockSpec((D, LANES), lambda i: (0, 0))],
        out_specs=pl.BlockSpec((tm, LANES), lambda i: (i, 0)),
        out_shape=jax.ShapeDtypeStruct((T, LANES), F32),
        compiler_params=_cp(("parallel",), 32),
        name="router",
    )(x, g.reshape(1, D), wr)


def _row_copy(src_hbm, row, dst, r, sem):
    return pltpu.make_async_copy(src_hbm.at[pl.ds(row, 1), :], dst.at[pl.ds(r, 1), :], sem)


def _experts_body(tok_ref, be_ref, nu_ref, x_hbm, g_ref, wg_ref, wu_ref, wd_ref, o_ref,
                  xbuf, xn_ref, sem):
    i = pl.program_id(0)
    f = pl.program_id(1)
    MB = xbuf.shape[1]
    active = i < nu_ref[0]
    slot = i % 2

    def gather(blk, s):
        def issue(r, c):
            _row_copy(x_hbm, tok_ref[blk * MB + r], xbuf.at[s], r, sem.at[s]).start()
            return c

        lax.fori_loop(0, MB, issue, 0, unroll=8)

    @pl.when(f == 0)
    def _():
        o_ref[...] = jnp.zeros_like(o_ref)

    @pl.when((f == 0) & (i == 0) & active)
    def _():
        gather(0, 0)

    @pl.when((f == 0) & active)
    def _():
        pltpu.make_async_copy(x_hbm.at[pl.ds(0, MB), :], xbuf.at[slot], sem.at[slot]).wait()
        xn_ref[...] = _rms(xbuf[slot], g_ref[...]).astype(BF16)

    @pl.when((f == 1) & (i + 1 < nu_ref[0]))
    def _():
        gather(i + 1, 1 - slot)

    @pl.when(active)
    def _():
        xn = xn_ref[...]
        h = (_silu(_dot(xn, wg_ref[...])) * _dot(xn, wu_ref[...])).astype(BF16)
        o_ref[...] += _dot(h, wd_ref[...])


def _experts(x, g, tok, blk_e, n_used, wg, wu, wd, *, tf):
    T, D = x.shape
    MB = MOE_BLOCK
    n_blk = tok.shape[0] // MB
    Fh = wg.shape[2]
    assert Fh // tf >= 2, "the next block's rows are requested during hidden chunk 1"
    fe = lambda i, f, nu: jnp.where(i < nu[0], f, 0)
    return pl.pallas_call(
        _experts_body,
        grid_spec=pltpu.PrefetchScalarGridSpec(
            num_scalar_prefetch=3,
            grid=(n_blk, Fh // tf),
            in_specs=[pl.BlockSpec(memory_space=pl.ANY),
                      pl.BlockSpec((1, D), lambda i, f, tk, be, nu: (0, 0)),
                      pl.BlockSpec((None, D, tf), lambda i, f, tk, be, nu: (be[i], 0, fe(i, f, nu))),
                      pl.BlockSpec((None, D, tf), lambda i, f, tk, be, nu: (be[i], 0, fe(i, f, nu))),
                      pl.BlockSpec((None, tf, D), lambda i, f, tk, be, nu: (be[i], fe(i, f, nu), 0))],
            out_specs=pl.BlockSpec((MB, D), lambda i, f, tk, be, nu: (i, 0)),
            scratch_shapes=[pltpu.VMEM((2, MB, D), F32), pltpu.VMEM((MB, D), BF16),
                            pltpu.SemaphoreType.DMA((2,))]),
        out_shape=jax.ShapeDtypeStruct((n_blk * MB, D), F32),
        compiler_params=_cp(("arbitrary", "arbitrary"), 52),
        name="experts",
    )(tok, blk_e, n_used, x, g.reshape(1, D), wg, wu, wd)


def _combine_body(slot_ref, y_hbm, x_ref, r_ref, o_ref, y0, y1, sem):
    i = pl.program_id(0)
    tm = x_ref.shape[0]

    def issue(r, c):
        a = (i * tm + r) * TOP_K
        _row_copy(y_hbm, slot_ref[a], y0, r, sem).start()
        _row_copy(y_hbm, slot_ref[a + 1], y1, r, sem).start()
        return c

    lax.fori_loop(0, tm, issue, 0, unroll=8)
    pltpu.make_async_copy(y_hbm.at[pl.ds(0, tm), :], y0, sem).wait()
    pltpu.make_async_copy(y_hbm.at[pl.ds(0, tm), :], y1, sem).wait()
    route = r_ref[...]
    o_ref[...] = x_ref[...] + y0[...] * route[:, 2:3] + y1[...] * route[:, 3:4]


def _combine(x, y, route, slots, *, tm):
    T, D = x.shape
    return pl.pallas_call(
        _combine_body,
        grid_spec=pltpu.PrefetchScalarGridSpec(
            num_scalar_prefetch=1,
            grid=(T // tm,),
            in_specs=[pl.BlockSpec(memory_space=pl.ANY),
                      pl.BlockSpec((tm, D), lambda i, s: (i, 0)),
                      pl.BlockSpec((tm, LANES), lambda i, s: (i, 0))],
            out_specs=pl.BlockSpec((tm, D), lambda i, s: (i, 0)),
            scratch_shapes=[pltpu.VMEM((tm, D), F32), pltpu.VMEM((tm, D), F32),
                            pltpu.SemaphoreType.DMA(())]),
        out_shape=jax.ShapeDtypeStruct((T, D), F32),
        compiler_params=_cp(("arbitrary",), 32),
        name="combine",
    )(slots, y, x, route)


def _moe(x, g, w_router, wg, wu, wd):
    T, D = x.shape
    MB = MOE_BLOCK
    A = T * TOP_K
    route = _router(x, g, w_router, tm=512)
    flat_e = route[:, :TOP_K].astype(I32).reshape(A)
    onehot = (flat_e[:, None] == jnp.arange(N_EXPERTS, dtype=I32)[None, :]).astype(I32)
    csum = jnp.cumsum(onehot, axis=0)
    rank = jnp.take_along_axis(csum, flat_e[:, None], axis=1)[:, 0] - 1
    padded = (csum[-1] + MB - 1) // MB * MB
    pend = jnp.cumsum(padded)
    slots = (pend - padded)[flat_e] + rank
    n_blk = A // MB + N_EXPERTS
    tok = jnp.zeros((n_blk * MB,), I32).at[slots].set(jnp.arange(A, dtype=I32) // TOP_K)
    blk_start = jnp.arange(n_blk, dtype=I32) * MB
    blk_e = jnp.minimum(jnp.sum(blk_start[:, None] >= pend[None, :], axis=1), N_EXPERTS - 1).astype(I32)
    n_used = (pend[-1:] // MB).astype(I32)
    y = _experts(x, g, tok, blk_e, n_used, wg, wu, wd, tf=min(1024, wg.shape[2]))
    return _combine(x, y, route, slots.astype(I32), tm=256)


def _ab_in_weight(w_in):
    r, k, v, wl, al, gl, zq, zc, zqi, zki, zwi = jnp.split(
        w_in, np.cumsum([1024, 1024, 1024, 96, 96, 256, 1024, 256, 1024, 64]).tolist(), axis=1)
    D = w_in.shape[0]
    z = lambda n: jnp.zeros((D, n), w_in.dtype)
    cols = [r, k, v, zq, zqi, gl, zc, wl, z(32), al, z(32), zki, zki, zwi, z(AB_COLS - AB_ZWI - 16)]
    return jnp.concatenate(cols, axis=1).astype(BF16)


def kernel(x, p, norm_mix, norm_ffn, ab_w_in, ab_mu, rwkv_w0, rwkv_w_up, rwkv_a0, rwkv_a_up, rwkv_g_up, rwkv_k_k, rwkv_k_a, rwkv_r_k, rwkv_gn_w, rwkv_gn_b, dsa_ckv_norm, dsa_w_uk, dsa_w_uv, ab_w_out, rel_bias, ffn_w_gate, ffn_w_up, ffn_w_down, gdn_w_in, gdn_conv, gdn_a_log, gdn_dt_bias, gdn_out_norm, gdn_w_out, moe_router, moe_w_gate, moe_w_up, moe_w_down, ple_norm, ple_w_gate, ple_w_proj, final_norm):
    B, L, D = x.shape
    T = B * L
    xf = x.reshape(T, D)
    bf = lambda w: w.astype(BF16)

    z = _norm_mm(xf, norm_mix[0], _ab_in_weight(ab_w_in[0]), tm=1024, tn=1024)
    y_a = _rwkv(z, B, L, ab_mu[0], rwkv_w0[0], rwkv_w_up[0], rwkv_a0[0], rwkv_a_up[0], rwkv_g_up[0],
                rwkv_k_k[0], rwkv_k_a[0], rwkv_r_k[0], rwkv_gn_w[0], rwkv_gn_b[0])
    y_b = _dsa(z, B, L, dsa_ckv_norm[0], dsa_w_uk[0], dsa_w_uv[0], rel_bias)
    xf = _mm_res([y_a, y_b], bf(ab_w_out[0]), xf, tm=1024, tn=1024)
    xf = _ffn(xf, norm_ffn[0], bf(ffn_w_gate[0]), bf(ffn_w_up[0]), bf(ffn_w_down[0]), tm=512, tf=512)
    xf = _ple(xf, ple_norm[0], bf(ple_w_gate[0]), p[0].reshape(T, PLE_DIM), bf(ple_w_proj[0]),
              final_norm, final=False, tm=512)

    z = _norm_mm(xf, norm_mix[1], _gd_in_weight(gdn_w_in[0]), tm=1024, tn=1280)
    gates = _gdn_gates(z, B, L, gdn_a_log[0], gdn_dt_bias[0])
    o = _gdn(z, gates, B, L, gdn_conv[0], gdn_out_norm[0])
    xf = _mm_res([o], bf(gdn_w_out[0]), xf, tm=512, tn=1024)
    xf = _moe(xf, norm_ffn[1], moe_router[0], bf(moe_w_gate[0]), bf(moe_w_up[0]), bf(moe_w_down[0]))
    xf = _ple(xf, ple_norm[1], bf(ple_w_gate[1]), p[1].reshape(T, PLE_DIM), bf(ple_w_proj[1]),
              final_norm, final=True, tm=512)
    return xf.reshape(B, L, D)
```

```python
import functools
import math

import numpy as np
import jax
import jax.numpy as jnp
from jax import lax
from jax.experimental import pallas as pl
from jax.experimental.pallas import tpu as pltpu

F32 = jnp.float32
BF16 = jnp.bfloat16
I32 = jnp.int32
HI = lax.Precision.HIGHEST

EPS = 1e-6
LANES = 128
MIB = 1024 * 1024

RWKV_H, RWKV_N = 16, 64
RWKV_W = RWKV_H * RWKV_N
RWKV_LORA = 96
RWKV_GATE = 256
RWKV_GN_EPS = 6.4e-4
RWKV_CHUNK = 64
RWKV_TB = 256

DSA_H, DSA_D, DSA_C = 8, 128, 256
IDX_H, IDX_D = 16, 64
TOPK_MAX = 256
QB = 128
KEY_GROUP = 4
REL_BUCKETS, REL_MAX_DIST = 32, 128
NEG = -1e30
INT_MIN = -(2 ** 31)

GDN_QK_H, GDN_V_H, GDN_D = 16, 32, 128
GDN_KW = GDN_QK_H * GDN_D
GDN_VW = GDN_V_H * GDN_D
GDN_CHUNK = 64
GDN_TB = 128
GDN_CONV_W = 4
GDN_HEADS_PER_STEP = 16

N_EXPERTS, TOP_K = 8, 2
MOE_BLOCK = 512
PLE_DIM = 256

AB_R, AB_K, AB_V = 0, 1024, 2048
AB_ZQ, AB_ZQI = 3072, 4096
AB_GL, AB_ZC = 5120, 5376
AB_WL, AB_AL = 5632, 5760
AB_ZKI, AB_ZWI = 5888, 6016
AB_COLS = 6144
GD_Q, GD_K, GD_V, GD_ZG, GD_GATES = 0, 2048, 4096, 8192, 12288
GD_COLS = 12800


def _cp(sem, vmem_mib):
    return pltpu.CompilerParams(dimension_semantics=sem, vmem_limit_bytes=vmem_mib * MIB)


def _dot(a, b, precision=None):
    return jnp.dot(a, b, preferred_element_type=F32, precision=precision)


def _dot_nt(a, b, precision=None):
    return lax.dot_general(a, b, (((1,), (1,)), ((), ())), preferred_element_type=F32,
                           precision=precision)


def _split(x):
    hi = x.astype(BF16)
    return hi, (x - hi.astype(F32)).astype(BF16)


def _seg_dot(x, ones_bf):
    hi, lo = _split(x)
    return _dot(hi, ones_bf) + _dot(lo, ones_bf)


def _rms(x, g):
    ms = jnp.mean(x * x, axis=-1, keepdims=True)
    return x * lax.rsqrt(ms + EPS) * g


def _sigmoid(x):
    return 0.5 * jnp.tanh(0.5 * x) + 0.5


def _silu(x):
    return x * _sigmoid(x)


def _softplus(x):
    return jnp.maximum(x, 0.0) + jnp.log1p(jnp.exp(-jnp.abs(x)))


def _iota(shape, dim):
    return lax.broadcasted_iota(I32, shape, dim)


def _norm_mm_body(x_ref, g_ref, w_ref, o_ref, xn_ref):
    @pl.when(pl.program_id(1) == 0)
    def _():
        xn_ref[...] = _rms(x_ref[...], g_ref[...]).astype(BF16)

    o_ref[...] = _dot(xn_ref[...], w_ref[...]).astype(o_ref.dtype)


def _norm_mm(x, g, w, *, tm, tn):
    T, K = x.shape
    N = w.shape[1]
    return pl.pallas_call(
        _norm_mm_body,
        grid=(T // tm, N // tn),
        in_specs=[pl.BlockSpec((tm, K), lambda i, j: (i, 0)),
                  pl.BlockSpec((1, K), lambda i, j: (0, 0)),
                  pl.BlockSpec((K, tn), lambda i, j: (0, j))],
        out_specs=pl.BlockSpec((tm, tn), lambda i, j: (i, j)),
        out_shape=jax.ShapeDtypeStruct((T, N), F32),
        scratch_shapes=[pltpu.VMEM((tm, K), BF16)],
        compiler_params=_cp(("parallel", "arbitrary"), 48),
        name="norm_mm",
    )(x, g.reshape(1, K), w)


def _mm_res_body(*refs):
    *a_refs, w_ref, r_ref, o_ref = refs
    acc = r_ref[...]
    off = 0
    for a_ref in a_refs:
        k = a_ref.shape[1]
        acc = acc + _dot(a_ref[...], w_ref[off:off + k, :])
        off += k
    o_ref[...] = acc


def _mm_res(parts, w, res, *, tm, tn):
    T = res.shape[0]
    K, N = w.shape
    assert sum(a.shape[1] for a in parts) == K
    return pl.pallas_call(
        _mm_res_body,
        grid=(T // tm, N // tn),
        in_specs=[pl.BlockSpec((tm, a.shape[1]), lambda i, j: (i, 0)) for a in parts]
        + [pl.BlockSpec((K, tn), lambda i, j: (0, j)),
           pl.BlockSpec((tm, tn), lambda i, j: (i, j))],
        out_specs=pl.BlockSpec((tm, tn), lambda i, j: (i, j)),
        out_shape=jax.ShapeDtypeStruct((T, N), F32),
        compiler_params=_cp(("parallel", "arbitrary"), 48),
        name="mm_res",
    )(*parts, w, res)


def _ple_body(x_ref, g_ref, wg_ref, p_ref, wp_ref, fg_ref, o_ref, *, final):
    x = x_ref[...]
    gate = _sigmoid(_dot(_rms(x, g_ref[...]).astype(BF16), wg_ref[...]))
    y = x + gate * _dot(p_ref[...].astype(BF16), wp_ref[...])
    if final:
        y = _rms(y, fg_ref[...])
    o_ref[...] = y


def _ple(x, g, wg, p, wp, fg, *, final, tm):
    T, D = x.shape
    P = p.shape[1]
    return pl.pallas_call(
        functools.partial(_ple_body, final=final),
        grid=(T // tm,),
        in_specs=[pl.BlockSpec((tm, D), lambda i: (i, 0)),
                  pl.BlockSpec((1, D), lambda i: (0, 0)),
                  pl.BlockSpec((D, D), lambda i: (0, 0)),
                  pl.BlockSpec((tm, P), lambda i: (i, 0)),
                  pl.BlockSpec((P, D), lambda i: (0, 0)),
                  pl.BlockSpec((1, D), lambda i: (0, 0))],
        out_specs=pl.BlockSpec((tm, D), lambda i: (i, 0)),
        out_shape=jax.ShapeDtypeStruct((T, D), F32),
        compiler_params=_cp(("parallel",), 48),
        name="ple",
    )(x, g.reshape(1, D), wg, p, wp, fg.reshape(1, D))


def _ffn_body(x_ref, g_ref, wg_ref, wu_ref, wd_ref, o_ref, xn_ref):
    @pl.when(pl.program_id(1) == 0)
    def _():
        x = x_ref[...]
        xn_ref[...] = _rms(x, g_ref[...]).astype(BF16)
        o_ref[...] = x

    xn = xn_ref[...]
    h = (_silu(_dot(xn, wg_ref[...])) * _dot(xn, wu_ref[...])).astype(BF16)
    o_ref[...] += _dot(h, wd_ref[...])


def _ffn(x, g, wg, wu, wd, *, tm, tf):
    T, D = x.shape
    Fh = wg.shape[1]
    return pl.pallas_call(
        _ffn_body,
        grid=(T // tm, Fh // tf),
        in_specs=[pl.BlockSpec((tm, D), lambda i, f: (i, 0)),
                  pl.BlockSpec((1, D), lambda i, f: (0, 0)),
                  pl.BlockSpec((D, tf), lambda i, f: (0, f)),
                  pl.BlockSpec((D, tf), lambda i, f: (0, f)),
                  pl.BlockSpec((tf, D), lambda i, f: (f, 0))],
        out_specs=pl.BlockSpec((tm, D), lambda i, f: (i, 0)),
        out_shape=jax.ShapeDtypeStruct((T, D), F32),
        scratch_shapes=[pltpu.VMEM((tm, D), BF16)],
        compiler_params=_cp(("parallel", "arbitrary"), 48),
        name="ffn",
    )(x, g.reshape(1, D), wg, wu, wd)


def _shift_mix(x, prev_row, mu):
    xs = pltpu.roll(x, 1, axis=0)
    xs = jnp.where(_iota(x.shape, 0) == 0, prev_row, xs)
    return x + (xs - x) * mu


def _bd(x, lo):
    return jnp.concatenate([jnp.where(lo, x, 0.0), jnp.where(lo, 0.0, x)], axis=0)


def _unit_lower_inverses(xs, eye, steps, sign):
    ps = [eye + x if sign > 0 else eye - x for x in xs]
    xbs = [x.astype(BF16) for x in xs]
    for _ in range(steps):
        xbs = [_dot(xb, xb).astype(BF16) for xb in xbs]
        ps = [p + _dot(p.astype(BF16), xb) for p, xb in zip(ps, xbs)]
    return ps


def _rwkv_body(rkv_ref, gl_ref, wa_ref, mu_rkv_ref, mu_gl_ref, mu_wa_ref, w0_ref, wup_ref,
               a0_ref, aup_ref, gup_ref, kk_ref, ka_ref, rk_ref, gnw_ref, gnb_ref, o_ref,
               prev_rkv, prev_gl, prev_wa, state, r_s, k_s, v_s, kn_s, a_s, lw_s, cg_s, g_s, y_s):
    TB = rkv_ref.shape[0]
    C = RWKV_CHUNK
    NP = RWKV_W // LANES

    @pl.when(pl.program_id(1) == 0)
    def _():
        prev_rkv[...] = jnp.zeros_like(prev_rkv)
        prev_gl[...] = jnp.zeros_like(prev_gl)
        prev_wa[...] = jnp.zeros_like(prev_wa)
        state[...] = jnp.zeros_like(state)

    lane = _iota((1, LANES), 1)
    lo = lane < RWKV_N
    r128 = _iota((LANES, LANES), 0)
    c128 = _iota((LANES, LANES), 1)
    same = (r128 // C) == (c128 // C)
    seg_ones = jnp.where(same, 1.0, 0.0).astype(BF16)
    eye = jnp.where(r128 == c128, 1.0, 0.0).astype(F32)
    strict = same & (c128 < r128)
    incl = same & (c128 <= r128)
    rt = _iota((TB, TB), 0)
    ct = _iota((TB, TB), 1)
    tri_chunks = jnp.where(((rt // C) == (ct // C)) & (ct <= rt), 1.0, 0.0).astype(BF16)

    wa_raw = wa_ref[...]
    wa = _shift_mix(wa_raw, prev_wa[...], mu_wa_ref[...])
    prev_wa[...] = wa_raw[TB - 1:TB, :]
    wl = jnp.tanh(wa[:, :LANES]).astype(BF16)
    al = wa[:, LANES:].astype(BF16)
    w = -_softplus(-(w0_ref[...] + _dot(wl, wup_ref[...]))) - 0.5
    lw = -jnp.exp(w)
    lw_s[...] = lw
    lw_hi, lw_lo = _split(lw)
    cg_s[...] = _dot(tri_chunks, lw_hi) + _dot(tri_chunks, lw_lo)
    a_s[...] = _sigmoid(a0_ref[...] + _dot(al, aup_ref[...]))
    gl_raw = gl_ref[...]
    gl = _shift_mix(gl_raw, prev_gl[...], mu_gl_ref[...])
    prev_gl[...] = gl_raw[TB - 1:TB, :]
    g_s[...] = _dot(_sigmoid(gl).astype(BF16), gup_ref[...])

    for p in range(NP):
        cs = slice(p * LANES, (p + 1) * LANES)
        cols = [slice(off + p * LANES, off + (p + 1) * LANES) for off in (AB_R, AB_K, AB_V)]
        mixed = []
        for c in cols:
            raw = rkv_ref[:, c]
            mixed.append(_shift_mix(raw, prev_rkv[:, c], mu_rkv_ref[:, c]))
            prev_rkv[:, c] = raw[TB - 1:TB, :]
        r, k, v = mixed
        kk = k * kk_ref[:, cs]
        kn_s[:, cs] = kk * lax.rsqrt(_seg_dot(kk * kk, seg_ones) + 1e-6)
        r_s[:, cs] = r
        k_s[:, cs] = k * (1.0 + (a_s[:, cs] - 1.0) * ka_ref[:, cs])
        v_s[:, cs] = v

    def chunk(c, carry):
        rows = pl.ds(pl.multiple_of(c * C, C), C)
        pairs = range(NP)
        css = [slice(p * LANES, (p + 1) * LANES) for p in pairs]
        gam_last, ar, bk, bk_end, v_f, v_t = [], [], [], [], [], []
        for cs in css:
            cg = cg_s[rows, cs]
            gam = jnp.exp(cg)
            gam_inv = jnp.exp(-cg)
            gam_prev = jnp.exp(cg - lw_s[rows, cs])
            gl_ = gam[C - 1:C, :]
            kn = kn_s[rows, cs]
            b_raw = kn * a_s[rows, cs] * gam_inv
            k_raw = k_s[rows, cs] * gam_inv
            gam_last.append(gl_)
            ar.append(jnp.concatenate([_bd(-kn * gam_prev, lo), _bd(r_s[rows, cs] * gam, lo)],
                                      axis=0).astype(BF16))
            bk.append(jnp.concatenate([_bd(b_raw, lo), _bd(k_raw, lo)], axis=0).astype(BF16))
            bk_end.append(jnp.concatenate([_bd(b_raw * gl_, lo), _bd(k_raw * gl_, lo)],
                                          axis=0).astype(BF16))
            vf = _bd(v_s[rows, cs], lo)
            v_f.append(vf)
            v_t.append(vf.astype(BF16))
        score = [_dot_nt(ar[p], bk[p]) for p in pairs]
        a_ab = [jnp.where(strict, s[:LANES, :LANES], 0.0) for s in score]
        a_akv = [_dot(jnp.where(strict, score[p][:LANES, LANES:], 0.0).astype(BF16), v_t[p])
                 for p in pairs]
        r_abk = [jnp.concatenate([jnp.where(incl, s[LANES:, :LANES], 0.0),
                                  jnp.where(incl, s[LANES:, LANES:], 0.0)], axis=1).astype(BF16)
                 for s in score]
        t_inv = [t.astype(BF16) for t in _unit_lower_inverses(a_ab, eye, 5, 1)]
        s_old = [state[p] for p in pairs]
        sproj = [_dot_nt(ar[p], s_old[p].astype(BF16)) for p in pairs]
        u = [_dot(t_inv[p], (sproj[p][:LANES] + a_akv[p]).astype(BF16)) for p in pairs]
        y_bd = [sproj[p][LANES:] + _dot(r_abk[p], jnp.concatenate([u[p].astype(BF16), v_t[p]], axis=0))
                for p in pairs]
        for p in pairs:
            y_s[rows, css[p]] = y_bd[p][:C] + y_bd[p][C:]
        uvT = [jnp.concatenate([u[p].T, v_f[p].T], axis=1).astype(BF16) for p in pairs]
        for p in pairs:
            state[p] = s_old[p] * gam_last[p] + _dot(uvT[p], bk_end[p])
        return carry

    lax.fori_loop(0, TB // C, chunk, 0)

    for p in range(NP):
        cs = slice(p * LANES, (p + 1) * LANES)
        y = y_s[:, cs]
        mean = _seg_dot(y, seg_ones) * (1.0 / RWKV_N)
        d = y - mean
        var = _seg_dot(d * d, seg_ones) * (1.0 / RWKV_N)
        yn = d * lax.rsqrt(var + RWKV_GN_EPS) * gnw_ref[:, cs] + gnb_ref[:, cs]
        bonus = _seg_dot(r_s[:, cs] * k_s[:, cs] * rk_ref[:, cs], seg_ones) * v_s[:, cs]
        o_ref[:, cs] = ((yn + bonus) * g_s[:, cs]).astype(o_ref.dtype)


def _rwkv(z, B, L, mu, w0, w_up, a0, a_up, g_up, k_k, k_a, r_k, gn_w, gn_b):
    TB = RWKV_TB
    nb = L // TB
    W = RWKV_W
    mu_r, mu_k, mu_v, mu_wl, mu_al, mu_gl = jnp.split(
        mu, np.cumsum([W, W, W, RWKV_LORA, RWKV_LORA])[:].tolist())
    pad = LANES - RWKV_LORA
    mu_rkv = jnp.concatenate([mu_r, mu_k, mu_v]).reshape(1, 3 * W)
    mu_wa = jnp.concatenate([jnp.pad(mu_wl, (0, pad)), jnp.pad(mu_al, (0, pad))]).reshape(1, 2 * LANES)
    wup = jnp.pad(w_up, ((0, pad), (0, 0))).astype(BF16)
    aup = jnp.pad(a_up, ((0, pad), (0, 0))).astype(BF16)
    row = lambda t: t.reshape(1, W)
    vec = lambda n: pl.BlockSpec((1, n), lambda b, i: (0, 0))
    big = lambda: pltpu.VMEM((TB, W), F32)
    return pl.pallas_call(
        _rwkv_body,
        grid=(B, nb),
        in_specs=[pl.BlockSpec((TB, 3 * W), lambda b, i: (b * nb + i, 0)),
                  pl.BlockSpec((TB, RWKV_GATE), lambda b, i: (b * nb + i, AB_GL // RWKV_GATE)),
                  pl.BlockSpec((TB, 2 * LANES), lambda b, i: (b * nb + i, AB_WL // (2 * LANES))),
                  vec(3 * W), vec(RWKV_GATE), vec(2 * LANES), vec(W),
                  pl.BlockSpec((LANES, W), lambda b, i: (0, 0)),
                  vec(W),
                  pl.BlockSpec((LANES, W), lambda b, i: (0, 0)),
                  pl.BlockSpec((RWKV_GATE, W), lambda b, i: (0, 0)),
                  vec(W), vec(W), vec(W), vec(W), vec(W)],
        out_specs=pl.BlockSpec((TB, W), lambda b, i: (b * nb + i, 0)),
        out_shape=jax.ShapeDtypeStruct((B * L, W), BF16),
        scratch_shapes=[pltpu.VMEM((1, 3 * W), F32), pltpu.VMEM((1, RWKV_GATE), F32),
                        pltpu.VMEM((1, 2 * LANES), F32),
                        pltpu.VMEM((W // LANES, LANES, LANES), F32),
                        big(), big(), big(), big(), big(), big(), big(), big(), big()],
        compiler_params=_cp(("arbitrary", "arbitrary"), 48),
        name="rwkv7",
    )(z, z, z, mu_rkv, mu_gl.reshape(1, RWKV_GATE), mu_wa, row(w0), wup, row(a0), aup,
      g_up.astype(BF16), row(k_k), row(k_a), row(r_k), row(gn_w), row(gn_b))


def _t5_bucket_np(dist):
    n = np.maximum(dist, 0)
    exact = REL_BUCKETS // 2
    ratio = np.log(np.maximum(n, 1).astype(np.float32) / np.float32(exact)) / np.float32(
        math.log(REL_MAX_DIST / exact))
    large = exact + (ratio.astype(np.float32) * np.float32(REL_BUCKETS - exact)).astype(np.int32)
    return np.where(n < exact, n, np.minimum(large, REL_BUCKETS - 1)).astype(np.int32)


def _near_buckets():
    kl = np.arange(QB)[:, None]
    ql = np.arange(QB)[None, :]
    return np.stack([_t5_bucket_np(ql - kl), _t5_bucket_np(QB + ql - kl)])


def _dsa_body(tbl_ref, bkt_ref, zq_ref, zqi_ref, zc_ref, zki_ref, zwi_ref, cn_ref, wukT_ref,
              wuvT_ref, o_ref, c_all, cT_all, kibd_all, sc, qiT, qlatT, bias, m_s, l_s, alpha_s,
              oT, *, topk):
    b = pl.program_id(0)
    qb = pl.program_id(1)
    lo = _iota((1, LANES), 1) < IDX_D
    krow = _iota((QB, QB), 0)
    qcol = _iota((QB, QB), 1)
    hsl = [slice(h * QB, (h + 1) * QB) for h in range(DSA_H)]

    @pl.when((b == 0) & (qb == 0))
    def _():
        for t in range(2):
            bk = bkt_ref[t]
            for h in range(DSA_H):
                far = tbl_ref[REL_BUCKETS - 1, h]
                acc = jnp.zeros((QB, QB), F32)
                for bb in range(REL_BUCKETS - 1):
                    acc = jnp.where(bk == bb, tbl_ref[bb, h] - far, acc)
                bias[t, h] = acc

    c_new = _rms(zc_ref[...], cn_ref[...])
    c_all[qb] = c_new.astype(BF16)
    cT_all[qb] = c_new.T.astype(BF16)
    kibd_all[qb] = _bd(zki_ref[...], lo).astype(BF16)

    for p in range(IDX_H // 2):
        qiT[:, p * QB:(p + 1) * QB] = zqi_ref[:, p * LANES:(p + 1) * LANES].T.astype(BF16)
    wT = zwi_ref[...].T * (IDX_D ** -0.5 * IDX_H ** -0.5)
    w_rows = [wT[h:h + 1, :] for h in range(IDX_H)]
    for h in range(DSA_H):
        qhT = zq_ref[:, hsl[h]].T.astype(BF16)
        qlatT[:, hsl[h]] = (_dot(wukT_ref[h], qhT) * DSA_D ** -0.5).astype(BF16)

    def score_blocks(j, nk):
        kb = kibd_all[pl.ds(j, nk)].reshape(nk * 2 * QB, LANES)
        acc = [jnp.zeros((QB, QB), F32) for _ in range(nk)]
        for pp in range(IDX_H // 4):
            s = _dot(kb, qiT[:, 2 * pp * QB:2 * (pp + 1) * QB])
            for k in range(nk):
                even = s[2 * k * QB:(2 * k + 1) * QB]
                odd = s[(2 * k + 1) * QB:(2 * k + 2) * QB]
                acc[k] = (acc[k] + w_rows[4 * pp] * jnp.maximum(even[:, :QB], 0.0)
                          + w_rows[4 * pp + 1] * jnp.maximum(odd[:, :QB], 0.0)
                          + w_rows[4 * pp + 2] * jnp.maximum(even[:, QB:], 0.0)
                          + w_rows[4 * pp + 3] * jnp.maximum(odd[:, QB:], 0.0))
        for k in range(nk):
            bits = lax.bitcast_convert_type(acc[k], I32)
            key = bits ^ ((bits >> 31) & 0x7FFFFFFF)
            sc[j + k] = jnp.where((j + k == qb) & (krow > qcol), INT_MIN, key)

    def score_group(jj, carry):
        score_blocks(jj * KEY_GROUP, KEY_GROUP)
        return carry

    def score_single(j, carry):
        score_blocks(j, 1)
        return carry

    n_sgroup = (qb + 1) // KEY_GROUP
    lax.fori_loop(0, n_sgroup, score_group, 0)
    lax.fori_loop(n_sgroup * KEY_GROUP, qb + 1, score_single, 0)

    n_quad = (qb + 1) // KEY_GROUP

    def count(pred):
        def quad(jj, a):
            blk = sc[pl.ds(jj * KEY_GROUP, KEY_GROUP)]
            for k in range(KEY_GROUP):
                a = a + jnp.where(pred(blk[k]), 1, 0)
            return a

        def single(j, a):
            return a + jnp.where(pred(sc[j]), 1, 0)

        a = lax.fori_loop(0, n_quad, quad, jnp.zeros((QB, QB), I32))
        a = lax.fori_loop(n_quad * KEY_GROUP, qb + 1, single, a)
        return jnp.sum(a, axis=0, keepdims=True)

    def bis_body(i, carry):
        t, n_t = carry
        cand = t ^ jnp.left_shift(jnp.int32(1), 31 - i)
        tot = count(lambda key: key >= cand)
        ok = tot >= topk
        return jnp.where(ok, cand, t), jnp.where(ok, tot, n_t)

    thr_raw, n_ge = lax.fori_loop(
        0, 32, bis_body,
        (jnp.full((1, QB), INT_MIN, I32), jnp.broadcast_to((qb + 1) * QB, (1, QB)).astype(I32)))
    thr = jnp.maximum(thr_raw, INT_MIN + 1)

    @pl.when(jnp.max(n_ge) > topk)
    def _():
        keep = (topk - count(lambda key: key > thr_raw)).astype(F32)
        tri = jnp.where(qcol <= krow, 1.0, 0.0).astype(BF16)

        def drop(j, seen):
            key = sc[j]
            tied = key == thr_raw
            tied_bf = jnp.where(tied, 1.0, 0.0).astype(BF16)
            rank = seen + _dot(tri, tied_bf)
            sc[j] = jnp.where(tied & (rank > keep), INT_MIN, key)
            return seen + jnp.sum(tied_bf.astype(F32), axis=0, keepdims=True)

        lax.fori_loop(0, qb + 1, drop, jnp.zeros((1, QB), F32))

    m_s[...] = jnp.full(m_s.shape, NEG, F32)
    l_s[...] = jnp.zeros_like(l_s)
    oT[...] = jnp.zeros_like(oT)

    def attend(j, nk, near):
        sel = sc[pl.ds(j, nk)].reshape(nk * QB, QB) >= thr
        lg = _dot(c_all[pl.ds(j, nk)].reshape(nk * QB, DSA_C), qlatT[...])
        cT = jnp.concatenate([cT_all[j + k] for k in range(nk)], axis=1)
        prs = []
        for h in range(DSA_H):
            lgh = lg[:, hsl[h]]
            if near is not None:
                lgh = lgh + bias[near, h]
            lgh = jnp.where(sel, lgh, NEG)
            m_old = m_s[:, hsl[h]]
            m_new = jnp.maximum(m_old, jnp.max(lgh, axis=0, keepdims=True))
            pr = jnp.exp(lgh - m_new)
            alpha_s[:, hsl[h]] = jnp.exp(m_old - m_new)
            l_s[:, hsl[h]] = alpha_s[:, hsl[h]] * l_s[:, hsl[h]] + jnp.sum(pr, axis=0, keepdims=True)
            m_s[:, hsl[h]] = m_new
            prs.append(pr.astype(BF16))
        oT[...] = alpha_s[...] * oT[...] + _dot(cT, jnp.concatenate(prs, axis=1))

    def far_group(jj, carry):
        attend(jj * KEY_GROUP, KEY_GROUP, None)
        return carry

    def far_single(j, carry):
        attend(j, 1, None)
        return carry

    n_far = jnp.maximum(qb - 1, 0)
    n_group = n_far // KEY_GROUP
    lax.fori_loop(0, n_group, far_group, 0)
    lax.fori_loop(n_group * KEY_GROUP, n_far, far_single, 0)

    @pl.when(qb >= 1)
    def _():
        attend(qb - 1, 1, 1)

    attend(qb, 1, 0)

    inv_l = 1.0 / l_s[...]
    for h in range(DSA_H):
        oh = (oT[:, hsl[h]] * inv_l[:, hsl[h]]).astype(BF16)
        o_ref[:, hsl[h]] = _dot(wuvT_ref[h], oh).T.astype(o_ref.dtype)


def _dsa(z, B, L, ckv_norm, w_uk, w_uv, rel_bias):
    nq = L // QB
    topk = min(TOPK_MAX, L // 4)
    wukT = jnp.swapaxes(w_uk, 1, 2).astype(BF16)
    wuvT = jnp.swapaxes(w_uv, 1, 2).astype(BF16)
    blk = lambda w, off: pl.BlockSpec((QB, w), lambda b, q: (b * nq + q, off // w))
    full = lambda shape: pl.BlockSpec(shape, lambda b, q: (0,) * len(shape))
    W = DSA_H * DSA_D
    return pl.pallas_call(
        functools.partial(_dsa_body, topk=topk),
        grid=(B, nq),
        in_specs=[pl.BlockSpec(memory_space=pltpu.SMEM),
                  full((2, QB, QB)),
                  blk(W, AB_ZQ), blk(IDX_H * IDX_D, AB_ZQI), blk(DSA_C, AB_ZC),
                  blk(LANES, AB_ZKI), blk(LANES, AB_ZWI),
                  full((1, DSA_C)), full((DSA_H, DSA_C, DSA_D)), full((DSA_H, DSA_D, DSA_C))],
        out_specs=pl.BlockSpec((QB, W), lambda b, q: (b * nq + q, 0)),
        out_shape=jax.ShapeDtypeStruct((B * L, W), BF16),
        scratch_shapes=[pltpu.VMEM((nq, QB, DSA_C), BF16), pltpu.VMEM((nq, DSA_C, QB), BF16),
                        pltpu.VMEM((nq, 2 * QB, LANES), BF16), pltpu.VMEM((nq, QB, QB), I32),
                        pltpu.VMEM((LANES, IDX_H // 2 * QB), BF16), pltpu.VMEM((DSA_C, W), BF16),
                        pltpu.VMEM((2, DSA_H, QB, QB), F32),
                        pltpu.VMEM((1, W), F32), pltpu.VMEM((1, W), F32), pltpu.VMEM((1, W), F32),
                        pltpu.VMEM((DSA_C, W), F32)],
        compiler_params=_cp(("arbitrary", "arbitrary"), 48),
        name="dsa",
    )(rel_bias, jnp.asarray(_near_buckets()), z, z, z, z, z, ckv_norm.reshape(1, DSA_C), wukT, wuvT)


def _gdn_gates_body(z_ref, alog_ref, dtb_ref, o_ref):
    TB = z_ref.shape[0]
    C = GDN_CHUNK
    z = z_ref[...]
    sub = _iota((1, LANES), 1) % 8
    beta = _sigmoid(z)
    g = -jnp.exp(alog_ref[...]) * _softplus(z + dtb_ref[...])
    rt = _iota((TB, TB), 0)
    ct = _iota((TB, TB), 1)
    same = (rt // C) == (ct // C)
    cum = _dot(jnp.where(same & (ct <= rt), 1.0, 0.0).astype(F32), g, HI)
    tot = _dot(jnp.where(same, 1.0, 0.0).astype(F32), g, HI)
    tile = jnp.where(sub < 2, beta, jnp.where(sub < 4, g, jnp.where(sub < 6, cum, tot)))
    o_ref[...] = tile.T


def _gdn_gates(z, B, L, a_log, dt_bias):
    TB = GDN_TB
    nb = L // TB
    spread = lambda t: jnp.zeros((GDN_QK_H, 8), F32).at[:, 2:].set(
        jnp.tile(t.reshape(GDN_QK_H, 2), (1, 3))).reshape(1, LANES)
    return pl.pallas_call(
        _gdn_gates_body,
        grid=(B, nb),
        in_specs=[pl.BlockSpec((TB, LANES), lambda b, i: (b * nb + i, GD_GATES // LANES)),
                  pl.BlockSpec((1, LANES), lambda b, i: (0, 0)),
                  pl.BlockSpec((1, LANES), lambda b, i: (0, 0))],
        out_specs=pl.BlockSpec((None, LANES, TB), lambda b, i: (b, 0, i)),
        out_shape=jax.ShapeDtypeStruct((B, LANES, L), F32),
        compiler_params=_cp(("parallel", "parallel"), 32),
        name="gdn_gates",
    )(z, spread(a_log), spread(dt_bias))


def _gdn_body(zq_ref, zk_ref, zv_ref, zg_ref, gates_ref, cwq_ref, cwk_ref, cwv_ref, on_ref, o_ref,
              xbuf, state, q_s, k_s, v_s, *, hg):
    TB = zq_ref.shape[0]
    C = GDN_CHUNK
    D = GDN_D
    KW = GDN_CONV_W
    W = 4 * D * hg

    @pl.when(pl.program_id(2) == 0)
    def _():
        xbuf[0:8, :] = jnp.zeros((8, W), F32)
        state[...] = jnp.zeros_like(state)

    xbuf[8:TB + 8, 0:D * hg] = zq_ref[...]
    xbuf[8:TB + 8, D * hg:2 * D * hg] = zk_ref[...]
    xbuf[8:TB + 8, 2 * D * hg:W] = zv_ref[...]
    cw = jnp.concatenate([cwq_ref[...], cwk_ref[...], cwv_ref[...]], axis=1)
    l2 = lambda t: t * lax.rsqrt(jnp.sum(t * t, axis=-1, keepdims=True) + 1e-6)
    for g in range(W // D):
        cols = slice(g * D, (g + 1) * D)
        acc = jnp.zeros((TB, D), F32)
        for j in range(KW):
            acc = acc + cw[j:j + 1, cols] * xbuf[8 - (KW - 1) + j:8 - (KW - 1) + j + TB, cols]
        act = _silu(acc)
        if g < hg:
            q_s[:, cols] = l2(act) * D ** -0.5
        elif g < 2 * hg:
            k_s[:, (g - hg) * D:(g - hg + 1) * D] = l2(act)
        else:
            v_s[:, (g - 2 * hg) * D:(g - 2 * hg + 1) * D] = act
    xbuf[0:8, :] = xbuf[TB:TB + 8, :]

    lane = _iota((1, LANES), 1)
    lo = lane < C
    r128 = _iota((LANES, LANES), 0)
    c128 = _iota((LANES, LANES), 1)
    same = (r128 // C) == (c128 // C)
    eye_m = r128 == c128
    eye = jnp.where(eye_m, 1.0, 0.0).astype(F32)
    strict = same & (c128 < r128)
    incl = same & (c128 <= r128)

    heads = range(hg)
    for c in range(TB // C):
        rows = slice(c * C, (c + 1) * C)
        e_st, e_neg, fb_st, etot_cat, kc, qc, dec = [], [], [], [], [], [], []
        for h in heads:
            win = gates_ref[8 * h:8 * h + 8, (c // 2) * LANES:(c // 2 + 1) * LANES]
            win_sw = pltpu.roll(win, C, axis=1)
            first, second = (win, win_sw) if c % 2 == 0 else (win_sw, win)
            st = lambda r: jnp.where(lo, first[r:r + 1, :], second[r + 1:r + 2, :])
            beta, cum, tot = st(0), st(4), st(6)
            etot = jnp.exp(tot)
            etot_sw = pltpu.roll(etot, C, axis=1)
            e_st.append(jnp.exp(cum))
            e_neg.append(-jnp.exp(cum))
            fb_st.append(jnp.exp(tot - cum) * beta)
            etot_cat.append(jnp.concatenate([jnp.where(lo, etot, etot_sw),
                                             jnp.where(lo, etot_sw, etot)], axis=1))
            kc.append(k_s[rows, h * D:(h + 1) * D])
            qc.append(q_s[rows, h * D:(h + 1) * D])
            cum_b = jnp.broadcast_to(cum, (LANES, LANES))
            dec.append(jnp.exp(jnp.where(incl, cum_b.T - cum_b, NEG)) * beta)
        score = [_dot_nt(jnp.concatenate([kc[h], kc[h], qc[h], qc[h]], axis=0).astype(BF16),
                         jnp.concatenate([kc[h], kc[h]], axis=0).astype(BF16))
                 for h in heads]
        t_inv = _unit_lower_inverses(
            [jnp.where(strict, score[h][:LANES] * dec[h], 0.0) for h in heads], eye, 5, -1)
        t_cat = [jnp.concatenate([t_inv[h], t_inv[h] * e_neg[h]], axis=1).astype(BF16) for h in heads]
        a_cat = [jnp.concatenate([eye * e_st[h], score[h][LANES:] * dec[h]], axis=1).astype(BF16)
                 for h in heads]
        s_old = [state[h] for h in heads]
        proj = [_dot(jnp.concatenate([kc[h], qc[h]], axis=0).astype(BF16), s_old[h].astype(BF16))
                for h in heads]
        rhs = [jnp.concatenate([v_s[rows, 2 * h * D:(2 * h + 1) * D],
                                v_s[rows, (2 * h + 1) * D:(2 * h + 2) * D],
                                proj[h][:C, :D], proj[h][:C, D:]], axis=0).astype(BF16) for h in heads]
        vn = [_dot(t_cat[h], rhs[h]) for h in heads]
        vn_bf = [x.astype(BF16) for x in vn]
        o_st = [_dot(a_cat[h], jnp.concatenate(
            [proj[h][C:, :D].astype(BF16), proj[h][C:, D:].astype(BF16), vn_bf[h]], axis=0))
            for h in heads]
        fv = [_dot((eye * fb_st[h]).astype(BF16), vn_bf[h]) for h in heads]
        for h in heads:
            state[h] = s_old[h] * etot_cat[h] + _dot(
                kc[h].T.astype(BF16), jnp.concatenate([fv[h][:C], fv[h][C:]], axis=1).astype(BF16))
        for h in heads:
            for u in range(2):
                cols = slice((2 * h + u) * D, (2 * h + u + 1) * D)
                oh = o_st[h][u * C:(u + 1) * C]
                o_ref[rows, cols] = (_rms(oh, on_ref[...]) * _silu(zg_ref[rows, cols])).astype(o_ref.dtype)


def _gdn(z, gates, B, L, conv_w, out_norm):
    TB = GDN_TB
    nb = L // TB
    D = GDN_D
    hg = GDN_HEADS_PER_STEP
    zspec = lambda w, off: pl.BlockSpec((TB, w), lambda b, h, i: (b * nb + i, off // w + h))
    cspec = lambda w, off: pl.BlockSpec((GDN_CONV_W, w), lambda b, h, i: (0, off // w + h))
    return pl.pallas_call(
        functools.partial(_gdn_body, hg=hg),
        grid=(B, GDN_QK_H // hg, nb),
        in_specs=[zspec(D * hg, GD_Q), zspec(D * hg, GD_K), zspec(2 * D * hg, GD_V),
                  zspec(2 * D * hg, GD_ZG),
                  pl.BlockSpec((None, 8 * hg, TB), lambda b, h, i: (b, h, i)),
                  cspec(D * hg, GD_Q), cspec(D * hg, GD_K), cspec(2 * D * hg, GD_V),
                  pl.BlockSpec((1, D), lambda b, h, i: (0, 0))],
        out_specs=pl.BlockSpec((TB, 2 * D * hg), lambda b, h, i: (b * nb + i, h)),
        out_shape=jax.ShapeDtypeStruct((B * L, GDN_VW), BF16),
        scratch_shapes=[pltpu.VMEM((TB + 8, 4 * D * hg), F32), pltpu.VMEM((hg, D, 2 * D), F32),
                        pltpu.VMEM((TB, D * hg), F32), pltpu.VMEM((TB, D * hg), F32),
                        pltpu.VMEM((TB, 2 * D * hg), F32)],
        compiler_params=_cp(("parallel", "parallel", "arbitrary"), 40),
        name="gdn",
    )(z, z, z, z, gates, conv_w, conv_w, conv_w, out_norm.reshape(1, D))


def _gd_in_weight(w_in):
    qkv, zg, b, a = jnp.split(w_in, np.cumsum([2 * GDN_KW + GDN_VW, GDN_VW, GDN_V_H]).tolist(), axis=1)
    D = w_in.shape[0]
    pair = lambda t: t.reshape(D, GDN_QK_H, 2)
    gates = jnp.concatenate([pair(b), pair(a), pair(a), pair(a)], axis=2).reshape(D, LANES)
    pad = jnp.zeros((D, GD_COLS - GD_GATES - LANES), w_in.dtype)
    return jnp.concatenate([qkv, zg, gates, pad], axis=1).astype(BF16)


def _router_body(x_ref, g_ref, wr_ref, o_ref):
    logits = _dot(_rms(x_ref[...], g_ref[...]), wr_ref[...], HI)
    lane = _iota(logits.shape, 1)
    logits = jnp.where(lane < N_EXPERTS, logits, -jnp.inf)
    m1 = jnp.max(logits, axis=-1, keepdims=True)
    i1 = jnp.min(jnp.where(logits == m1, lane, LANES), axis=-1, keepdims=True)
    rest = jnp.where(lane == i1, -jnp.inf, logits)
    m2 = jnp.max(rest, axis=-1, keepdims=True)
    i2 = jnp.min(jnp.where(rest == m2, lane, LANES), axis=-1, keepdims=True)
    e = jnp.exp(m2 - m1)
    w1 = 1.0 / (1.0 + e)
    o_ref[...] = jnp.where(lane == 0, i1.astype(F32),
                           jnp.where(lane == 1, i2.astype(F32),
                                     jnp.where(lane == 2, w1, jnp.where(lane == 3, e * w1, 0.0))))


def _router(x, g, w_router, *, tm):
    T, D = x.shape
    wr = jnp.pad(w_router, ((0, 0), (0, LANES - N_EXPERTS)))
    return pl.pallas_call(
        _router_body,
        grid=(T // tm,),
        in_specs=[pl.BlockSpec((tm, D), lambda i: (i, 0)),
                  pl.BlockSpec((1, D), lambda i: (0, 0)),
                  pl.BlockSpec((D, LANES), lambda i: (0, 0))],
        out_specs=pl.BlockSpec((tm, LANES), lambda i: (i, 0)),
        out_shape=jax.ShapeDtypeStruct((T, LANES), F32),
        compiler_params=_cp(("parallel",), 32),
        name="router",
    )(x, g.reshape(1, D), wr)


def _row_copy(src_hbm, row, dst, r, sem):
    return pltpu.make_async_copy(src_hbm.at[pl.ds(row, 1), :], dst.at[pl.ds(r, 1), :], sem)


def _experts_body(tok_ref, be_ref, nu_ref, x_hbm, g_ref, wg_ref, wu_ref, wd_ref, o_ref,
                  xbuf, xn_ref, sem):
    i = pl.program_id(0)
    f = pl.program_id(1)
    MB = xbuf.shape[1]
    active = i < nu_ref[0]
    slot = i % 2

    def gather(blk, s):
        def issue(r, c):
            _row_copy(x_hbm, tok_ref[blk * MB + r], xbuf.at[s], r, sem.at[s]).start()
            return c

        lax.fori_loop(0, MB, issue, 0, unroll=8)

    @pl.when(f == 0)
    def _():
        o_ref[...] = jnp.zeros_like(o_ref)

    @pl.when((f == 0) & (i == 0) & active)
    def _():
        gather(0, 0)

    @pl.when((f == 0) & active)
    def _():
        pltpu.make_async_copy(x_hbm.at[pl.ds(0, MB), :], xbuf.at[slot], sem.at[slot]).wait()
        xn_ref[...] = _rms(xbuf[slot], g_ref[...]).astype(BF16)

    @pl.when((f == 1) & (i + 1 < nu_ref[0]))
    def _():
        gather(i + 1, 1 - slot)

    @pl.when(active)
    def _():
        xn = xn_ref[...]
        h = (_silu(_dot(xn, wg_ref[...])) * _dot(xn, wu_ref[...])).astype(BF16)
        o_ref[...] += _dot(h, wd_ref[...])


def _experts(x, g, tok, blk_e, n_used, wg, wu, wd, *, tf):
    T, D = x.shape
    MB = MOE_BLOCK
    n_blk = tok.shape[0] // MB
    Fh = wg.shape[2]
    assert Fh // tf >= 2, "the next block's rows are requested during hidden chunk 1"
    fe = lambda i, f, nu: jnp.where(i < nu[0], f, 0)
    return pl.pallas_call(
        _experts_body,
        grid_spec=pltpu.PrefetchScalarGridSpec(
            num_scalar_prefetch=3,
            grid=(n_blk, Fh // tf),
            in_specs=[pl.BlockSpec(memory_space=pl.ANY),
                      pl.BlockSpec((1, D), lambda i, f, tk, be, nu: (0, 0)),
                      pl.BlockSpec((None, D, tf), lambda i, f, tk, be, nu: (be[i], 0, fe(i, f, nu))),
                      pl.BlockSpec((None, D, tf), lambda i, f, tk, be, nu: (be[i], 0, fe(i, f, nu))),
                      pl.BlockSpec((None, tf, D), lambda i, f, tk, be, nu: (be[i], fe(i, f, nu), 0))],
            out_specs=pl.BlockSpec((MB, D), lambda i, f, tk, be, nu: (i, 0)),
            scratch_shapes=[pltpu.VMEM((2, MB, D), F32), pltpu.VMEM((MB, D), BF16),
                            pltpu.SemaphoreType.DMA((2,))]),
        out_shape=jax.ShapeDtypeStruct((n_blk * MB, D), F32),
        compiler_params=_cp(("arbitrary", "arbitrary"), 52),
        name="experts",
    )(tok, blk_e, n_used, x, g.reshape(1, D), wg, wu, wd)


def _combine_body(slot_ref, y_hbm, x_ref, r_ref, o_ref, y0, y1, sem):
    i = pl.program_id(0)
    tm = x_ref.shape[0]

    def issue(r, c):
        a = (i * tm + r) * TOP_K
        _row_copy(y_hbm, slot_ref[a], y0, r, sem).start()
        _row_copy(y_hbm, slot_ref[a + 1], y1, r, sem).start()
        return c

    lax.fori_loop(0, tm, issue, 0, unroll=8)
    pltpu.make_async_copy(y_hbm.at[pl.ds(0, tm), :], y0, sem).wait()
    pltpu.make_async_copy(y_hbm.at[pl.ds(0, tm), :], y1, sem).wait()
    route = r_ref[...]
    o_ref[...] = x_ref[...] + y0[...] * route[:, 2:3] + y1[...] * route[:, 3:4]


def _combine(x, y, route, slots, *, tm):
    T, D = x.shape
    return pl.pallas_call(
        _combine_body,
        grid_spec=pltpu.PrefetchScalarGridSpec(
            num_scalar_prefetch=1,
            grid=(T // tm,),
            in_specs=[pl.BlockSpec(memory_space=pl.ANY),
                      pl.BlockSpec((tm, D), lambda i, s: (i, 0)),
                      pl.BlockSpec((tm, LANES), lambda i, s: (i, 0))],
            out_specs=pl.BlockSpec((tm, D), lambda i, s: (i, 0)),
            scratch_shapes=[pltpu.VMEM((tm, D), F32), pltpu.VMEM((tm, D), F32),
                            pltpu.SemaphoreType.DMA(())]),
        out_shape=jax.ShapeDtypeStruct((T, D), F32),
        compiler_params=_cp(("arbitrary",), 32),
        name="combine",
    )(slots, y, x, route)


def _moe(x, g, w_router, wg, wu, wd):
    T, D = x.shape
    MB = MOE_BLOCK
    A = T * TOP_K
    route = _router(x, g, w_router, tm=512)
    flat_e = route[:, :TOP_K].astype(I32).reshape(A)
    onehot = (flat_e[:, None] == jnp.arange(N_EXPERTS, dtype=I32)[None, :]).astype(I32)
    csum = jnp.cumsum(onehot, axis=0)
    rank = jnp.take_along_axis(csum, flat_e[:, None], axis=1)[:, 0] - 1
    padded = (csum[-1] + MB - 1) // MB * MB
    pend = jnp.cumsum(padded)
    slots = (pend - padded)[flat_e] + rank
    n_blk = A // MB + N_EXPERTS
    tok = jnp.zeros((n_blk * MB,), I32).at[slots].set(jnp.arange(A, dtype=I32) // TOP_K)
    blk_start = jnp.arange(n_blk, dtype=I32) * MB
    blk_e = jnp.minimum(jnp.sum(blk_start[:, None] >= pend[None, :], axis=1), N_EXPERTS - 1).astype(I32)
    n_used = (pend[-1:] // MB).astype(I32)
    y = _experts(x, g, tok, blk_e, n_used, wg, wu, wd, tf=min(1024, wg.shape[2]))
    return _combine(x, y, route, slots.astype(I32), tm=256)


def _ab_in_weight(w_in):
    r, k, v, wl, al, gl, zq, zc, zqi, zki, zwi = jnp.split(
        w_in, np.cumsum([1024, 1024, 1024, 96, 96, 256, 1024, 256, 1024, 64]).tolist(), axis=1)
    D = w_in.shape[0]
    z = lambda n: jnp.zeros((D, n), w_in.dtype)
    cols = [r, k, v, zq, zqi, gl, zc, wl, z(32), al, z(32), zki, zki, zwi, z(AB_COLS - AB_ZWI - 16)]
    return jnp.concatenate(cols, axis=1).astype(BF16)


def kernel(x, p, norm_mix, norm_ffn, ab_w_in, ab_mu, rwkv_w0, rwkv_w_up, rwkv_a0, rwkv_a_up, rwkv_g_up, rwkv_k_k, rwkv_k_a, rwkv_r_k, rwkv_gn_w, rwkv_gn_b, dsa_ckv_norm, dsa_w_uk, dsa_w_uv, ab_w_out, rel_bias, ffn_w_gate, ffn_w_up, ffn_w_down, gdn_w_in, gdn_conv, gdn_a_log, gdn_dt_bias, gdn_out_norm, gdn_w_out, moe_router, moe_w_gate, moe_w_up, moe_w_down, ple_norm, ple_w_gate, ple_w_proj, final_norm):
    B, L, D = x.shape
    T = B * L
    xf = x.reshape(T, D)
    bf = lambda w: w.astype(BF16)

    z = _norm_mm(xf, norm_mix[0], _ab_in_weight(ab_w_in[0]), tm=1024, tn=1024)
    y_a = _rwkv(z, B, L, ab_mu[0], rwkv_w0[0], rwkv_w_up[0], rwkv_a0[0], rwkv_a_up[0], rwkv_g_up[0],
                rwkv_k_k[0], rwkv_k_a[0], rwkv_r_k[0], rwkv_gn_w[0], rwkv_gn_b[0])
    y_b = _dsa(z, B, L, dsa_ckv_norm[0], dsa_w_uk[0], dsa_w_uv[0], rel_bias)
    xf = _mm_res([y_a, y_b], bf(ab_w_out[0]), xf, tm=1024, tn=1024)
    xf = _ffn(xf, norm_ffn[0], bf(ffn_w_gate[0]), bf(ffn_w_up[0]), bf(ffn_w_down[0]), tm=512, tf=512)
    xf = _ple(xf, ple_norm[0], bf(ple_w_gate[0]), p[0].reshape(T, PLE_DIM), bf(ple_w_proj[0]),
              final_norm, final=False, tm=512)

    z = _norm_mm(xf, norm_mix[1], _gd_in_weight(gdn_w_in[0]), tm=1024, tn=1280)
    gates = _gdn_gates(z, B, L, gdn_a_log[0], gdn_dt_bias[0])
    o = _gdn(z, gates, B, L, gdn_conv[0], gdn_out_norm[0])
    xf = _mm_res([o], bf(gdn_w_out[0]), xf, tm=512, tn=1024)
    xf = _moe(xf, norm_ffn[1], moe_router[0], bf(moe_w_gate[0]), bf(moe_w_up[0]), bf(moe_w_down[0]))
    xf = _ple(xf, ple_norm[1], bf(ple_w_gate[1]), p[1].reshape(T, PLE_DIM), bf(ple_w_proj[1]),
              final_norm, final=True, tm=512)
    return xf.reshape(B, L, D)
```

```python
import functools
import math

import numpy as np
import jax
import jax.numpy as jnp
from jax import lax
from jax.experimental import pallas as pl
from jax.experimental.pallas import tpu as pltpu

F32 = jnp.float32
BF16 = jnp.bfloat16
I32 = jnp.int32
HI = lax.Precision.HIGHEST

EPS = 1e-6
LANES = 128
MIB = 1024 * 1024

RWKV_H, RWKV_N = 16, 64
RWKV_W = RWKV_H * RWKV_N
RWKV_LORA = 96
RWKV_GATE = 256
RWKV_GN_EPS = 6.4e-4
RWKV_CHUNK = 64
RWKV_TB = 256

DSA_H, DSA_D, DSA_C = 8, 128, 256
IDX_H, IDX_D = 16, 64
TOPK_MAX = 256
QB = 128
KEY_GROUP = 4
ATTEND_GROUP = 4
REL_BUCKETS, REL_MAX_DIST = 32, 128
NEG = -1e30
INT_MIN = -(2 ** 31)

GDN_QK_H, GDN_V_H, GDN_D = 16, 32, 128
GDN_KW = GDN_QK_H * GDN_D
GDN_VW = GDN_V_H * GDN_D
GDN_CHUNK = 64
GDN_TB = 128
GDN_GATES_TB = 256
GDN_CONV_W = 4
GDN_HEADS_PER_STEP = 16

N_EXPERTS, TOP_K = 8, 2
MOE_BLOCK = 512
PLE_DIM = 256

AB_R, AB_K, AB_V = 0, 1024, 2048
AB_ZQ, AB_ZQI = 3072, 4096
AB_GL, AB_ZC = 5120, 5376
AB_WL, AB_AL = 5632, 5760
AB_ZKI, AB_ZWI = 5888, 6016
AB_COLS = 6144
GD_Q, GD_K, GD_V, GD_ZG, GD_GATES = 0, 2048, 4096, 8192, 12288
GD_COLS = 12800


def _cp(sem, vmem_mib):
    return pltpu.CompilerParams(dimension_semantics=sem, vmem_limit_bytes=vmem_mib * MIB)


def _dot(a, b, precision=None):
    return jnp.dot(a, b, preferred_element_type=F32, precision=precision)


def _dot_nt(a, b, precision=None):
    return lax.dot_general(a, b, (((1,), (1,)), ((), ())), preferred_element_type=F32,
                           precision=precision)


def _split(x):
    hi = x.astype(BF16)
    return hi, (x - hi.astype(F32)).astype(BF16)


def _seg_dot(x, ones_bf):
    hi, lo = _split(x)
    return _dot(hi, ones_bf) + _dot(lo, ones_bf)


def _rms(x, g):
    ms = jnp.mean(x * x, axis=-1, keepdims=True)
    return x * lax.rsqrt(ms + EPS) * g


def _sigmoid(x):
    return 0.5 * jnp.tanh(0.5 * x) + 0.5


def _silu(x):
    return x * _sigmoid(x)


def _softplus(x):
    return jnp.maximum(x, 0.0) + jnp.log1p(jnp.exp(-jnp.abs(x)))


def _iota(shape, dim):
    return lax.broadcasted_iota(I32, shape, dim)


def _norm_mm_body(x_ref, g_ref, w_ref, o_ref, xn_ref):
    @pl.when(pl.program_id(1) == 0)
    def _():
        xn_ref[...] = _rms(x_ref[...], g_ref[...]).astype(BF16)

    o_ref[...] = _dot(xn_ref[...], w_ref[...]).astype(o_ref.dtype)


def _norm_mm(x, g, w, *, tm, tn):
    T, K = x.shape
    N = w.shape[1]
    return pl.pallas_call(
        _norm_mm_body,
        grid=(T // tm, N // tn),
        in_specs=[pl.BlockSpec((tm, K), lambda i, j: (i, 0)),
                  pl.BlockSpec((1, K), lambda i, j: (0, 0)),
                  pl.BlockSpec((K, tn), lambda i, j: (0, j))],
        out_specs=pl.BlockSpec((tm, tn), lambda i, j: (i, j)),
        out_shape=jax.ShapeDtypeStruct((T, N), F32),
        scratch_shapes=[pltpu.VMEM((tm, K), BF16)],
        compiler_params=_cp(("parallel", "arbitrary"), 48),
        name="norm_mm",
    )(x, g.reshape(1, K), w)


def _mm_res_body(*refs):
    *a_refs, w_ref, r_ref, o_ref = refs
    acc = r_ref[...]
    off = 0
    for a_ref in a_refs:
        k = a_ref.shape[1]
        acc = acc + _dot(a_ref[...], w_ref[off:off + k, :])
        off += k
    o_ref[...] = acc


def _mm_res(parts, w, res, *, tm, tn):
    T = res.shape[0]
    K, N = w.shape
    assert sum(a.shape[1] for a in parts) == K
    return pl.pallas_call(
        _mm_res_body,
        grid=(T // tm, N // tn),
        in_specs=[pl.BlockSpec((tm, a.shape[1]), lambda i, j: (i, 0)) for a in parts]
        + [pl.BlockSpec((K, tn), lambda i, j: (0, j)),
           pl.BlockSpec((tm, tn), lambda i, j: (i, j))],
        out_specs=pl.BlockSpec((tm, tn), lambda i, j: (i, j)),
        out_shape=jax.ShapeDtypeStruct((T, N), F32),
        compiler_params=_cp(("parallel", "arbitrary"), 48),
        name="mm_res",
    )(*parts, w, res)


def _ple_body(x_ref, g_ref, wg_ref, p_ref, wp_ref, fg_ref, o_ref, *, final):
    x = x_ref[...]
    gate = _sigmoid(_dot(_rms(x, g_ref[...]).astype(BF16), wg_ref[...]))
    y = x + gate * _dot(p_ref[...].astype(BF16), wp_ref[...])
    if final:
        y = _rms(y, fg_ref[...])
    o_ref[...] = y


def _ple(x, g, wg, p, wp, fg, *, final, tm):
    T, D = x.shape
    P = p.shape[1]
    return pl.pallas_call(
        functools.partial(_ple_body, final=final),
        grid=(T // tm,),
        in_specs=[pl.BlockSpec((tm, D), lambda i: (i, 0)),
                  pl.BlockSpec((1, D), lambda i: (0, 0)),
                  pl.BlockSpec((D, D), lambda i: (0, 0)),
                  pl.BlockSpec((tm, P), lambda i: (i, 0)),
                  pl.BlockSpec((P, D), lambda i: (0, 0)),
                  pl.BlockSpec((1, D), lambda i: (0, 0))],
        out_specs=pl.BlockSpec((tm, D), lambda i: (i, 0)),
        out_shape=jax.ShapeDtypeStruct((T, D), F32),
        compiler_params=_cp(("parallel",), 48),
        name="ple",
    )(x, g.reshape(1, D), wg, p, wp, fg.reshape(1, D))


def _ffn_body(x_ref, g_ref, wg_ref, wu_ref, wd_ref, o_ref, xn_ref):
    @pl.when(pl.program_id(1) == 0)
    def _():
        x = x_ref[...]
        xn_ref[...] = _rms(x, g_ref[...]).astype(BF16)
        o_ref[...] = x

    xn = xn_ref[...]
    h = (_silu(_dot(xn, wg_ref[...])) * _dot(xn, wu_ref[...])).astype(BF16)
    o_ref[...] += _dot(h, wd_ref[...])


def _ffn(x, g, wg, wu, wd, *, tm, tf):
    T, D = x.shape
    Fh = wg.shape[1]
    return pl.pallas_call(
        _ffn_body,
        grid=(T // tm, Fh // tf),
        in_specs=[pl.BlockSpec((tm, D), lambda i, f: (i, 0)),
                  pl.BlockSpec((1, D), lambda i, f: (0, 0)),
                  pl.BlockSpec((D, tf), lambda i, f: (0, f)),
                  pl.BlockSpec((D, tf), lambda i, f: (0, f)),
                  pl.BlockSpec((tf, D), lambda i, f: (f, 0))],
        out_specs=pl.BlockSpec((tm, D), lambda i, f: (i, 0)),
        out_shape=jax.ShapeDtypeStruct((T, D), F32),
        scratch_shapes=[pltpu.VMEM((tm, D), BF16)],
        compiler_params=_cp(("parallel", "arbitrary"), 48),
        name="ffn",
    )(x, g.reshape(1, D), wg, wu, wd)


def _shift_mix(x, prev_row, mu):
    xs = pltpu.roll(x, 1, axis=0)
    xs = jnp.where(_iota(x.shape, 0) == 0, prev_row, xs)
    return x + (xs - x) * mu


def _bd(x, lo):
    return jnp.concatenate([jnp.where(lo, x, 0.0), jnp.where(lo, 0.0, x)], axis=0)


def _unit_lower_inverses(xs, eye, steps, sign):
    ps = [eye + x if sign > 0 else eye - x for x in xs]
    xbs = [x.astype(BF16) for x in xs]
    for _ in range(steps):
        xbs = [_dot(xb, xb).astype(BF16) for xb in xbs]
        ps = [p + _dot(p.astype(BF16), xb) for p, xb in zip(ps, xbs)]
    return ps


def _rwkv_body(rkv_ref, gl_ref, wa_ref, mu_rkv_ref, mu_gl_ref, mu_wa_ref, w0_ref, wup_ref,
               a0_ref, aup_ref, gup_ref, kk_ref, ka_ref, rk_ref, gnw_ref, gnb_ref, o_ref,
               prev_rkv, prev_gl, prev_wa, state, r_s, k_s, v_s, kn_s, a_s, lw_s, cg_s, g_s, y_s):
    TB = rkv_ref.shape[0]
    C = RWKV_CHUNK
    NP = RWKV_W // LANES

    @pl.when(pl.program_id(1) == 0)
    def _():
        prev_rkv[...] = jnp.zeros_like(prev_rkv)
        prev_gl[...] = jnp.zeros_like(prev_gl)
        prev_wa[...] = jnp.zeros_like(prev_wa)
        state[...] = jnp.zeros_like(state)

    lane = _iota((1, LANES), 1)
    lo = lane < RWKV_N
    r128 = _iota((LANES, LANES), 0)
    c128 = _iota((LANES, LANES), 1)
    same = (r128 // C) == (c128 // C)
    seg_ones = jnp.where(same, 1.0, 0.0).astype(BF16)
    eye = jnp.where(r128 == c128, 1.0, 0.0).astype(F32)
    strict = same & (c128 < r128)
    incl = same & (c128 <= r128)
    rt = _iota((TB, TB), 0)
    ct = _iota((TB, TB), 1)
    tri_chunks = jnp.where(((rt // C) == (ct // C)) & (ct <= rt), 1.0, 0.0).astype(BF16)

    wa_raw = wa_ref[...]
    wa = _shift_mix(wa_raw, prev_wa[...], mu_wa_ref[...])
    prev_wa[...] = wa_raw[TB - 1:TB, :]
    wl = jnp.tanh(wa[:, :LANES]).astype(BF16)
    al = wa[:, LANES:].astype(BF16)
    w = -_softplus(-(w0_ref[...] + _dot(wl, wup_ref[...]))) - 0.5
    lw = -jnp.exp(w)
    lw_s[...] = lw
    lw_hi, lw_lo = _split(lw)
    cg_s[...] = _dot(tri_chunks, lw_hi) + _dot(tri_chunks, lw_lo)
    a_s[...] = _sigmoid(a0_ref[...] + _dot(al, aup_ref[...]))
    gl_raw = gl_ref[...]
    gl = _shift_mix(gl_raw, prev_gl[...], mu_gl_ref[...])
    prev_gl[...] = gl_raw[TB - 1:TB, :]
    g_s[...] = _dot(_sigmoid(gl).astype(BF16), gup_ref[...])

    for p in range(NP):
        cs = slice(p * LANES, (p + 1) * LANES)
        cols = [slice(off + p * LANES, off + (p + 1) * LANES) for off in (AB_R, AB_K, AB_V)]
        mixed = []
        for c in cols:
            raw = rkv_ref[:, c]
            mixed.append(_shift_mix(raw, prev_rkv[:, c], mu_rkv_ref[:, c]))
            prev_rkv[:, c] = raw[TB - 1:TB, :]
        r, k, v = mixed
        kk = k * kk_ref[:, cs]
        kn_s[:, cs] = kk * lax.rsqrt(_seg_dot(kk * kk, seg_ones) + 1e-6)
        r_s[:, cs] = r
        k_s[:, cs] = k * (1.0 + (a_s[:, cs] - 1.0) * ka_ref[:, cs])
        v_s[:, cs] = v

    def chunk(c, carry):
        rows = pl.ds(pl.multiple_of(c * C, C), C)
        pairs = range(NP)
        css = [slice(p * LANES, (p + 1) * LANES) for p in pairs]
        gam_last, ar, bk, bk_end, v_f, v_t = [], [], [], [], [], []
        for cs in css:
            cg = cg_s[rows, cs]
            gam = jnp.exp(cg)
            gam_inv = jnp.exp(-cg)
            gam_prev = jnp.exp(cg - lw_s[rows, cs])
            gl_ = gam[C - 1:C, :]
            kn = kn_s[rows, cs]
            b_raw = kn * a_s[rows, cs] * gam_inv
            k_raw = k_s[rows, cs] * gam_inv
            gam_last.append(gl_)
            ar.append(jnp.concatenate([_bd(-kn * gam_prev, lo), _bd(r_s[rows, cs] * gam, lo)],
                                      axis=0).astype(BF16))
            bk.append(jnp.concatenate([_bd(b_raw, lo), _bd(k_raw, lo)], axis=0).astype(BF16))
            bk_end.append(jnp.concatenate([_bd(b_raw * gl_, lo), _bd(k_raw * gl_, lo)],
                                          axis=0).astype(BF16))
            vf = _bd(v_s[rows, cs], lo)
            v_f.append(vf)
            v_t.append(vf.astype(BF16))
        score = [_dot_nt(ar[p], bk[p]) for p in pairs]
        a_ab = [jnp.where(strict, s[:LANES, :LANES], 0.0) for s in score]
        a_akv = [_dot(jnp.where(strict, score[p][:LANES, LANES:], 0.0).astype(BF16), v_t[p])
                 for p in pairs]
        r_abk = [jnp.concatenate([jnp.where(incl, s[LANES:, :LANES], 0.0),
                                  jnp.where(incl, s[LANES:, LANES:], 0.0)], axis=1).astype(BF16)
                 for s in score]
        t_inv = [t.astype(BF16) for t in _unit_lower_inverses(a_ab, eye, 5, 1)]
        s_old = [state[p] for p in pairs]
        sproj = [_dot_nt(ar[p], s_old[p].astype(BF16)) for p in pairs]
        u = [_dot(t_inv[p], (sproj[p][:LANES] + a_akv[p]).astype(BF16)) for p in pairs]
        y_bd = [sproj[p][LANES:] + _dot(r_abk[p], jnp.concatenate([u[p].astype(BF16), v_t[p]], axis=0))
                for p in pairs]
        for p in pairs:
            y_s[rows, css[p]] = y_bd[p][:C] + y_bd[p][C:]
        uvT = [jnp.concatenate([u[p].T, v_f[p].T], axis=1).astype(BF16) for p in pairs]
        for p in pairs:
            state[p] = s_old[p] * gam_last[p] + _dot(uvT[p], bk_end[p])
        return carry

    lax.fori_loop(0, TB // C, chunk, 0)

    for p in range(NP):
        cs = slice(p * LANES, (p + 1) * LANES)
        y = y_s[:, cs]
        mean = _seg_dot(y, seg_ones) * (1.0 / RWKV_N)
        d = y - mean
        var = _seg_dot(d * d, seg_ones) * (1.0 / RWKV_N)
        yn = d * lax.rsqrt(var + RWKV_GN_EPS) * gnw_ref[:, cs] + gnb_ref[:, cs]
        bonus = _seg_dot(r_s[:, cs] * k_s[:, cs] * rk_ref[:, cs], seg_ones) * v_s[:, cs]
        o_ref[:, cs] = ((yn + bonus) * g_s[:, cs]).astype(o_ref.dtype)


def _rwkv(z, B, L, mu, w0, w_up, a0, a_up, g_up, k_k, k_a, r_k, gn_w, gn_b):
    TB = RWKV_TB
    nb = L // TB
    W = RWKV_W
    mu_r, mu_k, mu_v, mu_wl, mu_al, mu_gl = jnp.split(
        mu, np.cumsum([W, W, W, RWKV_LORA, RWKV_LORA])[:].tolist())
    pad = LANES - RWKV_LORA
    mu_rkv = jnp.concatenate([mu_r, mu_k, mu_v]).reshape(1, 3 * W)
    mu_wa = jnp.concatenate([jnp.pad(mu_wl, (0, pad)), jnp.pad(mu_al, (0, pad))]).reshape(1, 2 * LANES)
    wup = jnp.pad(w_up, ((0, pad), (0, 0))).astype(BF16)
    aup = jnp.pad(a_up, ((0, pad), (0, 0))).astype(BF16)
    row = lambda t: t.reshape(1, W)
    vec = lambda n: pl.BlockSpec((1, n), lambda b, i: (0, 0))
    big = lambda: pltpu.VMEM((TB, W), F32)
    return pl.pallas_call(
        _rwkv_body,
        grid=(B, nb),
        in_specs=[pl.BlockSpec((TB, 3 * W), lambda b, i: (b * nb + i, 0)),
                  pl.BlockSpec((TB, RWKV_GATE), lambda b, i: (b * nb + i, AB_GL // RWKV_GATE)),
                  pl.BlockSpec((TB, 2 * LANES), lambda b, i: (b * nb + i, AB_WL // (2 * LANES))),
                  vec(3 * W), vec(RWKV_GATE), vec(2 * LANES), vec(W),
                  pl.BlockSpec((LANES, W), lambda b, i: (0, 0)),
                  vec(W),
                  pl.BlockSpec((LANES, W), lambda b, i: (0, 0)),
                  pl.BlockSpec((RWKV_GATE, W), lambda b, i: (0, 0)),
                  vec(W), vec(W), vec(W), vec(W), vec(W)],
        out_specs=pl.BlockSpec((TB, W), lambda b, i: (b * nb + i, 0)),
        out_shape=jax.ShapeDtypeStruct((B * L, W), BF16),
        scratch_shapes=[pltpu.VMEM((1, 3 * W), F32), pltpu.VMEM((1, RWKV_GATE), F32),
                        pltpu.VMEM((1, 2 * LANES), F32),
                        pltpu.VMEM((W // LANES, LANES, LANES), F32),
                        big(), big(), big(), big(), big(), big(), big(), big(), big()],
        compiler_params=_cp(("arbitrary", "arbitrary"), 48),
        name="rwkv7",
    )(z, z, z, mu_rkv, mu_gl.reshape(1, RWKV_GATE), mu_wa, row(w0), wup, row(a0), aup,
      g_up.astype(BF16), row(k_k), row(k_a), row(r_k), row(gn_w), row(gn_b))


def _t5_bucket_np(dist):
    n = np.maximum(dist, 0)
    exact = REL_BUCKETS // 2
    ratio = np.log(np.maximum(n, 1).astype(np.float32) / np.float32(exact)) / np.float32(
        math.log(REL_MAX_DIST / exact))
    large = exact + (ratio.astype(np.float32) * np.float32(REL_BUCKETS - exact)).astype(np.int32)
    return np.where(n < exact, n, np.minimum(large, REL_BUCKETS - 1)).astype(np.int32)


def _near_buckets():
    kl = np.arange(QB)[:, None]
    ql = np.arange(QB)[None, :]
    return np.stack([_t5_bucket_np(ql - kl), _t5_bucket_np(QB + ql - kl)])


def _dsa_body(tbl_ref, bkt_ref, zq_ref, zqi_ref, zc_ref, zki_ref, zwi_ref, cn_ref, wukT_ref,
              wuvT_ref, o_ref, c_all, cT_all, kibd_all, sc, qiT, qlatT, bias, m_s, l_s, alpha_s,
              oT, *, topk):
    b = pl.program_id(0)
    qb = pl.program_id(1)
    lo = _iota((1, LANES), 1) < IDX_D
    krow = _iota((QB, QB), 0)
    qcol = _iota((QB, QB), 1)
    hsl = [slice(h * QB, (h + 1) * QB) for h in range(DSA_H)]

    @pl.when((b == 0) & (qb == 0))
    def _():
        for t in range(2):
            bk = bkt_ref[t]
            for h in range(DSA_H):
                far = tbl_ref[REL_BUCKETS - 1, h]
                acc = jnp.zeros((QB, QB), F32)
                for bb in range(REL_BUCKETS - 1):
                    acc = jnp.where(bk == bb, tbl_ref[bb, h] - far, acc)
                bias[t, h] = acc

    c_new = _rms(zc_ref[...], cn_ref[...])
    c_all[qb] = c_new.astype(BF16)
    cT_all[qb] = c_new.T.astype(BF16)
    kibd_all[qb] = _bd(zki_ref[...], lo).astype(BF16)

    for p in range(IDX_H // 2):
        qiT[:, p * QB:(p + 1) * QB] = zqi_ref[:, p * LANES:(p + 1) * LANES].T.astype(BF16)
    wT = zwi_ref[...].T * (IDX_D ** -0.5 * IDX_H ** -0.5)
    w_rows = [wT[h:h + 1, :] for h in range(IDX_H)]
    for h in range(DSA_H):
        qhT = zq_ref[:, hsl[h]].T.astype(BF16)
        qlatT[:, hsl[h]] = (_dot(wukT_ref[h], qhT) * DSA_D ** -0.5).astype(BF16)

    def score_blocks(j, nk):
        kb = kibd_all[pl.ds(j, nk)].reshape(nk * 2 * QB, LANES)
        acc = [jnp.zeros((QB, QB), F32) for _ in range(nk)]
        for pp in range(IDX_H // 4):
            s = _dot(kb, qiT[:, 2 * pp * QB:2 * (pp + 1) * QB])
            for k in range(nk):
                even = s[2 * k * QB:(2 * k + 1) * QB]
                odd = s[(2 * k + 1) * QB:(2 * k + 2) * QB]
                acc[k] = (acc[k] + w_rows[4 * pp] * jnp.maximum(even[:, :QB], 0.0)
                          + w_rows[4 * pp + 1] * jnp.maximum(odd[:, :QB], 0.0)
                          + w_rows[4 * pp + 2] * jnp.maximum(even[:, QB:], 0.0)
                          + w_rows[4 * pp + 3] * jnp.maximum(odd[:, QB:], 0.0))
        for k in range(nk):
            bits = lax.bitcast_convert_type(acc[k], I32)
            key = bits ^ ((bits >> 31) & 0x7FFFFFFF)
            sc[j + k] = jnp.where((j + k == qb) & (krow > qcol), INT_MIN, key)

    def score_group(jj, carry):
        score_blocks(jj * KEY_GROUP, KEY_GROUP)
        return carry

    def score_single(j, carry):
        score_blocks(j, 1)
        return carry

    n_sgroup = (qb + 1) // KEY_GROUP
    lax.fori_loop(0, n_sgroup, score_group, 0)
    lax.fori_loop(n_sgroup * KEY_GROUP, qb + 1, score_single, 0)

    n_quad = (qb + 1) // KEY_GROUP

    def count(pred):
        def quad(jj, a):
            blk = sc[pl.ds(jj * KEY_GROUP, KEY_GROUP)]
            for k in range(KEY_GROUP):
                a = a + jnp.where(pred(blk[k]), 1, 0)
            return a

        def single(j, a):
            return a + jnp.where(pred(sc[j]), 1, 0)

        a = lax.fori_loop(0, n_quad, quad, jnp.zeros((QB, QB), I32))
        a = lax.fori_loop(n_quad * KEY_GROUP, qb + 1, single, a)
        return jnp.sum(a, axis=0, keepdims=True)

    def bis_body(i, carry):
        t, n_t = carry
        cand = t ^ jnp.left_shift(jnp.int32(1), 31 - i)
        tot = count(lambda key: key >= cand)
        ok = tot >= topk
        return jnp.where(ok, cand, t), jnp.where(ok, tot, n_t)

    thr_raw, n_ge = lax.fori_loop(
        0, 32, bis_body,
        (jnp.full((1, QB), INT_MIN, I32), jnp.broadcast_to((qb + 1) * QB, (1, QB)).astype(I32)))
    thr = jnp.maximum(thr_raw, INT_MIN + 1)

    @pl.when(jnp.max(n_ge) > topk)
    def _():
        keep = (topk - count(lambda key: key > thr_raw)).astype(F32)
        tri = jnp.where(qcol <= krow, 1.0, 0.0).astype(BF16)

        def drop(j, seen):
            key = sc[j]
            tied = key == thr_raw
            tied_bf = jnp.where(tied, 1.0, 0.0).astype(BF16)
            rank = seen + _dot(tri, tied_bf)
            sc[j] = jnp.where(tied & (rank > keep), INT_MIN, key)
            return seen + jnp.sum(tied_bf.astype(F32), axis=0, keepdims=True)

        lax.fori_loop(0, qb + 1, drop, jnp.zeros((1, QB), F32))

    m_s[...] = jnp.full(m_s.shape, NEG, F32)
    l_s[...] = jnp.zeros_like(l_s)
    oT[...] = jnp.zeros_like(oT)

    def attend(j, nk, near):
        sel = sc[pl.ds(j, nk)].reshape(nk * QB, QB) >= thr
        lg = _dot(c_all[pl.ds(j, nk)].reshape(nk * QB, DSA_C), qlatT[...])
        cT = jnp.concatenate([cT_all[j + k] for k in range(nk)], axis=1)
        prs = []
        for h in range(DSA_H):
            lgh = lg[:, hsl[h]]
            if near is not None:
                lgh = lgh + bias[near, h]
            lgh = jnp.where(sel, lgh, NEG)
            m_old = m_s[:, hsl[h]]
            m_new = jnp.maximum(m_old, jnp.max(lgh, axis=0, keepdims=True))
            pr = jnp.exp(lgh - m_new)
            alpha_s[:, hsl[h]] = jnp.exp(m_old - m_new)
            l_s[:, hsl[h]] = alpha_s[:, hsl[h]] * l_s[:, hsl[h]] + jnp.sum(pr, axis=0, keepdims=True)
            m_s[:, hsl[h]] = m_new
            prs.append(pr.astype(BF16))
        oT[...] = alpha_s[...] * oT[...] + _dot(cT, jnp.concatenate(prs, axis=1))

    def far_group(jj, carry):
        attend(jj * ATTEND_GROUP, ATTEND_GROUP, None)
        return carry

    def far_single(j, carry):
        attend(j, 1, None)
        return carry

    n_far = jnp.maximum(qb - 1, 0)
    n_group = n_far // ATTEND_GROUP
    lax.fori_loop(0, n_group, far_group, 0)
    lax.fori_loop(n_group * ATTEND_GROUP, n_far, far_single, 0)

    @pl.when(qb >= 1)
    def _():
        attend(qb - 1, 1, 1)

    attend(qb, 1, 0)

    inv_l = 1.0 / l_s[...]
    for h in range(DSA_H):
        oh = (oT[:, hsl[h]] * inv_l[:, hsl[h]]).astype(BF16)
        o_ref[:, hsl[h]] = _dot(wuvT_ref[h], oh).T.astype(o_ref.dtype)


def _dsa(z, B, L, ckv_norm, w_uk, w_uv, rel_bias):
    nq = L // QB
    topk = min(TOPK_MAX, L // 4)
    wukT = jnp.swapaxes(w_uk, 1, 2).astype(BF16)
    wuvT = jnp.swapaxes(w_uv, 1, 2).astype(BF16)
    blk = lambda w, off: pl.BlockSpec((QB, w), lambda b, q: (b * nq + q, off // w))
    full = lambda shape: pl.BlockSpec(shape, lambda b, q: (0,) * len(shape))
    W = DSA_H * DSA_D
    return pl.pallas_call(
        functools.partial(_dsa_body, topk=topk),
        grid=(B, nq),
        in_specs=[pl.BlockSpec(memory_space=pltpu.SMEM),
                  full((2, QB, QB)),
                  blk(W, AB_ZQ), blk(IDX_H * IDX_D, AB_ZQI), blk(DSA_C, AB_ZC),
                  blk(LANES, AB_ZKI), blk(LANES, AB_ZWI),
                  full((1, DSA_C)), full((DSA_H, DSA_C, DSA_D)), full((DSA_H, DSA_D, DSA_C))],
        out_specs=pl.BlockSpec((QB, W), lambda b, q: (b * nq + q, 0)),
        out_shape=jax.ShapeDtypeStruct((B * L, W), BF16),
        scratch_shapes=[pltpu.VMEM((nq, QB, DSA_C), BF16), pltpu.VMEM((nq, DSA_C, QB), BF16),
                        pltpu.VMEM((nq, 2 * QB, LANES), BF16), pltpu.VMEM((nq, QB, QB), I32),
                        pltpu.VMEM((LANES, IDX_H // 2 * QB), BF16), pltpu.VMEM((DSA_C, W), BF16),
                        pltpu.VMEM((2, DSA_H, QB, QB), F32),
                        pltpu.VMEM((1, W), F32), pltpu.VMEM((1, W), F32), pltpu.VMEM((1, W), F32),
                        pltpu.VMEM((DSA_C, W), F32)],
        compiler_params=_cp(("arbitrary", "arbitrary"), 48),
        name="dsa",
    )(rel_bias, jnp.asarray(_near_buckets()), z, z, z, z, z, ckv_norm.reshape(1, DSA_C), wukT, wuvT)


def _gdn_gates_body(z_ref, alog_ref, dtb_ref, o_ref):
    TB = z_ref.shape[0]
    C = GDN_CHUNK
    z = z_ref[...]
    sub = _iota((1, LANES), 1) % 8
    beta = _sigmoid(z)
    g = -jnp.exp(alog_ref[...]) * _softplus(z + dtb_ref[...])
    rt = _iota((TB, TB), 0)
    ct = _iota((TB, TB), 1)
    same = (rt // C) == (ct // C)
    g_hi, g_rest = _split(g)
    g_mid, g_lo = _split(g - g_hi.astype(F32))
    tri = jnp.where(same & (ct <= rt), 1.0, 0.0).astype(BF16)
    blk = jnp.where(same, 1.0, 0.0).astype(BF16)
    cum = _dot(tri, g_hi) + _dot(tri, g_mid) + _dot(tri, g_lo)
    tot = _dot(blk, g_hi) + _dot(blk, g_mid) + _dot(blk, g_lo)
    tile =jnp.where(sub < 2, beta, jnp.where(sub < 4, g, jnp.where(sub < 6, cum, tot)))
    o_ref[...] = tile.T


def _gdn_gates(z, B, L, a_log, dt_bias):
    TB = GDN_GATES_TB
    nb = L // TB
    spread =lambda t: jnp.zeros((GDN_QK_H, 8), F32).at[:, 2:].set(
        jnp.tile(t.reshape(GDN_QK_H, 2), (1, 3))).reshape(1, LANES)
    return pl.pallas_call(
        _gdn_gates_body,
        grid=(B, nb),
        in_specs=[pl.BlockSpec((TB, LANES), lambda b, i: (b * nb + i, GD_GATES // LANES)),
                  pl.BlockSpec((1, LANES), lambda b, i: (0, 0)),
                  pl.BlockSpec((1, LANES), lambda b, i: (0, 0))],
        out_specs=pl.BlockSpec((None, LANES, TB), lambda b, i: (b, 0, i)),
        out_shape=jax.ShapeDtypeStruct((B, LANES, L), F32),
        compiler_params=_cp(("parallel", "parallel"), 32),
        name="gdn_gates",
    )(z, spread(a_log), spread(dt_bias))


def _gdn_body(zq_ref, zk_ref, zv_ref, zg_ref, gates_ref, cwq_ref, cwk_ref, cwv_ref, on_ref, o_ref,
              xbuf, state, q_s, k_s, v_s, *, hg):
    TB = zq_ref.shape[0]
    C = GDN_CHUNK
    D = GDN_D
    KW = GDN_CONV_W
    W = 4 * D * hg

    @pl.when(pl.program_id(2) == 0)
    def _():
        xbuf[0:8, :] = jnp.zeros((8, W), F32)
        state[...] = jnp.zeros_like(state)

    xbuf[8:TB + 8, 0:D * hg] = zq_ref[...]
    xbuf[8:TB + 8, D * hg:2 * D * hg] = zk_ref[...]
    xbuf[8:TB + 8, 2 * D * hg:W] = zv_ref[...]
    cw = jnp.concatenate([cwq_ref[...], cwk_ref[...], cwv_ref[...]], axis=1)
    l2 = lambda t: t * lax.rsqrt(jnp.sum(t * t, axis=-1, keepdims=True) + 1e-6)
    for g in range(W // D):
        cols = slice(g * D, (g + 1) * D)
        acc = jnp.zeros((TB, D), F32)
        for j in range(KW):
            acc = acc + cw[j:j + 1, cols] * xbuf[8 - (KW - 1) + j:8 - (KW - 1) + j + TB, cols]
        act = _silu(acc)
        if g < hg:
            q_s[:, cols] = l2(act) * D ** -0.5
        elif g < 2 * hg:
            k_s[:, (g - hg) * D:(g - hg + 1) * D] = l2(act)
        else:
            v_s[:, (g - 2 * hg) * D:(g - 2 * hg + 1) * D] = act
    xbuf[0:8, :] = xbuf[TB:TB + 8, :]

    lane = _iota((1, LANES), 1)
    lo = lane < C
    r128 = _iota((LANES, LANES), 0)
    c128 = _iota((LANES, LANES), 1)
    same = (r128 // C) == (c128 // C)
    eye_m = r128 == c128
    eye = jnp.where(eye_m, 1.0, 0.0).astype(F32)
    strict = same & (c128 < r128)
    incl = same & (c128 <= r128)

    heads = range(hg)
    for c in range(TB // C):
        rows = slice(c * C, (c + 1) * C)
        e_st, e_neg, fb_st, etot_cat, kc, qc, dec = [], [], [], [], [], [], []
        for h in heads:
            win = gates_ref[8 * h:8 * h + 8, (c // 2) * LANES:(c // 2 + 1) * LANES]
            win_sw = pltpu.roll(win, C, axis=1)
            first, second = (win, win_sw) if c % 2 == 0 else (win_sw, win)
            st = lambda r: jnp.where(lo, first[r:r + 1, :], second[r + 1:r + 2, :])
            beta, cum, tot = st(0), st(4), st(6)
            etot = jnp.exp(tot)
            etot_sw = pltpu.roll(etot, C, axis=1)
            e_st.append(jnp.exp(cum))
            e_neg.append(-jnp.exp(cum))
            fb_st.append(jnp.exp(tot - cum) * beta)
            etot_cat.append(jnp.concatenate([jnp.where(lo, etot, etot_sw),
                                             jnp.where(lo, etot_sw, etot)], axis=1))
            kc.append(k_s[rows, h * D:(h + 1) * D])
            qc.append(q_s[rows, h * D:(h + 1) * D])
            cum_b = jnp.broadcast_to(cum, (LANES, LANES))
            dec.append(jnp.exp(jnp.where(incl, cum_b.T - cum_b, NEG)) * beta)
        score = [_dot_nt(jnp.concatenate([kc[h], kc[h], qc[h], qc[h]], axis=0).astype(BF16),
                         jnp.concatenate([kc[h], kc[h]], axis=0).astype(BF16))
                 for h in heads]
        t_inv = _unit_lower_inverses(
            [jnp.where(strict, score[h][:LANES] * dec[h], 0.0) for h in heads], eye, 5, -1)
        t_cat = [jnp.concatenate([t_inv[h], t_inv[h] * e_neg[h]], axis=1).astype(BF16) for h in heads]
        a_cat = [jnp.concatenate([eye * e_st[h], score[h][LANES:] * dec[h]], axis=1).astype(BF16)
                 for h in heads]
        s_old = [state[h] for h in heads]
        proj = [_dot(jnp.concatenate([kc[h], qc[h]], axis=0).astype(BF16), s_old[h].astype(BF16))
                for h in heads]
        rhs = [jnp.concatenate([v_s[rows, 2 * h * D:(2 * h + 1) * D],
                                v_s[rows, (2 * h + 1) * D:(2 * h + 2) * D],
                                proj[h][:C, :D], proj[h][:C, D:]], axis=0).astype(BF16) for h in heads]
        vn = [_dot(t_cat[h], rhs[h]) for h in heads]
        vn_bf = [x.astype(BF16) for x in vn]
        o_st = [_dot(a_cat[h], jnp.concatenate(
            [proj[h][C:, :D].astype(BF16), proj[h][C:, D:].astype(BF16), vn_bf[h]], axis=0))
            for h in heads]
        fv = [_dot((eye * fb_st[h]).astype(BF16), vn_bf[h]) for h in heads]
        for h in heads:
            state[h] = s_old[h] * etot_cat[h] + _dot(
                kc[h].T.astype(BF16), jnp.concatenate([fv[h][:C], fv[h][C:]], axis=1).astype(BF16))
        for h in heads:
            for u in range(2):
                cols = slice((2 * h + u) * D, (2 * h + u + 1) * D)
                oh = o_st[h][u * C:(u + 1) * C]
                o_ref[rows, cols] = (_rms(oh, on_ref[...]) * _silu(zg_ref[rows, cols])).astype(o_ref.dtype)


def _gdn(z, gates, B, L, conv_w, out_norm):
    TB = GDN_TB
    nb = L // TB
    D = GDN_D
    hg = GDN_HEADS_PER_STEP
    zspec = lambda w, off: pl.BlockSpec((TB, w), lambda b, h, i: (b * nb + i, off // w + h))
    cspec = lambda w, off: pl.BlockSpec((GDN_CONV_W, w), lambda b, h, i: (0, off // w + h))
    return pl.pallas_call(
        functools.partial(_gdn_body, hg=hg),
        grid=(B, GDN_QK_H // hg, nb),
        in_specs=[zspec(D * hg, GD_Q), zspec(D * hg, GD_K), zspec(2 * D * hg, GD_V),
                  zspec(2 * D * hg, GD_ZG),
                  pl.BlockSpec((None, 8 * hg, TB), lambda b, h, i: (b, h, i)),
                  cspec(D * hg, GD_Q), cspec(D * hg, GD_K), cspec(2 * D * hg, GD_V),
                  pl.BlockSpec((1, D), lambda b, h, i: (0, 0))],
        out_specs=pl.BlockSpec((TB, 2 * D * hg), lambda b, h, i: (b * nb + i, h)),
        out_shape=jax.ShapeDtypeStruct((B * L, GDN_VW), BF16),
        scratch_shapes=[pltpu.VMEM((TB + 8, 4 * D * hg), F32), pltpu.VMEM((hg, D, 2 * D), F32),
                        pltpu.VMEM((TB, D * hg), F32), pltpu.VMEM((TB, D * hg), F32),
                        pltpu.VMEM((TB, 2 * D * hg), F32)],
        compiler_params=_cp(("parallel", "parallel", "arbitrary"), 40),
        name="gdn",
    )(z, z, z, z, gates, conv_w, conv_w, conv_w, out_norm.reshape(1, D))


def _gd_in_weight(w_in):
    w_in = w_in.astype(BF16)
    qkv, zg, b, a = jnp.split(w_in, np.cumsum([2 * GDN_KW + GDN_VW, GDN_VW, GDN_V_H]).tolist(), axis=1)
    D = w_in.shape[0]
    pair = lambda t: t.reshape(D, GDN_QK_H, 2)
    gates = jnp.concatenate([pair(b), pair(a), pair(a), pair(a)], axis=2).reshape(D, LANES)
    pad = jnp.zeros((D, GD_COLS - GD_GATES - LANES), w_in.dtype)
    return jnp.concatenate([qkv, zg, gates, pad], axis=1)


def _router_body(x_ref, g_ref, wr_ref, o_ref):
    x_hi, x_lo = _split(_rms(x_ref[...], g_ref[...]))
    w_hi, w_lo = _split(wr_ref[...])
    logits = _dot(x_hi, w_hi) + _dot(x_lo, w_hi) + _dot(x_hi, w_lo)
    lane = _iota(logits.shape, 1)
    logits = jnp.where(lane < N_EXPERTS, logits, -jnp.inf)
    m1 = jnp.max(logits, axis=-1, keepdims=True)
    i1 = jnp.min(jnp.where(logits == m1, lane, LANES), axis=-1, keepdims=True)
    rest = jnp.where(lane == i1, -jnp.inf, logits)
    m2 = jnp.max(rest, axis=-1, keepdims=True)
    i2 = jnp.min(jnp.where(rest == m2, lane, LANES), axis=-1, keepdims=True)
    e = jnp.exp(m2 - m1)
    w1 = 1.0 / (1.0 + e)
    o_ref[...] = jnp.where(lane == 0, i1.astype(F32),
                           jnp.where(lane == 1, i2.astype(F32),
                                     jnp.where(lane == 2, w1, jnp.where(lane == 3, e * w1, 0.0))))


def _router(x, g, w_router, *, tm):
    T, D = x.shape
    wr = jnp.pad(w_router, ((0, 0), (0, LANES - N_EXPERTS)))
    return pl.pallas_call(
        _router_body,
        grid=(T // tm,),
        in_specs=[pl.BlockSpec((tm, D), lambda i: (i, 0)),
                  pl.BlockSpec((1, D), lambda i: (0, 0)),
                  pl.BlockSpec((D, LANES), lambda i: (0, 0))],
        out_specs=pl.BlockSpec((tm, LANES), lambda i: (i, 0)),
        out_shape=jax.ShapeDtypeStruct((T, LANES), F32),
        compiler_params=_cp(("parallel",), 32),
        name="router",
    )(x, g.reshape(1, D), wr)


def _row_copy(src_hbm, row, dst, r, sem):
    return pltpu.make_async_copy(src_hbm.at[pl.ds(row, 1), :], dst.at[pl.ds(r, 1), :], sem)


def _experts_body(tok_ref, be_ref, nu_ref, x_hbm, g_ref, wg_ref, wu_ref, wd_ref, o_ref,
                  xbuf, xn_ref, sem):
    i = pl.program_id(0)
    f = pl.program_id(1)
    MB = xbuf.shape[1]
    active = i < nu_ref[0]
    slot = i % 2

    def gather(blk, s):
        def issue(r, c):
            _row_copy(x_hbm, tok_ref[blk * MB + r], xbuf.at[s], r, sem.at[s]).start()
            return c

        lax.fori_loop(0, MB, issue, 0, unroll=8)

    @pl.when(f == 0)
    def _():
        o_ref[...] = jnp.zeros_like(o_ref)

    @pl.when((f == 0) & (i == 0) & active)
    def _():
        gather(0, 0)

    @pl.when((f == 0) & active)
    def _():
        pltpu.make_async_copy(x_hbm.at[pl.ds(0, MB), :], xbuf.at[slot], sem.at[slot]).wait()
        xn_ref[...] = _rms(xbuf[slot], g_ref[...]).astype(BF16)

    @pl.when((f == 1) & (i + 1 < nu_ref[0]))
    def _():
        gather(i + 1, 1 - slot)

    @pl.when(active)
    def _():
        xn = xn_ref[...]
        h = (_silu(_dot(xn, wg_ref[...])) * _dot(xn, wu_ref[...])).astype(BF16)
        o_ref[...] += _dot(h, wd_ref[...])


def _experts(x, g, tok, blk_e, n_used, wg, wu, wd, *, tf):
    T, D = x.shape
    MB = MOE_BLOCK
    n_blk = tok.shape[0] // MB
    Fh = wg.shape[2]
    assert Fh // tf >= 2, "the next block's rows are requested during hidden chunk 1"
    fe = lambda i, f, nu: jnp.where(i < nu[0], f, 0)
    return pl.pallas_call(
        _experts_body,
        grid_spec=pltpu.PrefetchScalarGridSpec(
            num_scalar_prefetch=3,
            grid=(n_blk, Fh // tf),
            in_specs=[pl.BlockSpec(memory_space=pl.ANY),
                      pl.BlockSpec((1, D), lambda i, f, tk, be, nu: (0, 0)),
                      pl.BlockSpec((None, D, tf), lambda i, f, tk, be, nu: (be[i], 0, fe(i, f, nu))),
                      pl.BlockSpec((None, D, tf), lambda i, f, tk, be, nu: (be[i], 0, fe(i, f, nu))),
                      pl.BlockSpec((None, tf, D), lambda i, f, tk, be, nu: (be[i], fe(i, f, nu), 0))],
            out_specs=pl.BlockSpec((MB, D), lambda i, f, tk, be, nu: (i, 0)),
            scratch_shapes=[pltpu.VMEM((2, MB, D), F32), pltpu.VMEM((MB, D), BF16),
                            pltpu.SemaphoreType.DMA((2,))]),
        out_shape=jax.ShapeDtypeStruct((n_blk * MB, D), F32),
        compiler_params=_cp(("arbitrary", "arbitrary"), 52),
        name="experts",
    )(tok, blk_e, n_used, x, g.reshape(1, D), wg, wu, wd)


def _combine_ple_body(slot_ref, y_hbm, x_ref, r_ref, g_ref, wg_ref, p_ref, wp_ref, fg_ref, o_ref,
                      y0, y1, sem, *, final):
    i = pl.program_id(0)
    tm = x_ref.shape[0]
    slot = i % 2

    def gather(blk, s):
        def issue(r, c):
            a = (blk * tm + r) * TOP_K
            _row_copy(y_hbm, slot_ref[a], y0.at[s], r, sem.at[s]).start()
            _row_copy(y_hbm, slot_ref[a + 1], y1.at[s], r, sem.at[s]).start()
            return c

        lax.fori_loop(0, tm, issue, 0, unroll=8)

    @pl.when(i == 0)
    def _():
        gather(0, 0)

    pltpu.make_async_copy(y_hbm.at[pl.ds(0, tm), :], y0.at[slot], sem.at[slot]).wait()
    pltpu.make_async_copy(y_hbm.at[pl.ds(0, tm), :], y1.at[slot], sem.at[slot]).wait()

    @pl.when(i + 1 < pl.num_programs(0))
    def _():
        gather(i + 1, 1 - slot)

    route = r_ref[...]
    x = x_ref[...] + y0[slot] * route[:, 2:3] + y1[slot] * route[:, 3:4]
    gate = _sigmoid(_dot(_rms(x, g_ref[...]).astype(BF16), wg_ref[...]))
    y = x + gate * _dot(p_ref[...].astype(BF16), wp_ref[...])
    if final:
        y = _rms(y, fg_ref[...])
    o_ref[...] = y


def _combine_ple(x, y, route, slots, g, wg, p, wp, fg, *, final, tm):
    T, D = x.shape
    P = p.shape[1]
    const = lambda shape: pl.BlockSpec(shape, lambda i, s: (0, 0))
    return pl.pallas_call(
        functools.partial(_combine_ple_body, final=final),
        grid_spec=pltpu.PrefetchScalarGridSpec(
            num_scalar_prefetch=1,
            grid=(T // tm,),
            in_specs=[pl.BlockSpec(memory_space=pl.ANY),
                      pl.BlockSpec((tm, D), lambda i, s: (i, 0)),
                      pl.BlockSpec((tm, LANES), lambda i, s: (i, 0)),
                      const((1, D)), const((D, D)),
                      pl.BlockSpec((tm, P), lambda i, s: (i, 0)),
                      const((P, D)), const((1, D))],
            out_specs=pl.BlockSpec((tm, D), lambda i, s: (i, 0)),
            scratch_shapes=[pltpu.VMEM((2, tm, D), F32), pltpu.VMEM((2, tm, D), F32),
                            pltpu.SemaphoreType.DMA((2,))]),
        out_shape=jax.ShapeDtypeStruct((T, D), F32),
        compiler_params=_cp(("arbitrary",), 48),
        name="combine_ple",
    )(slots, y, x, route, g.reshape(1, D), wg, p, wp, fg.reshape(1, D))


def _moe(x, g, w_router, wg, wu, wd):
    T, D = x.shape
    MB = MOE_BLOCK
    A = T * TOP_K
    route = _router(x, g, w_router, tm=512)
    flat_e = route[:, :TOP_K].astype(I32).reshape(A)
    onehot = (flat_e[:, None] == jnp.arange(N_EXPERTS, dtype=I32)[None, :]).astype(I32)
    csum = jnp.cumsum(onehot, axis=0)
    rank = jnp.take_along_axis(csum, flat_e[:, None], axis=1)[:, 0] - 1
    padded = (csum[-1] + MB - 1) // MB * MB
    pend = jnp.cumsum(padded)
    slots = (pend - padded)[flat_e] + rank
    n_blk = A // MB + N_EXPERTS
    tok = jnp.zeros((n_blk * MB,), I32).at[slots].set(jnp.arange(A, dtype=I32) // TOP_K)
    blk_start = jnp.arange(n_blk, dtype=I32) * MB
    blk_e = jnp.minimum(jnp.sum(blk_start[:, None] >= pend[None, :], axis=1), N_EXPERTS - 1).astype(I32)
    n_used = (pend[-1:] // MB).astype(I32)
    y = _experts(x, g, tok, blk_e, n_used, wg, wu, wd, tf=min(1024, wg.shape[2]))
    return y, route, slots.astype(I32)


def _ab_in_weight(w_in):
    w_in = w_in.astype(BF16)
    r, k, v, wl, al, gl, zq, zc, zqi, zki, zwi = jnp.split(
        w_in, np.cumsum([1024, 1024, 1024, 96, 96, 256, 1024, 256, 1024, 64]).tolist(), axis=1)
    D = w_in.shape[0]
    z = lambda n: jnp.zeros((D, n), w_in.dtype)
    cols = [r, k, v, zq, zqi, gl, zc, wl, z(32), al, z(32), zki, zki, zwi, z(AB_COLS - AB_ZWI - 16)]
    return jnp.concatenate(cols, axis=1)


def kernel(x, p, norm_mix, norm_ffn, ab_w_in, ab_mu, rwkv_w0, rwkv_w_up, rwkv_a0, rwkv_a_up, rwkv_g_up, rwkv_k_k, rwkv_k_a, rwkv_r_k, rwkv_gn_w, rwkv_gn_b, dsa_ckv_norm, dsa_w_uk, dsa_w_uv, ab_w_out, rel_bias, ffn_w_gate, ffn_w_up, ffn_w_down, gdn_w_in, gdn_conv, gdn_a_log, gdn_dt_bias, gdn_out_norm, gdn_w_out, moe_router, moe_w_gate, moe_w_up, moe_w_down, ple_norm, ple_w_gate, ple_w_proj, final_norm):
    B, L, D = x.shape
    T = B * L
    xf = x.reshape(T, D)
    bf = lambda w: w.astype(BF16)

    z = _norm_mm(xf, norm_mix[0], _ab_in_weight(ab_w_in[0]), tm=1024, tn=1024)
    y_a = _rwkv(z, B, L, ab_mu[0], rwkv_w0[0], rwkv_w_up[0], rwkv_a0[0], rwkv_a_up[0], rwkv_g_up[0],
                rwkv_k_k[0], rwkv_k_a[0], rwkv_r_k[0], rwkv_gn_w[0], rwkv_gn_b[0])
    y_b = _dsa(z, B, L, dsa_ckv_norm[0], dsa_w_uk[0], dsa_w_uv[0], rel_bias)
    xf = _mm_res([y_a, y_b], bf(ab_w_out[0]), xf, tm=1024, tn=1024)
    xf = _ffn(xf, norm_ffn[0], bf(ffn_w_gate[0]), bf(ffn_w_up[0]), bf(ffn_w_down[0]), tm=512, tf=512)
    xf = _ple(xf, ple_norm[0], bf(ple_w_gate[0]), p[0].reshape(T, PLE_DIM), bf(ple_w_proj[0]),
              final_norm, final=False, tm=512)

    z = _norm_mm(xf, norm_mix[1], _gd_in_weight(gdn_w_in[0]), tm=1024, tn=1280)
    gates = _gdn_gates(z, B, L, gdn_a_log[0], gdn_dt_bias[0])
    o = _gdn(z, gates, B, L, gdn_conv[0], gdn_out_norm[0])
    xf = _mm_res([o], bf(gdn_w_out[0]), xf, tm=512, tn=1024)
    y, route, slots = _moe(xf, norm_ffn[1], moe_router[0], bf(moe_w_gate[0]), bf(moe_w_up[0]),
                           bf(moe_w_down[0]))
    xf = _combine_ple(xf, y, route, slots, ple_norm[1], bf(ple_w_gate[1]), p[1].reshape(T, PLE_DIM),
                      bf(ple_w_proj[1]), final_norm, final=True, tm=256)
    return xf.reshape(B, L, D)
```

```python
import functools
import math

import numpy as np
import jax
import jax.numpy as jnp
from jax import lax
from jax.experimental import pallas as pl
from jax.experimental.pallas import tpu as pltpu

F32 = jnp.float32
BF16 = jnp.bfloat16
I32 = jnp.int32
HI = lax.Precision.HIGHEST

EPS = 1e-6
LANES = 128
MIB = 1024 * 1024

RWKV_H, RWKV_N = 16, 64
RWKV_W = RWKV_H * RWKV_N
RWKV_LORA = 96
RWKV_GATE = 256
RWKV_GN_EPS = 6.4e-4
RWKV_CHUNK = 64
RWKV_TB = 256

DSA_H, DSA_D, DSA_C = 8, 128, 256
IDX_H, IDX_D = 16, 64
TOPK_MAX = 256
QB = 128
KEY_GROUP = 4
ATTEND_GROUP = 4
REL_BUCKETS, REL_MAX_DIST = 32, 128
NEG = -1e30
INT_MIN = -(2 ** 31)

GDN_QK_H, GDN_V_H, GDN_D = 16, 32, 128
GDN_KW = GDN_QK_H * GDN_D
GDN_VW = GDN_V_H * GDN_D
GDN_CHUNK = 64
GDN_TB = 128
GDN_GATES_TB = 256
GDN_CONV_W = 4
GDN_HEADS_PER_STEP = 16

N_EXPERTS, TOP_K = 8, 2
MOE_BLOCK = 512
PLE_DIM = 256

AB_R, AB_K, AB_V = 0, 1024, 2048
AB_ZQ, AB_ZQI = 3072, 4096
AB_GL, AB_ZC = 5120, 5376
AB_WL, AB_AL = 5632, 5760
AB_ZKI, AB_ZWI = 5888, 6016
AB_COLS = 6144
GD_Q, GD_K, GD_V, GD_ZG, GD_GATES = 0, 2048, 4096, 8192, 12288
GD_COLS = 12800


def _cp(sem, vmem_mib):
    return pltpu.CompilerParams(dimension_semantics=sem, vmem_limit_bytes=vmem_mib * MIB)


def _dot(a, b, precision=None):
    return jnp.dot(a, b, preferred_element_type=F32, precision=precision)


def _dot_nt(a, b, precision=None):
    return lax.dot_general(a, b, (((1,), (1,)), ((), ())), preferred_element_type=F32,
                           precision=precision)


def _split(x):
    hi = x.astype(BF16)
    return hi, (x - hi.astype(F32)).astype(BF16)


def _seg_dot(x, ones_bf):
    hi, lo = _split(x)
    return _dot(hi, ones_bf) + _dot(lo, ones_bf)


def _rms(x, g):
    ms = jnp.mean(x * x, axis=-1, keepdims=True)
    return x * lax.rsqrt(ms + EPS) * g


def _sigmoid(x):
    return 0.5 * jnp.tanh(0.5 * x) + 0.5


def _silu(x):
    return x * _sigmoid(x)


def _softplus(x):
    return jnp.maximum(x, 0.0) + jnp.log1p(jnp.exp(-jnp.abs(x)))


def _iota(shape, dim):
    return lax.broadcasted_iota(I32, shape, dim)


def _norm_mm_body(x_ref, g_ref, w_ref, o_ref, xn_ref):
    @pl.when(pl.program_id(1) == 0)
    def _():
        xn_ref[...] = _rms(x_ref[...], g_ref[...]).astype(BF16)

    o_ref[...] = _dot(xn_ref[...], w_ref[...]).astype(o_ref.dtype)


def _norm_mm(x, g, w, *, tm, tn):
    T, K = x.shape
    N = w.shape[1]
    return pl.pallas_call(
        _norm_mm_body,
        grid=(T // tm, N // tn),
        in_specs=[pl.BlockSpec((tm, K), lambda i, j: (i, 0)),
                  pl.BlockSpec((1, K), lambda i, j: (0, 0)),
                  pl.BlockSpec((K, tn), lambda i, j: (0, j))],
        out_specs=pl.BlockSpec((tm, tn), lambda i, j: (i, j)),
        out_shape=jax.ShapeDtypeStruct((T, N), F32),
        scratch_shapes=[pltpu.VMEM((tm, K), BF16)],
        compiler_params=_cp(("parallel", "arbitrary"), 48),
        name="norm_mm",
    )(x, g.reshape(1, K), w)


def _mm_res_body(*refs):
    *a_refs, w_ref, r_ref, o_ref = refs
    acc = r_ref[...]
    off = 0
    for a_ref in a_refs:
        k = a_ref.shape[1]
        acc = acc + _dot(a_ref[...], w_ref[off:off + k, :])
        off += k
    o_ref[...] = acc


def _mm_res(parts, w, res, *, tm, tn):
    T = res.shape[0]
    K, N = w.shape
    assert sum(a.shape[1] for a in parts) == K
    return pl.pallas_call(
        _mm_res_body,
        grid=(T // tm, N // tn),
        in_specs=[pl.BlockSpec((tm, a.shape[1]), lambda i, j: (i, 0)) for a in parts]
        + [pl.BlockSpec((K, tn), lambda i, j: (0, j)),
           pl.BlockSpec((tm, tn), lambda i, j: (i, j))],
        out_specs=pl.BlockSpec((tm, tn), lambda i, j: (i, j)),
        out_shape=jax.ShapeDtypeStruct((T, N), F32),
        compiler_params=_cp(("parallel", "arbitrary"), 48),
        name="mm_res",
    )(*parts, w, res)


def _ple_body(x_ref, g_ref, wg_ref, p_ref, wp_ref, fg_ref, o_ref, *, final):
    x = x_ref[...]
    gate = _sigmoid(_dot(_rms(x, g_ref[...]).astype(BF16), wg_ref[...]))
    y = x + gate * _dot(p_ref[...].astype(BF16), wp_ref[...])
    if final:
        y = _rms(y, fg_ref[...])
    o_ref[...] = y


def _ple(x, g, wg, p, wp, fg, *, final, tm):
    T, D = x.shape
    P = p.shape[1]
    return pl.pallas_call(
        functools.partial(_ple_body, final=final),
        grid=(T // tm,),
        in_specs=[pl.BlockSpec((tm, D), lambda i: (i, 0)),
                  pl.BlockSpec((1, D), lambda i: (0, 0)),
                  pl.BlockSpec((D, D), lambda i: (0, 0)),
                  pl.BlockSpec((tm, P), lambda i: (i, 0)),
                  pl.BlockSpec((P, D), lambda i: (0, 0)),
                  pl.BlockSpec((1, D), lambda i: (0, 0))],
        out_specs=pl.BlockSpec((tm, D), lambda i: (i, 0)),
        out_shape=jax.ShapeDtypeStruct((T, D), F32),
        compiler_params=_cp(("parallel",), 48),
        name="ple",
    )(x, g.reshape(1, D), wg, p, wp, fg.reshape(1, D))


def _ffn_body(x_ref, g_ref, wg_ref, wu_ref, wd_ref, o_ref, xn_ref):
    @pl.when(pl.program_id(1) == 0)
    def _():
        x = x_ref[...]
        xn_ref[...] = _rms(x, g_ref[...]).astype(BF16)
        o_ref[...] = x

    xn = xn_ref[...]
    h = (_silu(_dot(xn, wg_ref[...])) * _dot(xn, wu_ref[...])).astype(BF16)
    o_ref[...] += _dot(h, wd_ref[...])


def _ffn(x, g, wg, wu, wd, *, tm, tf):
    T, D = x.shape
    Fh = wg.shape[1]
    return pl.pallas_call(
        _ffn_body,
        grid=(T // tm, Fh // tf),
        in_specs=[pl.BlockSpec((tm, D), lambda i, f: (i, 0)),
                  pl.BlockSpec((1, D), lambda i, f: (0, 0)),
                  pl.BlockSpec((D, tf), lambda i, f: (0, f)),
                  pl.BlockSpec((D, tf), lambda i, f: (0, f)),
                  pl.BlockSpec((tf, D), lambda i, f: (f, 0))],
        out_specs=pl.BlockSpec((tm, D), lambda i, f: (i, 0)),
        out_shape=jax.ShapeDtypeStruct((T, D), F32),
        scratch_shapes=[pltpu.VMEM((tm, D), BF16)],
        compiler_params=_cp(("parallel", "arbitrary"), 56),
        name="ffn",
    )(x, g.reshape(1, D), wg, wu, wd)


def _shift_mix(x, prev_row, mu):
    xs = pltpu.roll(x, 1, axis=0)
    xs = jnp.where(_iota(x.shape, 0) == 0, prev_row, xs)
    return x + (xs - x) * mu


def _bd(x, lo):
    return jnp.concatenate([jnp.where(lo, x, 0.0), jnp.where(lo, 0.0, x)], axis=0)


def _unit_lower_inverses(xs, eye, steps, sign):
    ps = [eye + x if sign > 0 else eye - x for x in xs]
    xbs = [x.astype(BF16) for x in xs]
    for _ in range(steps):
        xbs = [_dot(xb, xb).astype(BF16) for xb in xbs]
        ps = [p + _dot(p.astype(BF16), xb) for p, xb in zip(ps, xbs)]
    return ps


def _rwkv_body(rkv_ref, gl_ref, wa_ref, mu_rkv_ref, mu_gl_ref, mu_wa_ref, w0_ref, wup_ref,
               a0_ref, aup_ref, gup_ref, kk_ref, ka_ref, rk_ref, gnw_ref, gnb_ref, o_ref,
               prev_rkv, prev_gl, prev_wa, state, r_s, k_s, v_s, kn_s, a_s, lw_s, cg_s, g_s, y_s):
    NB, TB = rkv_ref.shape[0], rkv_ref.shape[1]
    C = RWKV_CHUNK
    NP = RWKV_W // LANES

    @pl.when(pl.program_id(0) == 0)
    def _():
        prev_rkv[...] = jnp.zeros_like(prev_rkv)
        prev_gl[...] = jnp.zeros_like(prev_gl)
        prev_wa[...] = jnp.zeros_like(prev_wa)
        state[...] = jnp.zeros_like(state)

    lane = _iota((1, LANES), 1)
    lo = lane < RWKV_N
    r128 = _iota((LANES, LANES), 0)
    c128 = _iota((LANES, LANES), 1)
    same = (r128 // C) == (c128 // C)
    seg_ones = jnp.where(same, 1.0, 0.0).astype(BF16)
    eye = jnp.where(r128 == c128, 1.0, 0.0).astype(F32)
    strict = same & (c128 < r128)
    incl = same & (c128 <= r128)
    rt = _iota((TB, TB), 0)
    ct = _iota((TB, TB), 1)
    tri_chunks = jnp.where(((rt // C) == (ct // C)) & (ct <= rt), 1.0, 0.0).astype(BF16)

    for bi in range(NB):
        rb = slice(bi * TB, (bi + 1) * TB)
        pb = slice(bi, bi + 1)
        wa_raw = wa_ref[bi]
        wa = _shift_mix(wa_raw, prev_wa[pb, :], mu_wa_ref[...])
        prev_wa[pb, :] = wa_raw[TB - 1:TB, :]
        wl = jnp.tanh(wa[:, :LANES]).astype(BF16)
        al = wa[:, LANES:].astype(BF16)
        w = -_softplus(-(w0_ref[...] + _dot(wl, wup_ref[...]))) - 0.5
        lw = -jnp.exp(w)
        lw_s[rb, :] = lw
        lw_hi, lw_lo = _split(lw)
        cg_s[rb, :] = _dot(tri_chunks, lw_hi) + _dot(tri_chunks, lw_lo)
        a_s[rb, :] = _sigmoid(a0_ref[...] + _dot(al, aup_ref[...]))
        gl_raw = gl_ref[bi]
        gl = _shift_mix(gl_raw, prev_gl[pb, :], mu_gl_ref[...])
        prev_gl[pb, :] = gl_raw[TB - 1:TB, :]
        g_s[rb, :] = _dot(_sigmoid(gl).astype(BF16), gup_ref[...])

        for p in range(NP):
            cs = slice(p * LANES, (p + 1) * LANES)
            cols = [slice(off + p * LANES, off + (p + 1) * LANES) for off in (AB_R, AB_K, AB_V)]
            mixed = []
            for c in cols:
                raw = rkv_ref[bi, :, c]
                mixed.append(_shift_mix(raw, prev_rkv[pb, c], mu_rkv_ref[:, c]))
                prev_rkv[pb, c] = raw[TB - 1:TB, :]
            r, k, v = mixed
            kk = k * kk_ref[:, cs]
            kn_s[rb, cs] = kk * lax.rsqrt(_seg_dot(kk * kk, seg_ones) + 1e-6)
            r_s[rb, cs] = r
            k_s[rb, cs] = k * (1.0 + (a_s[rb, cs] - 1.0) * ka_ref[:, cs])
            v_s[rb, cs] = v

    def chunk(c, carry):
        pairs = range(NB * NP)
        rws = [pl.ds(pl.multiple_of((q // NP) * TB + c * C, C), C) for q in pairs]
        css = [slice((q % NP) * LANES, (q % NP + 1) * LANES) for q in pairs]
        gam_last, ar, bk, bk_end, v_f, v_t = [], [], [], [], [], []
        for rows, cs in zip(rws, css):
            cg = cg_s[rows, cs]
            gam = jnp.exp(cg)
            gam_inv = jnp.exp(-cg)
            gam_prev = jnp.exp(cg - lw_s[rows, cs])
            gl_ = gam[C - 1:C, :]
            kn = kn_s[rows, cs]
            b_raw = kn * a_s[rows, cs] * gam_inv
            k_raw = k_s[rows, cs] * gam_inv
            gam_last.append(gl_)
            ar.append(jnp.concatenate([_bd(-kn * gam_prev, lo), _bd(r_s[rows, cs] * gam, lo)],
                                      axis=0).astype(BF16))
            bk.append(jnp.concatenate([_bd(b_raw, lo), _bd(k_raw, lo)], axis=0).astype(BF16))
            bk_end.append(jnp.concatenate([_bd(b_raw * gl_, lo), _bd(k_raw * gl_, lo)],
                                          axis=0).astype(BF16))
            vf = _bd(v_s[rows, cs], lo)
            v_f.append(vf)
            v_t.append(vf.astype(BF16))
        score = [_dot_nt(ar[p], bk[p]) for p in pairs]
        a_ab = [jnp.where(strict, s[:LANES, :LANES], 0.0) for s in score]
        a_akv = [_dot(jnp.where(strict, score[p][:LANES, LANES:], 0.0).astype(BF16), v_t[p])
                 for p in pairs]
        r_abk = [jnp.concatenate([jnp.where(incl, s[LANES:, :LANES], 0.0),
                                  jnp.where(incl, s[LANES:, LANES:], 0.0)], axis=1).astype(BF16)
                 for s in score]
        t_inv = [t.astype(BF16) for t in _unit_lower_inverses(a_ab, eye, 5, 1)]
        s_old = [state[p] for p in pairs]
        sproj = [_dot_nt(ar[p], s_old[p].astype(BF16)) for p in pairs]
        u = [_dot(t_inv[p], (sproj[p][:LANES] + a_akv[p]).astype(BF16)) for p in pairs]
        y_bd = [sproj[p][LANES:] + _dot(r_abk[p], jnp.concatenate([u[p].astype(BF16), v_t[p]], axis=0))
                for p in pairs]
        for p in pairs:
            y_s[rws[p], css[p]] = y_bd[p][:C] + y_bd[p][C:]
        uvT = [jnp.concatenate([u[p].T, v_f[p].T], axis=1).astype(BF16) for p in pairs]
        for p in pairs:
            state[p] = s_old[p] * gam_last[p] + _dot(uvT[p], bk_end[p])
        return carry

    lax.fori_loop(0, TB // C, chunk, 0)

    for bi in range(NB):
        rb = slice(bi * TB, (bi + 1) * TB)
        for p in range(NP):
            cs = slice(p * LANES, (p + 1) * LANES)
            y = y_s[rb, cs]
            mean = _seg_dot(y, seg_ones) * (1.0 / RWKV_N)
            d = y - mean
            var = _seg_dot(d * d, seg_ones) * (1.0 / RWKV_N)
            yn = d * lax.rsqrt(var + RWKV_GN_EPS) * gnw_ref[:, cs] + gnb_ref[:, cs]
            bonus = _seg_dot(r_s[rb, cs] * k_s[rb, cs] * rk_ref[:, cs], seg_ones) * v_s[rb, cs]
            o_ref[bi, :, cs] = ((yn + bonus) * g_s[rb, cs]).astype(o_ref.dtype)


def _rwkv(z, B, L, mu, w0, w_up, a0, a_up, g_up, k_k, k_a, r_k, gn_w, gn_b):
    TB = RWKV_TB
    nb = L // TB
    W = RWKV_W
    mu_r, mu_k, mu_v, mu_wl, mu_al, mu_gl = jnp.split(
        mu, np.cumsum([W, W, W, RWKV_LORA, RWKV_LORA])[:].tolist())
    pad = LANES - RWKV_LORA
    mu_rkv = jnp.concatenate([mu_r, mu_k, mu_v]).reshape(1, 3 * W)
    mu_wa = jnp.concatenate([jnp.pad(mu_wl, (0, pad)), jnp.pad(mu_al, (0, pad))]).reshape(1, 2 * LANES)
    wup = jnp.pad(w_up, ((0, pad), (0, 0))).astype(BF16)
    aup = jnp.pad(a_up, ((0, pad), (0, 0))).astype(BF16)
    row = lambda t: t.reshape(1, W)
    vec = lambda n: pl.BlockSpec((1, n), lambda i: (0, 0))
    big = lambda: pltpu.VMEM((B * TB, W), F32)
    z3 = z.reshape(B, L, AB_COLS)
    out = pl.pallas_call(
        _rwkv_body,
        grid=(nb,),
        in_specs=[pl.BlockSpec((B, TB, 3 * W), lambda i: (0, i, 0)),
                  pl.BlockSpec((B, TB, RWKV_GATE), lambda i: (0, i, AB_GL // RWKV_GATE)),
                  pl.BlockSpec((B, TB, 2 * LANES), lambda i: (0, i, AB_WL // (2 * LANES))),
                  vec(3 * W), vec(RWKV_GATE), vec(2 * LANES), vec(W),
                  pl.BlockSpec((LANES, W), lambda i: (0, 0)),
                  vec(W),
                  pl.BlockSpec((LANES, W), lambda i: (0, 0)),
                  pl.BlockSpec((RWKV_GATE, W), lambda i: (0, 0)),
                  vec(W), vec(W), vec(W), vec(W), vec(W)],
        out_specs=pl.BlockSpec((B, TB, W), lambda i: (0, i, 0)),
        out_shape=jax.ShapeDtypeStruct((B, L, W), BF16),
        scratch_shapes=[pltpu.VMEM((B, 3 * W), F32), pltpu.VMEM((B, RWKV_GATE), F32),
                        pltpu.VMEM((B, 2 * LANES), F32),
                        pltpu.VMEM((B * W // LANES, LANES, LANES), F32),
                        big(), big(), big(), big(), big(), big(), big(), big(), big()],
        compiler_params=_cp(("arbitrary",), 56),
        name="rwkv7",
    )(z3, z3, z3, mu_rkv, mu_gl.reshape(1, RWKV_GATE), mu_wa, row(w0), wup, row(a0), aup,
      g_up.astype(BF16), row(k_k), row(k_a), row(r_k), row(gn_w), row(gn_b))
    return out.reshape(B * L, W)


def _t5_bucket_np(dist):
    n = np.maximum(dist, 0)
    exact = REL_BUCKETS // 2
    ratio = np.log(np.maximum(n, 1).astype(np.float32) / np.float32(exact)) / np.float32(
        math.log(REL_MAX_DIST / exact))
    large = exact + (ratio.astype(np.float32) * np.float32(REL_BUCKETS - exact)).astype(np.int32)
    return np.where(n < exact, n, np.minimum(large, REL_BUCKETS - 1)).astype(np.int32)


def _near_buckets():
    kl = np.arange(QB)[:, None]
    ql = np.arange(QB)[None, :]
    return np.stack([_t5_bucket_np(ql - kl), _t5_bucket_np(QB + ql - kl)])


def _dsa_body(tbl_ref, bkt_ref, zq_ref, zqi_ref, zc_ref, zki_ref, zwi_ref, cn_ref, wukT_ref,
              wuvT_ref, o_ref, c_all, cT_all, kibd_all, sc, qiT, qlatT, bias, m_s, l_s, alpha_s,
              oT, *, topk):
    b = pl.program_id(0)
    qb = pl.program_id(1)
    lo = _iota((1, LANES), 1) < IDX_D
    krow = _iota((QB, QB), 0)
    qcol = _iota((QB, QB), 1)
    hsl = [slice(h * QB, (h + 1) * QB) for h in range(DSA_H)]

    @pl.when((b == 0) & (qb == 0))
    def _():
        for t in range(2):
            bk = bkt_ref[t]
            for h in range(DSA_H):
                far = tbl_ref[REL_BUCKETS - 1, h]
                acc = jnp.zeros((QB, QB), F32)
                for bb in range(REL_BUCKETS - 1):
                    acc = jnp.where(bk == bb, tbl_ref[bb, h] - far, acc)
                bias[t, h] = acc

    c_new = _rms(zc_ref[...], cn_ref[...])
    c_all[qb] = c_new.astype(BF16)
    cT_all[qb] = c_new.T.astype(BF16)
    kibd_all[qb] = _bd(zki_ref[...], lo).astype(BF16)

    for p in range(IDX_H // 2):
        qiT[:, p * QB:(p + 1) * QB] = zqi_ref[:, p * LANES:(p + 1) * LANES].T.astype(BF16)
    wT = zwi_ref[...].T * (IDX_D ** -0.5 * IDX_H ** -0.5)
    w_rows = [wT[h:h + 1, :] for h in range(IDX_H)]
    for h in range(DSA_H):
        qhT = zq_ref[:, hsl[h]].T.astype(BF16)
        qlatT[:, hsl[h]] = (_dot(wukT_ref[h], qhT) * DSA_D ** -0.5).astype(BF16)

    def score_blocks(j, nk):
        kb = kibd_all[pl.ds(j, nk)].reshape(nk * 2 * QB, LANES)
        acc = [jnp.zeros((QB, QB), F32) for _ in range(nk)]
        for pp in range(IDX_H // 4):
            s = _dot(kb, qiT[:, 2 * pp * QB:2 * (pp + 1) * QB])
            for k in range(nk):
                even = s[2 * k * QB:(2 * k + 1) * QB]
                odd = s[(2 * k + 1) * QB:(2 * k + 2) * QB]
                acc[k] = (acc[k] + w_rows[4 * pp] * jnp.maximum(even[:, :QB], 0.0)
                          + w_rows[4 * pp + 1] * jnp.maximum(odd[:, :QB], 0.0)
                          + w_rows[4 * pp + 2] * jnp.maximum(even[:, QB:], 0.0)
                          + w_rows[4 * pp + 3] * jnp.maximum(odd[:, QB:], 0.0))
        for k in range(nk):
            bits = lax.bitcast_convert_type(acc[k], I32)
            key = bits ^ ((bits >> 31) & 0x7FFFFFFF)
            sc[j + k] = jnp.where((j + k == qb) & (krow > qcol), INT_MIN, key)

    def score_group(jj, carry):
        score_blocks(jj * KEY_GROUP, KEY_GROUP)
        return carry

    def score_single(j, carry):
        score_blocks(j, 1)
        return carry

    n_sgroup = (qb + 1) // KEY_GROUP
    lax.fori_loop(0, n_sgroup, score_group, 0)
    lax.fori_loop(n_sgroup * KEY_GROUP, qb + 1, score_single, 0)

    n_quad = (qb + 1) // KEY_GROUP

    def count(pred):
        def quad(jj, a):
            blk = sc[pl.ds(jj * KEY_GROUP, KEY_GROUP)]
            for k in range(KEY_GROUP):
                a = a + jnp.where(pred(blk[k]), 1, 0)
            return a

        def single(j, a):
            return a + jnp.where(pred(sc[j]), 1, 0)

        a = lax.fori_loop(0, n_quad, quad, jnp.zeros((QB, QB), I32))
        a = lax.fori_loop(n_quad * KEY_GROUP, qb + 1, single, a)
        return jnp.sum(a, axis=0, keepdims=True)

    def bis_body(i, carry):
        t, n_t = carry
        cand = t ^ jnp.left_shift(jnp.int32(1), 31 - i)
        tot = count(lambda key: key >= cand)
        ok = tot >= topk
        return jnp.where(ok, cand, t), jnp.where(ok, tot, n_t)

    thr_raw, n_ge = lax.fori_loop(
        0, 32, bis_body,
        (jnp.full((1, QB), INT_MIN, I32), jnp.broadcast_to((qb + 1) * QB, (1, QB)).astype(I32)))
    thr = jnp.maximum(thr_raw, INT_MIN + 1)

    @pl.when(jnp.max(n_ge) > topk)
    def _():
        keep = (topk - count(lambda key: key > thr_raw)).astype(F32)
        tri = jnp.where(qcol <= krow, 1.0, 0.0).astype(BF16)

        def drop(j, seen):
            key = sc[j]
            tied = key == thr_raw
            tied_bf = jnp.where(tied, 1.0, 0.0).astype(BF16)
            rank = seen + _dot(tri, tied_bf)
            sc[j] = jnp.where(tied & (rank > keep), INT_MIN, key)
            return seen + jnp.sum(tied_bf.astype(F32), axis=0, keepdims=True)

        lax.fori_loop(0, qb + 1, drop, jnp.zeros((1, QB), F32))

    m_s[...] = jnp.full(m_s.shape, NEG, F32)
    l_s[...] = jnp.zeros_like(l_s)
    oT[...] = jnp.zeros_like(oT)

    def attend(j, nk, near):
        sel = sc[pl.ds(j, nk)].reshape(nk * QB, QB) >= thr
        lg = _dot(c_all[pl.ds(j, nk)].reshape(nk * QB, DSA_C), qlatT[...])
        cT = jnp.concatenate([cT_all[j + k] for k in range(nk)], axis=1)
        prs = []
        for h in range(DSA_H):
            lgh = lg[:, hsl[h]]
            if near is not None:
                lgh = lgh + bias[near, h]
            lgh = jnp.where(sel, lgh, NEG)
            m_old = m_s[:, hsl[h]]
            m_new = jnp.maximum(m_old, jnp.max(lgh, axis=0, keepdims=True))
            pr = jnp.exp(lgh - m_new)
            alpha_s[:, hsl[h]] = jnp.exp(m_old - m_new)
            l_s[:, hsl[h]] = alpha_s[:, hsl[h]] * l_s[:, hsl[h]] + jnp.sum(pr, axis=0, keepdims=True)
            m_s[:, hsl[h]] = m_new
            prs.append(pr.astype(BF16))
        oT[...] = alpha_s[...] * oT[...] + _dot(cT, jnp.concatenate(prs, axis=1))

    def far_group(jj, carry):
        attend(jj * ATTEND_GROUP, ATTEND_GROUP, None)
        return carry

    def far_single(j, carry):
        attend(j, 1, None)
        return carry

    n_far = jnp.maximum(qb - 1, 0)
    n_group = n_far // ATTEND_GROUP
    lax.fori_loop(0, n_group, far_group, 0)
    lax.fori_loop(n_group * ATTEND_GROUP, n_far, far_single, 0)

    @pl.when(qb >= 1)
    def _():
        attend(qb - 1, 1, 1)

    attend(qb, 1, 0)

    inv_l = 1.0 / l_s[...]
    for h in range(DSA_H):
        oh = (oT[:, hsl[h]] * inv_l[:, hsl[h]]).astype(BF16)
        o_ref[:, hsl[h]] = _dot(wuvT_ref[h], oh).T.astype(o_ref.dtype)


def _dsa(z, B, L, ckv_norm, w_uk, w_uv, rel_bias):
    nq = L // QB
    topk = min(TOPK_MAX, L // 4)
    wukT = jnp.swapaxes(w_uk, 1, 2).astype(BF16)
    wuvT = jnp.swapaxes(w_uv, 1, 2).astype(BF16)
    blk = lambda w, off: pl.BlockSpec((QB, w), lambda b, q: (b * nq + q, off // w))
    full = lambda shape: pl.BlockSpec(shape, lambda b, q: (0,) * len(shape))
    W = DSA_H * DSA_D
    return pl.pallas_call(
        functools.partial(_dsa_body, topk=topk),
        grid=(B, nq),
        in_specs=[pl.BlockSpec(memory_space=pltpu.SMEM),
                  full((2, QB, QB)),
                  blk(W, AB_ZQ), blk(IDX_H * IDX_D, AB_ZQI), blk(DSA_C, AB_ZC),
                  blk(LANES, AB_ZKI), blk(LANES, AB_ZWI),
                  full((1, DSA_C)), full((DSA_H, DSA_C, DSA_D)), full((DSA_H, DSA_D, DSA_C))],
        out_specs=pl.BlockSpec((QB, W), lambda b, q: (b * nq + q, 0)),
        out_shape=jax.ShapeDtypeStruct((B * L, W), BF16),
        scratch_shapes=[pltpu.VMEM((nq, QB, DSA_C), BF16), pltpu.VMEM((nq, DSA_C, QB), BF16),
                        pltpu.VMEM((nq, 2 * QB, LANES), BF16), pltpu.VMEM((nq, QB, QB), I32),
                        pltpu.VMEM((LANES, IDX_H // 2 * QB), BF16), pltpu.VMEM((DSA_C, W), BF16),
                        pltpu.VMEM((2, DSA_H, QB, QB), F32),
                        pltpu.VMEM((1, W), F32), pltpu.VMEM((1, W), F32), pltpu.VMEM((1, W), F32),
                        pltpu.VMEM((DSA_C, W), F32)],
        compiler_params=_cp(("arbitrary", "arbitrary"), 48),
        name="dsa",
    )(rel_bias, jnp.asarray(_near_buckets()), z, z, z, z, z, ckv_norm.reshape(1, DSA_C), wukT, wuvT)


def _gdn_gates_body(z_ref, alog_ref, dtb_ref, o_ref):
    TB = z_ref.shape[0]
    C = GDN_CHUNK
    z = z_ref[...]
    sub = _iota((1, LANES), 1) % 8
    beta = _sigmoid(z)
    g = -jnp.exp(alog_ref[...]) * _softplus(z + dtb_ref[...])
    rt = _iota((TB, TB), 0)
    ct = _iota((TB, TB), 1)
    same = (rt // C) == (ct // C)
    g_hi, g_rest = _split(g)
    g_mid, g_lo = _split(g - g_hi.astype(F32))
    tri = jnp.where(same & (ct <= rt), 1.0, 0.0).astype(BF16)
    blk = jnp.where(same, 1.0, 0.0).astype(BF16)
    cum = _dot(tri, g_hi) + _dot(tri, g_mid) + _dot(tri, g_lo)
    tot = _dot(blk, g_hi) + _dot(blk, g_mid) + _dot(blk, g_lo)
    tile =jnp.where(sub < 2, beta, jnp.where(sub < 4, g, jnp.where(sub < 6, cum, tot)))
    o_ref[...] = tile.T


def _gdn_gates(z, B, L, a_log, dt_bias):
    TB = GDN_GATES_TB
    nb = L // TB
    spread =lambda t: jnp.zeros((GDN_QK_H, 8), F32).at[:, 2:].set(
        jnp.tile(t.reshape(GDN_QK_H, 2), (1, 3))).reshape(1, LANES)
    return pl.pallas_call(
        _gdn_gates_body,
        grid=(B, nb),
        in_specs=[pl.BlockSpec((TB, LANES), lambda b, i: (b * nb + i, GD_GATES // LANES)),
                  pl.BlockSpec((1, LANES), lambda b, i: (0, 0)),
                  pl.BlockSpec((1, LANES), lambda b, i: (0, 0))],
        out_specs=pl.BlockSpec((None, LANES, TB), lambda b, i: (b, 0, i)),
        out_shape=jax.ShapeDtypeStruct((B, LANES, L), F32),
        compiler_params=_cp(("parallel", "parallel"), 32),
        name="gdn_gates",
    )(z, spread(a_log), spread(dt_bias))


def _gdn_body(zq_ref, zk_ref, zv_ref, zg_ref, gates_ref, cwq_ref, cwk_ref, cwv_ref, on_ref, o_ref,
              xbuf, state, q_s, k_s, v_s, *, hg):
    TB = zq_ref.shape[0]
    C = GDN_CHUNK
    D = GDN_D
    KW = GDN_CONV_W
    W = 4 * D * hg

    @pl.when(pl.program_id(2) == 0)
    def _():
        xbuf[0:8, :] = jnp.zeros((8, W), F32)
        state[...] = jnp.zeros_like(state)

    xbuf[8:TB + 8, 0:D * hg] = zq_ref[...]
    xbuf[8:TB + 8, D * hg:2 * D * hg] = zk_ref[...]
    xbuf[8:TB + 8, 2 * D * hg:W] = zv_ref[...]
    cw = jnp.concatenate([cwq_ref[...], cwk_ref[...], cwv_ref[...]], axis=1)
    l2 = lambda t: t * lax.rsqrt(jnp.sum(t * t, axis=-1, keepdims=True) + 1e-6)
    for g in range(W // D):
        cols = slice(g * D, (g + 1) * D)
        acc = jnp.zeros((TB, D), F32)
        for j in range(KW):
            acc = acc + cw[j:j + 1, cols] * xbuf[8 - (KW - 1) + j:8 - (KW - 1) + j + TB, cols]
        act = _silu(acc)
        if g < hg:
            q_s[:, cols] = l2(act) * D ** -0.5
        elif g < 2 * hg:
            k_s[:, (g - hg) * D:(g - hg + 1) * D] = l2(act)
        else:
            v_s[:, (g - 2 * hg) * D:(g - 2 * hg + 1) * D] = act
    xbuf[0:8, :] = xbuf[TB:TB + 8, :]

    lane = _iota((1, LANES), 1)
    lo = lane < C
    r128 = _iota((LANES, LANES), 0)
    c128 = _iota((LANES, LANES), 1)
    same = (r128 // C) == (c128 // C)
    eye_m = r128 == c128
    eye = jnp.where(eye_m, 1.0, 0.0).astype(F32)
    strict = same & (c128 < r128)
    incl = same & (c128 <= r128)

    heads = range(hg)
    for c in range(TB // C):
        rows = slice(c * C, (c + 1) * C)
        e_st, e_neg, fb_st, etot_cat, kc, qc, dec = [], [], [], [], [], [], []
        for h in heads:
            win = gates_ref[8 * h:8 * h + 8, (c // 2) * LANES:(c // 2 + 1) * LANES]
            win_sw = pltpu.roll(win, C, axis=1)
            first, second = (win, win_sw) if c % 2 == 0 else (win_sw, win)
            st = lambda r: jnp.where(lo, first[r:r + 1, :], second[r + 1:r + 2, :])
            beta, cum, tot = st(0), st(4), st(6)
            etot = jnp.exp(tot)
            etot_sw = pltpu.roll(etot, C, axis=1)
            e_st.append(jnp.exp(cum))
            e_neg.append(-jnp.exp(cum))
            fb_st.append(jnp.exp(tot - cum) * beta)
            etot_cat.append(jnp.concatenate([jnp.where(lo, etot, etot_sw),
                                             jnp.where(lo, etot_sw, etot)], axis=1))
            kc.append(k_s[rows, h * D:(h + 1) * D])
            qc.append(q_s[rows, h * D:(h + 1) * D])
            cum_b = jnp.broadcast_to(cum, (LANES, LANES))
            dec.append(jnp.exp(jnp.where(incl, cum_b.T - cum_b, NEG)) * beta)
        score = [_dot_nt(jnp.concatenate([kc[h], kc[h], qc[h], qc[h]], axis=0).astype(BF16),
                         jnp.concatenate([kc[h], kc[h]], axis=0).astype(BF16))
                 for h in heads]
        t_inv = _unit_lower_inverses(
            [jnp.where(strict, score[h][:LANES] * dec[h], 0.0) for h in heads], eye, 5, -1)
        t_cat = [jnp.concatenate([t_inv[h], t_inv[h] * e_neg[h]], axis=1).astype(BF16) for h in heads]
        a_cat = [jnp.concatenate([eye * e_st[h], score[h][LANES:] * dec[h]], axis=1).astype(BF16)
                 for h in heads]
        s_old = [state[h] for h in heads]
        proj = [_dot(jnp.concatenate([kc[h], qc[h]], axis=0).astype(BF16), s_old[h].astype(BF16))
                for h in heads]
        rhs = [jnp.concatenate([v_s[rows, 2 * h * D:(2 * h + 1) * D],
                                v_s[rows, (2 * h + 1) * D:(2 * h + 2) * D],
                                proj[h][:C, :D], proj[h][:C, D:]], axis=0).astype(BF16) for h in heads]
        vn = [_dot(t_cat[h], rhs[h]) for h in heads]
        vn_bf = [x.astype(BF16) for x in vn]
        o_st = [_dot(a_cat[h], jnp.concatenate(
            [proj[h][C:, :D].astype(BF16), proj[h][C:, D:].astype(BF16), vn_bf[h]], axis=0))
            for h in heads]
        fv = [_dot((eye * fb_st[h]).astype(BF16), vn_bf[h]) for h in heads]
        for h in heads:
            state[h] = s_old[h] * etot_cat[h] + _dot(
                kc[h].T.astype(BF16), jnp.concatenate([fv[h][:C], fv[h][C:]], axis=1).astype(BF16))
        for h in heads:
            for u in range(2):
                cols = slice((2 * h + u) * D, (2 * h + u + 1) * D)
                oh = o_st[h][u * C:(u + 1) * C]
                o_ref[rows, cols] = (_rms(oh, on_ref[...]) * _silu(zg_ref[rows, cols])).astype(o_ref.dtype)


def _gdn(z, gates, B, L, conv_w, out_norm):
    TB = GDN_TB
    nb = L // TB
    D = GDN_D
    hg = GDN_HEADS_PER_STEP
    zspec = lambda w, off: pl.BlockSpec((TB, w), lambda b, h, i: (b * nb + i, off // w + h))
    cspec = lambda w, off: pl.BlockSpec((GDN_CONV_W, w), lambda b, h, i: (0, off // w + h))
    return pl.pallas_call(
        functools.partial(_gdn_body, hg=hg),
        grid=(B, GDN_QK_H // hg, nb),
        in_specs=[zspec(D * hg, GD_Q), zspec(D * hg, GD_K), zspec(2 * D * hg, GD_V),
                  zspec(2 * D * hg, GD_ZG),
                  pl.BlockSpec((None, 8 * hg, TB), lambda b, h, i: (b, h, i)),
                  cspec(D * hg, GD_Q), cspec(D * hg, GD_K), cspec(2 * D * hg, GD_V),
                  pl.BlockSpec((1, D), lambda b, h, i: (0, 0))],
        out_specs=pl.BlockSpec((TB, 2 * D * hg), lambda b, h, i: (b * nb + i, h)),
        out_shape=jax.ShapeDtypeStruct((B * L, GDN_VW), BF16),
        scratch_shapes=[pltpu.VMEM((TB + 8, 4 * D * hg), F32), pltpu.VMEM((hg, D, 2 * D), F32),
                        pltpu.VMEM((TB, D * hg), F32), pltpu.VMEM((TB, D * hg), F32),
                        pltpu.VMEM((TB, 2 * D * hg), F32)],
        compiler_params=_cp(("parallel", "parallel", "arbitrary"), 40),
        name="gdn",
    )(z, z, z, z, gates, conv_w, conv_w, conv_w, out_norm.reshape(1, D))


def _gd_in_weight(w_in):
    w_in = w_in.astype(BF16)
    qkv, zg, b, a = jnp.split(w_in, np.cumsum([2 * GDN_KW + GDN_VW, GDN_VW, GDN_V_H]).tolist(), axis=1)
    D = w_in.shape[0]
    pair = lambda t: t.reshape(D, GDN_QK_H, 2)
    gates = jnp.concatenate([pair(b), pair(a), pair(a), pair(a)], axis=2).reshape(D, LANES)
    pad = jnp.zeros((D, GD_COLS - GD_GATES - LANES), w_in.dtype)
    return jnp.concatenate([qkv, zg, gates, pad], axis=1)


def _router_body(x_ref, g_ref, wr_ref, o_ref):
    x_hi, x_lo = _split(_rms(x_ref[...], g_ref[...]))
    w_hi, w_lo = _split(wr_ref[...])
    logits = _dot(x_hi, w_hi) + _dot(x_lo, w_hi) + _dot(x_hi, w_lo)
    lane = _iota(logits.shape, 1)
    logits = jnp.where(lane < N_EXPERTS, logits, -jnp.inf)
    m1 = jnp.max(logits, axis=-1, keepdims=True)
    i1 = jnp.min(jnp.where(logits == m1, lane, LANES), axis=-1, keepdims=True)
    rest = jnp.where(lane == i1, -jnp.inf, logits)
    m2 = jnp.max(rest, axis=-1, keepdims=True)
    i2 = jnp.min(jnp.where(rest == m2, lane, LANES), axis=-1, keepdims=True)
    e = jnp.exp(m2 - m1)
    w1 = 1.0 / (1.0 + e)
    o_ref[...] = jnp.where(lane == 0, i1.astype(F32),
                           jnp.where(lane == 1, i2.astype(F32),
                                     jnp.where(lane == 2, w1, jnp.where(lane == 3, e * w1, 0.0))))


def _router(x, g, w_router, *, tm):
    T, D = x.shape
    wr = jnp.pad(w_router, ((0, 0), (0, LANES - N_EXPERTS)))
    return pl.pallas_call(
        _router_body,
        grid=(T // tm,),
        in_specs=[pl.BlockSpec((tm, D), lambda i: (i, 0)),
                  pl.BlockSpec((1, D), lambda i: (0, 0)),
                  pl.BlockSpec((D, LANES), lambda i: (0, 0))],
        out_specs=pl.BlockSpec((tm, LANES), lambda i: (i, 0)),
        out_shape=jax.ShapeDtypeStruct((T, LANES), F32),
        compiler_params=_cp(("parallel",), 32),
        name="router",
    )(x, g.reshape(1, D), wr)


def _row_copy(src_hbm, row, dst, r, sem):
    return pltpu.make_async_copy(src_hbm.at[pl.ds(row, 1), :], dst.at[pl.ds(r, 1), :], sem)


def _experts_body(tok_ref, be_ref, nu_ref, x_hbm, g_ref, wg_ref, wu_ref, wd_ref, o_ref,
                  xbuf, xn_ref, sem):
    i = pl.program_id(0)
    f = pl.program_id(1)
    MB = xbuf.shape[1]
    active = i < nu_ref[0]
    slot = i % 2

    def gather(blk, s):
        def issue(r, c):
            _row_copy(x_hbm, tok_ref[blk * MB + r], xbuf.at[s], r, sem.at[s]).start()
            return c

        lax.fori_loop(0, MB, issue, 0, unroll=8)

    @pl.when(f == 0)
    def _():
        o_ref[...] = jnp.zeros_like(o_ref)

    @pl.when((f == 0) & (i == 0) & active)
    def _():
        gather(0, 0)

    @pl.when((f == 0) & active)
    def _():
        pltpu.make_async_copy(x_hbm.at[pl.ds(0, MB), :], xbuf.at[slot], sem.at[slot]).wait()
        xn_ref[...] = _rms(xbuf[slot], g_ref[...]).astype(BF16)

    @pl.when((f == 1) & (i + 1 < nu_ref[0]))
    def _():
        gather(i + 1, 1 - slot)

    @pl.when(active)
    def _():
        xn = xn_ref[...]
        h = (_silu(_dot(xn, wg_ref[...])) * _dot(xn, wu_ref[...])).astype(BF16)
        o_ref[...] += _dot(h, wd_ref[...])


def _experts(x, g, tok, blk_e, n_used, wg, wu, wd, *, tf):
    T, D = x.shape
    MB = MOE_BLOCK
    n_blk = tok.shape[0] // MB
    Fh = wg.shape[2]
    assert Fh // tf >= 2, "the next block's rows are requested during hidden chunk 1"
    fe = lambda i, f, nu: jnp.where(i < nu[0], f, 0)
    return pl.pallas_call(
        _experts_body,
        grid_spec=pltpu.PrefetchScalarGridSpec(
            num_scalar_prefetch=3,
            grid=(n_blk, Fh // tf),
            in_specs=[pl.BlockSpec(memory_space=pl.ANY),
                      pl.BlockSpec((1, D), lambda i, f, tk, be, nu: (0, 0)),
                      pl.BlockSpec((None, D, tf), lambda i, f, tk, be, nu: (be[i], 0, fe(i, f, nu))),
                      pl.BlockSpec((None, D, tf), lambda i, f, tk, be, nu: (be[i], 0, fe(i, f, nu))),
                      pl.BlockSpec((None, tf, D), lambda i, f, tk, be, nu: (be[i], fe(i, f, nu), 0))],
            out_specs=pl.BlockSpec((MB, D), lambda i, f, tk, be, nu: (i, 0)),
            scratch_shapes=[pltpu.VMEM((2, MB, D), F32), pltpu.VMEM((MB, D), BF16),
                            pltpu.SemaphoreType.DMA((2,))]),
        out_shape=jax.ShapeDtypeStruct((n_blk * MB, D), F32),
        compiler_params=_cp(("arbitrary", "arbitrary"), 52),
        name="experts",
    )(tok, blk_e, n_used, x, g.reshape(1, D), wg, wu, wd)


def _combine_ple_body(slot_ref, y_hbm, x_ref, r_ref, g_ref, wg_ref, p_ref, wp_ref, fg_ref, o_ref,
                      y0, y1, sem, *, final):
    i = pl.program_id(0)
    tm = x_ref.shape[0]
    slot = i % 2

    def gather(blk, s):
        def issue(r, c):
            a = (blk * tm + r) * TOP_K
            _row_copy(y_hbm, slot_ref[a], y0.at[s], r, sem.at[s]).start()
            _row_copy(y_hbm, slot_ref[a + 1], y1.at[s], r, sem.at[s]).start()
            return c

        lax.fori_loop(0, tm, issue, 0, unroll=8)

    @pl.when(i == 0)
    def _():
        gather(0, 0)

    pltpu.make_async_copy(y_hbm.at[pl.ds(0, tm), :], y0.at[slot], sem.at[slot]).wait()
    pltpu.make_async_copy(y_hbm.at[pl.ds(0, tm), :], y1.at[slot], sem.at[slot]).wait()

    @pl.when(i + 1 < pl.num_programs(0))
    def _():
        gather(i + 1, 1 - slot)

    route = r_ref[...]
    x = x_ref[...] + y0[slot] * route[:, 2:3] + y1[slot] * route[:, 3:4]
    gate = _sigmoid(_dot(_rms(x, g_ref[...]).astype(BF16), wg_ref[...]))
    y = x + gate * _dot(p_ref[...].astype(BF16), wp_ref[...])
    if final:
        y = _rms(y, fg_ref[...])
    o_ref[...] = y


def _combine_ple(x, y, route, slots, g, wg, p, wp, fg, *, final, tm):
    T, D = x.shape
    P = p.shape[1]
    const = lambda shape: pl.BlockSpec(shape, lambda i, s: (0, 0))
    return pl.pallas_call(
        functools.partial(_combine_ple_body, final=final),
        grid_spec=pltpu.PrefetchScalarGridSpec(
            num_scalar_prefetch=1,
            grid=(T // tm,),
            in_specs=[pl.BlockSpec(memory_space=pl.ANY),
                      pl.BlockSpec((tm, D), lambda i, s: (i, 0)),
                      pl.BlockSpec((tm, LANES), lambda i, s: (i, 0)),
                      const((1, D)), const((D, D)),
                      pl.BlockSpec((tm, P), lambda i, s: (i, 0)),
                      const((P, D)), const((1, D))],
            out_specs=pl.BlockSpec((tm, D), lambda i, s: (i, 0)),
            scratch_shapes=[pltpu.VMEM((2, tm, D), F32), pltpu.VMEM((2, tm, D), F32),
                            pltpu.SemaphoreType.DMA((2,))]),
        out_shape=jax.ShapeDtypeStruct((T, D), F32),
        compiler_params=_cp(("arbitrary",), 48),
        name="combine_ple",
    )(slots, y, x, route, g.reshape(1, D), wg, p, wp, fg.reshape(1, D))


def _moe(x, g, w_router, wg, wu, wd):
    T, D = x.shape
    MB = MOE_BLOCK
    A = T * TOP_K
    route = _router(x, g, w_router, tm=512)
    flat_e = route[:, :TOP_K].astype(I32).reshape(A)
    onehot = (flat_e[:, None] == jnp.arange(N_EXPERTS, dtype=I32)[None, :]).astype(I32)
    csum = jnp.cumsum(onehot, axis=0)
    rank = jnp.take_along_axis(csum, flat_e[:, None], axis=1)[:, 0] - 1
    padded = (csum[-1] + MB - 1) // MB * MB
    pend = jnp.cumsum(padded)
    slots = (pend - padded)[flat_e] + rank
    n_blk = A // MB + N_EXPERTS
    tok = jnp.zeros((n_blk * MB,), I32).at[slots].set(jnp.arange(A, dtype=I32) // TOP_K)
    blk_start = jnp.arange(n_blk, dtype=I32) * MB
    blk_e = jnp.minimum(jnp.sum(blk_start[:, None] >= pend[None, :], axis=1), N_EXPERTS - 1).astype(I32)
    n_used = (pend[-1:] // MB).astype(I32)
    y = _experts(x, g, tok, blk_e, n_used, wg, wu, wd, tf=min(1024, wg.shape[2]))
    return y, route, slots.astype(I32)


def _ab_in_weight(w_in):
    w_in = w_in.astype(BF16)
    r, k, v, wl, al, gl, zq, zc, zqi, zki, zwi = jnp.split(
        w_in, np.cumsum([1024, 1024, 1024, 96, 96, 256, 1024, 256, 1024, 64]).tolist(), axis=1)
    D = w_in.shape[0]
    z = lambda n: jnp.zeros((D, n), w_in.dtype)
    cols = [r, k, v, zq, zqi, gl, zc, wl, z(32), al, z(32), zki, zki, zwi, z(AB_COLS - AB_ZWI - 16)]
    return jnp.concatenate(cols, axis=1)


def kernel(x, p, norm_mix, norm_ffn, ab_w_in, ab_mu, rwkv_w0, rwkv_w_up, rwkv_a0, rwkv_a_up, rwkv_g_up, rwkv_k_k, rwkv_k_a, rwkv_r_k, rwkv_gn_w, rwkv_gn_b, dsa_ckv_norm, dsa_w_uk, dsa_w_uv, ab_w_out, rel_bias, ffn_w_gate, ffn_w_up, ffn_w_down, gdn_w_in, gdn_conv, gdn_a_log, gdn_dt_bias, gdn_out_norm, gdn_w_out, moe_router, moe_w_gate, moe_w_up, moe_w_down, ple_norm, ple_w_gate, ple_w_proj, final_norm):
    B, L, D = x.shape
    T = B * L
    xf = x.reshape(T, D)
    bf = lambda w: w.astype(BF16)

    z = _norm_mm(xf, norm_mix[0], _ab_in_weight(ab_w_in[0]), tm=1024, tn=1024)
    y_a = _rwkv(z, B, L, ab_mu[0], rwkv_w0[0], rwkv_w_up[0], rwkv_a0[0], rwkv_a_up[0], rwkv_g_up[0],
                rwkv_k_k[0], rwkv_k_a[0], rwkv_r_k[0], rwkv_gn_w[0], rwkv_gn_b[0])
    y_b = _dsa(z, B, L, dsa_ckv_norm[0], dsa_w_uk[0], dsa_w_uv[0], rel_bias)
    xf = _mm_res([y_a, y_b], bf(ab_w_out[0]), xf, tm=1024, tn=1024)
    xf = _ffn(xf, norm_ffn[0], bf(ffn_w_gate[0]), bf(ffn_w_up[0]), bf(ffn_w_down[0]), tm=1024, tf=512)
    xf = _ple(xf, ple_norm[0], bf(ple_w_gate[0]), p[0].reshape(T, PLE_DIM), bf(ple_w_proj[0]),
              final_norm, final=False, tm=512)

    z = _norm_mm(xf, norm_mix[1], _gd_in_weight(gdn_w_in[0]), tm=1024, tn=1280)
    gates = _gdn_gates(z, B, L, gdn_a_log[0], gdn_dt_bias[0])
    o = _gdn(z, gates, B, L, gdn_conv[0], gdn_out_norm[0])
    xf = _mm_res([o], bf(gdn_w_out[0]), xf, tm=512, tn=1024)
    y, route, slots = _moe(xf, norm_ffn[1], moe_router[0], bf(moe_w_gate[0]), bf(moe_w_up[0]),
                           bf(moe_w_down[0]))
    xf = _combine_ple(xf, y, route, slots, ple_norm[1], bf(ple_w_gate[1]), p[1].reshape(T, PLE_DIM),
                      bf(ple_w_proj[1]), final_norm, final=True, tm=256)
    return xf.reshape(B, L, D)
```

```python
import functools
import math

import numpy as np
import jax
import jax.numpy as jnp
from jax import lax
from jax.experimental import pallas as pl
from jax.experimental.pallas import tpu as pltpu

F32 = jnp.float32
BF16 = jnp.bfloat16
I32 = jnp.int32
HI = lax.Precision.HIGHEST

EPS = 1e-6
LANES = 128
MIB = 1024 * 1024

RWKV_H, RWKV_N = 16, 64
RWKV_W = RWKV_H * RWKV_N
RWKV_LORA = 96
RWKV_GATE = 256
RWKV_GN_EPS = 6.4e-4
RWKV_CHUNK = 64
RWKV_TB = 256

DSA_H, DSA_D, DSA_C = 8, 128, 256
IDX_H, IDX_D = 16, 64
TOPK_MAX = 256
QB = 128
KEY_GROUP = 4
ATTEND_GROUP = 8
REL_BUCKETS, REL_MAX_DIST = 32, 128
NEG = -1e30
INT_MIN = -(2 ** 31)

GDN_QK_H, GDN_V_H, GDN_D = 16, 32, 128
GDN_KW = GDN_QK_H * GDN_D
GDN_VW = GDN_V_H * GDN_D
GDN_CHUNK = 64
GDN_TB = 128
GDN_GATES_TB = 256
GDN_CONV_W = 4
GDN_HEADS_PER_STEP = 16

N_EXPERTS, TOP_K = 8, 2
MOE_BLOCK = 512
PLE_DIM = 256

AB_R, AB_K, AB_V = 0, 1024, 2048
AB_ZQ, AB_ZQI = 3072, 4096
AB_GL, AB_ZC = 5120, 5376
AB_WL, AB_AL = 5632, 5760
AB_ZKI, AB_ZWI = 5888, 6016
AB_COLS = 6144
GD_Q, GD_K, GD_V, GD_ZG, GD_GATES = 0, 2048, 4096, 8192, 12288
GD_COLS = 12800


def _cp(sem, vmem_mib):
    return pltpu.CompilerParams(dimension_semantics=sem, vmem_limit_bytes=vmem_mib * MIB)


def _dot(a, b, precision=None):
    return jnp.dot(a, b, preferred_element_type=F32, precision=precision)


def _dot_nt(a, b, precision=None):
    return lax.dot_general(a, b, (((1,), (1,)), ((), ())), preferred_element_type=F32,
                           precision=precision)


def _split(x):
    hi = x.astype(BF16)
    return hi, (x - hi.astype(F32)).astype(BF16)


def _seg_dot(x, ones_bf):
    hi, lo = _split(x)
    return _dot(hi, ones_bf) + _dot(lo, ones_bf)


def _rms(x, g):
    ms = jnp.mean(x * x, axis=-1, keepdims=True)
    return x * lax.rsqrt(ms + EPS) * g


def _sigmoid(x):
    return 0.5 * jnp.tanh(0.5 * x) + 0.5


def _silu(x):
    return x * _sigmoid(x)


def _softplus(x):
    return jnp.maximum(x, 0.0) + jnp.log1p(jnp.exp(-jnp.abs(x)))


def _iota(shape, dim):
    return lax.broadcasted_iota(I32, shape, dim)


def _norm_mm_body(x_ref, g_ref, w_ref, o_ref, xn_ref):
    @pl.when(pl.program_id(1) == 0)
    def _():
        xn_ref[...] = _rms(x_ref[...], g_ref[...]).astype(BF16)

    o_ref[...] = _dot(xn_ref[...], w_ref[...]).astype(o_ref.dtype)


def _norm_mm(x, g, w, *, tm, tn):
    T, K = x.shape
    N = w.shape[1]
    return pl.pallas_call(
        _norm_mm_body,
        grid=(T // tm, N // tn),
        in_specs=[pl.BlockSpec((tm, K), lambda i, j: (i, 0)),
                  pl.BlockSpec((1, K), lambda i, j: (0, 0)),
                  pl.BlockSpec((K, tn), lambda i, j: (0, j))],
        out_specs=pl.BlockSpec((tm, tn), lambda i, j: (i, j)),
        out_shape=jax.ShapeDtypeStruct((T, N), F32),
        scratch_shapes=[pltpu.VMEM((tm, K), BF16)],
        compiler_params=_cp(("parallel", "arbitrary"), 48),
        name="norm_mm",
    )(x, g.reshape(1, K), w)


def _mm_res_body(*refs):
    *a_refs, w_ref, r_ref, o_ref = refs
    acc = r_ref[...]
    off = 0
    for a_ref in a_refs:
        k = a_ref.shape[1]
        acc = acc + _dot(a_ref[...], w_ref[off:off + k, :])
        off += k
    o_ref[...] = acc


def _mm_res(parts, w, res, *, tm, tn):
    T = res.shape[0]
    K, N = w.shape
    assert sum(a.shape[1] for a in parts) == K
    return pl.pallas_call(
        _mm_res_body,
        grid=(T // tm, N // tn),
        in_specs=[pl.BlockSpec((tm, a.shape[1]), lambda i, j: (i, 0)) for a in parts]
        + [pl.BlockSpec((K, tn), lambda i, j: (0, j)),
           pl.BlockSpec((tm, tn), lambda i, j: (i, j))],
        out_specs=pl.BlockSpec((tm, tn), lambda i, j: (i, j)),
        out_shape=jax.ShapeDtypeStruct((T, N), F32),
        compiler_params=_cp(("parallel", "arbitrary"), 48),
        name="mm_res",
    )(*parts, w, res)


def _ple_body(x_ref, g_ref, wg_ref, p_ref, wp_ref, fg_ref, o_ref, *, final):
    x = x_ref[...]
    gate = _sigmoid(_dot(_rms(x, g_ref[...]).astype(BF16), wg_ref[...]))
    y = x + gate * _dot(p_ref[...].astype(BF16), wp_ref[...])
    if final:
        y = _rms(y, fg_ref[...])
    o_ref[...] = y


def _ple(x, g, wg, p, wp, fg, *, final, tm):
    T, D = x.shape
    P = p.shape[1]
    return pl.pallas_call(
        functools.partial(_ple_body, final=final),
        grid=(T // tm,),
        in_specs=[pl.BlockSpec((tm, D), lambda i: (i, 0)),
                  pl.BlockSpec((1, D), lambda i: (0, 0)),
                  pl.BlockSpec((D, D), lambda i: (0, 0)),
                  pl.BlockSpec((tm, P), lambda i: (i, 0)),
                  pl.BlockSpec((P, D), lambda i: (0, 0)),
                  pl.BlockSpec((1, D), lambda i: (0, 0))],
        out_specs=pl.BlockSpec((tm, D), lambda i: (i, 0)),
        out_shape=jax.ShapeDtypeStruct((T, D), F32),
        compiler_params=_cp(("parallel",), 48),
        name="ple",
    )(x, g.reshape(1, D), wg, p, wp, fg.reshape(1, D))


def _ffn_body(x_ref, g_ref, wg_ref, wu_ref, wd_ref, o_ref, xn_ref):
    @pl.when(pl.program_id(1) == 0)
    def _():
        x = x_ref[...]
        xn_ref[...] = _rms(x, g_ref[...]).astype(BF16)
        o_ref[...] = x

    xn = xn_ref[...]
    h = (_silu(_dot(xn, wg_ref[...].astype(BF16))) * _dot(xn, wu_ref[...].astype(BF16))).astype(BF16)
    o_ref[...] += _dot(h, wd_ref[...].astype(BF16))


def _ffn(x, g, wg, wu, wd, *, tm, tf):
    T, D = x.shape
    Fh = wg.shape[1]
    return pl.pallas_call(
        _ffn_body,
        grid=(T // tm, Fh // tf),
        in_specs=[pl.BlockSpec((tm, D), lambda i, f: (i, 0)),
                  pl.BlockSpec((1, D), lambda i, f: (0, 0)),
                  pl.BlockSpec((D, tf), lambda i, f: (0, f)),
                  pl.BlockSpec((D, tf), lambda i, f: (0, f)),
                  pl.BlockSpec((tf, D), lambda i, f: (f, 0))],
        out_specs=pl.BlockSpec((tm, D), lambda i, f: (i, 0)),
        out_shape=jax.ShapeDtypeStruct((T, D), F32),
        scratch_shapes=[pltpu.VMEM((tm, D), BF16)],
        compiler_params=_cp(("parallel", "arbitrary"), 56),
        name="ffn",
    )(x, g.reshape(1, D), wg, wu, wd)


def _shift_mix(x, prev_row, mu):
    xs = pltpu.roll(x, 1, axis=0)
    xs = jnp.where(_iota(x.shape, 0) == 0, prev_row, xs)
    return x + (xs - x) * mu


def _bd(x, lo):
    return jnp.concatenate([jnp.where(lo, x, 0.0), jnp.where(lo, 0.0, x)], axis=0)


def _unit_lower_inverses(xs, eye, steps, sign):
    ps = [eye + x if sign > 0 else eye - x for x in xs]
    xbs = [x.astype(BF16) for x in xs]
    for _ in range(steps):
        xbs = [_dot(xb, xb).astype(BF16) for xb in xbs]
        ps = [p + _dot(p.astype(BF16), xb) for p, xb in zip(ps, xbs)]
    return ps


def _rwkv_body(rkv_ref, gl_ref, wa_ref, mu_rkv_ref, mu_gl_ref, mu_wa_ref, w0_ref, wup_ref,
               a0_ref, aup_ref, gup_ref, kk_ref, ka_ref, rk_ref, gnw_ref, gnb_ref, o_ref,
               prev_rkv, prev_gl, prev_wa, state, r_s, k_s, v_s, kn_s, a_s, lw_s, cg_s, g_s, y_s):
    NB, TB = rkv_ref.shape[0], rkv_ref.shape[1]
    C = RWKV_CHUNK
    NP = RWKV_W // LANES

    @pl.when(pl.program_id(0) == 0)
    def _():
        prev_rkv[...] = jnp.zeros_like(prev_rkv)
        prev_gl[...] = jnp.zeros_like(prev_gl)
        prev_wa[...] = jnp.zeros_like(prev_wa)
        state[...] = jnp.zeros_like(state)

    lane = _iota((1, LANES), 1)
    lo = lane < RWKV_N
    r128 = _iota((LANES, LANES), 0)
    c128 = _iota((LANES, LANES), 1)
    same = (r128 // C) == (c128 // C)
    seg_ones = jnp.where(same, 1.0, 0.0).astype(BF16)
    eye = jnp.where(r128 == c128, 1.0, 0.0).astype(F32)
    strict = same & (c128 < r128)
    incl = same & (c128 <= r128)
    rt = _iota((TB, TB), 0)
    ct = _iota((TB, TB), 1)
    tri_chunks = jnp.where(((rt // C) == (ct // C)) & (ct <= rt), 1.0, 0.0).astype(BF16)

    for bi in range(NB):
        rb = slice(bi * TB, (bi + 1) * TB)
        pb = slice(bi, bi + 1)
        wa_raw = wa_ref[bi]
        wa = _shift_mix(wa_raw, prev_wa[pb, :], mu_wa_ref[...])
        prev_wa[pb, :] = wa_raw[TB - 1:TB, :]
        wl = jnp.tanh(wa[:, :LANES]).astype(BF16)
        al = wa[:, LANES:].astype(BF16)
        w = -_softplus(-(w0_ref[...] + _dot(wl, wup_ref[...]))) - 0.5
        lw = -jnp.exp(w)
        lw_s[rb, :] = lw
        lw_hi, lw_lo = _split(lw)
        cg_s[rb, :] = _dot(tri_chunks, lw_hi) + _dot(tri_chunks, lw_lo)
        a_s[rb, :] = _sigmoid(a0_ref[...] + _dot(al, aup_ref[...]))
        gl_raw = gl_ref[bi]
        gl = _shift_mix(gl_raw, prev_gl[pb, :], mu_gl_ref[...])
        prev_gl[pb, :] = gl_raw[TB - 1:TB, :]
        g_s[rb, :] = _dot(_sigmoid(gl).astype(BF16), gup_ref[...])

        for p in range(NP):
            cs = slice(p * LANES, (p + 1) * LANES)
            cols = [slice(off + p * LANES, off + (p + 1) * LANES) for off in (AB_R, AB_K, AB_V)]
            mixed = []
            for c in cols:
                raw = rkv_ref[bi, :, c]
                mixed.append(_shift_mix(raw, prev_rkv[pb, c], mu_rkv_ref[:, c]))
                prev_rkv[pb, c] = raw[TB - 1:TB, :]
            r, k, v = mixed
            kk = k * kk_ref[:, cs]
            kn_s[rb, cs] = kk * lax.rsqrt(_seg_dot(kk * kk, seg_ones) + 1e-6)
            r_s[rb, cs] = r
            k_s[rb, cs] = k * (1.0 + (a_s[rb, cs] - 1.0) * ka_ref[:, cs])
            v_s[rb, cs] = v

    def chunk(c, carry):
        pairs = range(NB * NP)
        rws = [pl.ds(pl.multiple_of((q // NP) * TB + c * C, C), C) for q in pairs]
        css = [slice((q % NP) * LANES, (q % NP + 1) * LANES) for q in pairs]
        gam_last, ar, bk, bk_end, v_f, v_t = [], [], [], [], [], []
        for rows, cs in zip(rws, css):
            cg = cg_s[rows, cs]
            gam = jnp.exp(cg)
            gam_inv = jnp.exp(-cg)
            gam_prev = jnp.exp(cg - lw_s[rows, cs])
            gl_ = gam[C - 1:C, :]
            kn = kn_s[rows, cs]
            b_raw = kn * a_s[rows, cs] * gam_inv
            k_raw = k_s[rows, cs] * gam_inv
            gam_last.append(gl_)
            ar.append(jnp.concatenate([_bd(-kn * gam_prev, lo), _bd(r_s[rows, cs] * gam, lo)],
                                      axis=0).astype(BF16))
            bk.append(jnp.concatenate([_bd(b_raw, lo), _bd(k_raw, lo)], axis=0).astype(BF16))
            bk_end.append(jnp.concatenate([_bd(b_raw * gl_, lo), _bd(k_raw * gl_, lo)],
                                          axis=0).astype(BF16))
            vf = _bd(v_s[rows, cs], lo)
            v_f.append(vf)
            v_t.append(vf.astype(BF16))
        score = [_dot_nt(ar[p], bk[p]) for p in pairs]
        a_ab = [jnp.where(strict, s[:LANES, :LANES], 0.0) for s in score]
        a_akv = [_dot(jnp.where(strict, score[p][:LANES, LANES:], 0.0).astype(BF16), v_t[p])
                 for p in pairs]
        r_abk = [jnp.concatenate([jnp.where(incl, s[LANES:, :LANES], 0.0),
                                  jnp.where(incl, s[LANES:, LANES:], 0.0)], axis=1).astype(BF16)
                 for s in score]
        t_inv = [t.astype(BF16) for t in _unit_lower_inverses(a_ab, eye, 5, 1)]
        s_old = [state[p] for p in pairs]
        sproj = [_dot_nt(ar[p], s_old[p].astype(BF16)) for p in pairs]
        u = [_dot(t_inv[p], (sproj[p][:LANES] + a_akv[p]).astype(BF16)) for p in pairs]
        y_bd = [sproj[p][LANES:] + _dot(r_abk[p], jnp.concatenate([u[p].astype(BF16), v_t[p]], axis=0))
                for p in pairs]
        for p in pairs:
            y_s[rws[p], css[p]] = y_bd[p][:C] + y_bd[p][C:]
        uvT = [jnp.concatenate([u[p].T, v_f[p].T], axis=1).astype(BF16) for p in pairs]
        for p in pairs:
            state[p] = s_old[p] * gam_last[p] + _dot(uvT[p], bk_end[p])
        return carry

    lax.fori_loop(0, TB // C, chunk, 0)

    for bi in range(NB):
        rb = slice(bi * TB, (bi + 1) * TB)
        for p in range(NP):
            cs = slice(p * LANES, (p + 1) * LANES)
            y = y_s[rb, cs]
            mean = _seg_dot(y, seg_ones) * (1.0 / RWKV_N)
            d = y - mean
            var = _seg_dot(d * d, seg_ones) * (1.0 / RWKV_N)
            yn = d * lax.rsqrt(var + RWKV_GN_EPS) * gnw_ref[:, cs] + gnb_ref[:, cs]
            bonus = _seg_dot(r_s[rb, cs] * k_s[rb, cs] * rk_ref[:, cs], seg_ones) * v_s[rb, cs]
            o_ref[bi, :, cs] = ((yn + bonus) * g_s[rb, cs]).astype(o_ref.dtype)


def _rwkv(z, B, L, mu, w0, w_up, a0, a_up, g_up, k_k, k_a, r_k, gn_w, gn_b):
    TB = RWKV_TB
    nb = L // TB
    W = RWKV_W
    mu_r, mu_k, mu_v, mu_wl, mu_al, mu_gl = jnp.split(
        mu, np.cumsum([W, W, W, RWKV_LORA, RWKV_LORA])[:].tolist())
    pad = LANES - RWKV_LORA
    mu_rkv = jnp.concatenate([mu_r, mu_k, mu_v]).reshape(1, 3 * W)
    mu_wa = jnp.concatenate([jnp.pad(mu_wl, (0, pad)), jnp.pad(mu_al, (0, pad))]).reshape(1, 2 * LANES)
    wup = jnp.pad(w_up, ((0, pad), (0, 0))).astype(BF16)
    aup = jnp.pad(a_up, ((0, pad), (0, 0))).astype(BF16)
    row = lambda t: t.reshape(1, W)
    vec = lambda n: pl.BlockSpec((1, n), lambda i: (0, 0))
    big = lambda: pltpu.VMEM((B * TB, W), F32)
    z3 = z.reshape(B, L, AB_COLS)
    out = pl.pallas_call(
        _rwkv_body,
        grid=(nb,),
        in_specs=[pl.BlockSpec((B, TB, 3 * W), lambda i: (0, i, 0)),
                  pl.BlockSpec((B, TB, RWKV_GATE), lambda i: (0, i, AB_GL // RWKV_GATE)),
                  pl.BlockSpec((B, TB, 2 * LANES), lambda i: (0, i, AB_WL // (2 * LANES))),
                  vec(3 * W), vec(RWKV_GATE), vec(2 * LANES), vec(W),
                  pl.BlockSpec((LANES, W), lambda i: (0, 0)),
                  vec(W),
                  pl.BlockSpec((LANES, W), lambda i: (0, 0)),
                  pl.BlockSpec((RWKV_GATE, W), lambda i: (0, 0)),
                  vec(W), vec(W), vec(W), vec(W), vec(W)],
        out_specs=pl.BlockSpec((B, TB, W), lambda i: (0, i, 0)),
        out_shape=jax.ShapeDtypeStruct((B, L, W), BF16),
        scratch_shapes=[pltpu.VMEM((B, 3 * W), F32), pltpu.VMEM((B, RWKV_GATE), F32),
                        pltpu.VMEM((B, 2 * LANES), F32),
                        pltpu.VMEM((B * W // LANES, LANES, LANES), F32),
                        big(), big(), big(), big(), big(), big(), big(), big(), big()],
        compiler_params=_cp(("arbitrary",), 56),
        name="rwkv7",
    )(z3, z3, z3, mu_rkv, mu_gl.reshape(1, RWKV_GATE), mu_wa, row(w0), wup, row(a0), aup,
      g_up.astype(BF16), row(k_k), row(k_a), row(r_k), row(gn_w), row(gn_b))
    return out.reshape(B * L, W)


def _t5_bucket_np(dist):
    n = np.maximum(dist, 0)
    exact = REL_BUCKETS // 2
    ratio = np.log(np.maximum(n, 1).astype(np.float32) / np.float32(exact)) / np.float32(
        math.log(REL_MAX_DIST / exact))
    large = exact + (ratio.astype(np.float32) * np.float32(REL_BUCKETS - exact)).astype(np.int32)
    return np.where(n < exact, n, np.minimum(large, REL_BUCKETS - 1)).astype(np.int32)


def _near_buckets():
    kl = np.arange(QB)[:, None]
    ql = np.arange(QB)[None, :]
    return np.stack([_t5_bucket_np(ql - kl), _t5_bucket_np(QB + ql - kl)])


def _dsa_body(tbl_ref, bkt_ref, zq_ref, zqi_ref, zc_ref, zki_ref, zwi_ref, cn_ref, wukT_ref,
              wuvT_ref, o_ref, c_all, cT_all, kibd_all, sc, qiT, qlatT, bias, m_s, l_s, alpha_s,
              oT, *, topk):
    b = pl.program_id(0)
    qb = pl.program_id(1)
    lo = _iota((1, LANES), 1) < IDX_D
    krow = _iota((QB, QB), 0)
    qcol = _iota((QB, QB), 1)
    hsl = [slice(h * QB, (h + 1) * QB) for h in range(DSA_H)]

    @pl.when((b == 0) & (qb == 0))
    def _():
        for t in range(2):
            bk = bkt_ref[t]
            for h in range(DSA_H):
                far = tbl_ref[REL_BUCKETS - 1, h]
                acc = jnp.zeros((QB, QB), F32)
                for bb in range(REL_BUCKETS - 1):
                    acc = jnp.where(bk == bb, tbl_ref[bb, h] - far, acc)
                bias[t, h] = acc

    c_new = _rms(zc_ref[...], cn_ref[...])
    c_all[qb] = c_new.astype(BF16)
    cT_all[qb] = c_new.T.astype(BF16)
    kibd_all[qb] = _bd(zki_ref[...], lo).astype(BF16)

    for p in range(IDX_H // 2):
        qiT[:, p * QB:(p + 1) * QB] = zqi_ref[:, p * LANES:(p + 1) * LANES].T.astype(BF16)
    wT = zwi_ref[...].T * (IDX_D ** -0.5 * IDX_H ** -0.5)
    w_rows = [wT[h:h + 1, :] for h in range(IDX_H)]
    for h in range(DSA_H):
        qhT = zq_ref[:, hsl[h]].T.astype(BF16)
        qlatT[:, hsl[h]] = (_dot(wukT_ref[h], qhT) * DSA_D ** -0.5).astype(BF16)

    def score_blocks(j, nk):
        kb = kibd_all[pl.ds(j, nk)].reshape(nk * 2 * QB, LANES)
        acc = [jnp.zeros((QB, QB), F32) for _ in range(nk)]
        for pp in range(IDX_H // 4):
            s = _dot(kb, qiT[:, 2 * pp * QB:2 * (pp + 1) * QB])
            for k in range(nk):
                even = s[2 * k * QB:(2 * k + 1) * QB]
                odd = s[(2 * k + 1) * QB:(2 * k + 2) * QB]
                acc[k] = (acc[k] + w_rows[4 * pp] * jnp.maximum(even[:, :QB], 0.0)
                          + w_rows[4 * pp + 1] * jnp.maximum(odd[:, :QB], 0.0)
                          + w_rows[4 * pp + 2] * jnp.maximum(even[:, QB:], 0.0)
                          + w_rows[4 * pp + 3] * jnp.maximum(odd[:, QB:], 0.0))
        for k in range(nk):
            bits = lax.bitcast_convert_type(acc[k], I32)
            key = bits ^ ((bits >> 31) & 0x7FFFFFFF)
            sc[j + k] = jnp.where((j + k == qb) & (krow > qcol), INT_MIN, key)

    def score_group(jj, carry):
        score_blocks(jj * KEY_GROUP, KEY_GROUP)
        return carry

    def score_single(j, carry):
        score_blocks(j, 1)
        return carry

    n_sgroup = (qb + 1) // KEY_GROUP
    lax.fori_loop(0, n_sgroup, score_group, 0)
    lax.fori_loop(n_sgroup * KEY_GROUP, qb + 1, score_single, 0)

    n_quad = (qb + 1) // KEY_GROUP

    def count(pred):
        def quad(jj, a):
            blk = sc[pl.ds(jj * KEY_GROUP, KEY_GROUP)]
            for k in range(KEY_GROUP):
                a = a + jnp.where(pred(blk[k]), 1, 0)
            return a

        def single(j, a):
            return a + jnp.where(pred(sc[j]), 1, 0)

        a = lax.fori_loop(0, n_quad, quad, jnp.zeros((QB, QB), I32))
        a = lax.fori_loop(n_quad * KEY_GROUP, qb + 1, single, a)
        return jnp.sum(a, axis=0, keepdims=True)

    def bis_body(i, carry):
        t, n_t = carry
        cand = t ^ jnp.left_shift(jnp.int32(1), 31 - i)
        tot = count(lambda key: key >= cand)
        ok = tot >= topk
        return jnp.where(ok, cand, t), jnp.where(ok, tot, n_t)

    thr_raw, n_ge = lax.fori_loop(
        0, 32, bis_body,
        (jnp.full((1, QB), INT_MIN, I32), jnp.broadcast_to((qb + 1) * QB, (1, QB)).astype(I32)))
    thr = jnp.maximum(thr_raw, INT_MIN + 1)

    @pl.when(jnp.max(n_ge) > topk)
    def _():
        keep = (topk - count(lambda key: key > thr_raw)).astype(F32)
        tri = jnp.where(qcol <= krow, 1.0, 0.0).astype(BF16)

        def drop(j, seen):
            key = sc[j]
            tied = key == thr_raw
            tied_bf = jnp.where(tied, 1.0, 0.0).astype(BF16)
            rank = seen + _dot(tri, tied_bf)
            sc[j] = jnp.where(tied & (rank > keep), INT_MIN, key)
            return seen + jnp.sum(tied_bf.astype(F32), axis=0, keepdims=True)

        lax.fori_loop(0, qb + 1, drop, jnp.zeros((1, QB), F32))

    m_s[...] = jnp.full(m_s.shape, NEG, F32)
    l_s[...] = jnp.zeros_like(l_s)
    oT[...] = jnp.zeros_like(oT)

    def attend(j, nk, near):
        sel = sc[pl.ds(j, nk)].reshape(nk * QB, QB) >= thr
        lg = _dot(c_all[pl.ds(j, nk)].reshape(nk * QB, DSA_C), qlatT[...])
        cT = jnp.concatenate([cT_all[j + k] for k in range(nk)], axis=1)
        prs = []
        for h in range(DSA_H):
            lgh = lg[:, hsl[h]]
            if near is not None:
                lgh = lgh + bias[near, h]
            lgh = jnp.where(sel, lgh, NEG)
            m_old = m_s[:, hsl[h]]
            m_new = jnp.maximum(m_old, jnp.max(lgh, axis=0, keepdims=True))
            pr = jnp.exp(lgh - m_new)
            alpha_s[:, hsl[h]] = jnp.exp(m_old - m_new)
            l_s[:, hsl[h]] = alpha_s[:, hsl[h]] * l_s[:, hsl[h]] + jnp.sum(pr, axis=0, keepdims=True)
            m_s[:, hsl[h]] = m_new
            prs.append(pr.astype(BF16))
        oT[...] = alpha_s[...] * oT[...] + _dot(cT, jnp.concatenate(prs, axis=1))

    def far_group(jj, carry):
        attend(jj * ATTEND_GROUP, ATTEND_GROUP, None)
        return carry

    def far_single(j, carry):
        attend(j, 1, None)
        return carry

    n_far = jnp.maximum(qb - 1, 0)
    n_group = n_far // ATTEND_GROUP
    lax.fori_loop(0, n_group, far_group, 0)
    done = n_group * ATTEND_GROUP
    half = (n_far - done) // (ATTEND_GROUP // 2)

    @pl.when(half == 1)
    def _():
        attend(done, ATTEND_GROUP // 2, None)

    lax.fori_loop(done + half * (ATTEND_GROUP // 2), n_far, far_single, 0)

    @pl.when(qb >= 1)
    def _():
        attend(qb - 1, 1, 1)

    attend(qb, 1, 0)

    inv_l = 1.0 / l_s[...]
    for h in range(DSA_H):
        oh = (oT[:, hsl[h]] * inv_l[:, hsl[h]]).astype(BF16)
        o_ref[:, hsl[h]] = _dot(wuvT_ref[h], oh).T.astype(o_ref.dtype)


def _dsa(z, B, L, ckv_norm, w_uk, w_uv, rel_bias):
    nq = L // QB
    topk = min(TOPK_MAX, L // 4)
    wukT = jnp.swapaxes(w_uk, 1, 2).astype(BF16)
    wuvT = jnp.swapaxes(w_uv, 1, 2).astype(BF16)
    blk = lambda w, off: pl.BlockSpec((QB, w), lambda b, q: (b * nq + q, off // w))
    full = lambda shape: pl.BlockSpec(shape, lambda b, q: (0,) * len(shape))
    W = DSA_H * DSA_D
    return pl.pallas_call(
        functools.partial(_dsa_body, topk=topk),
        grid=(B, nq),
        in_specs=[pl.BlockSpec(memory_space=pltpu.SMEM),
                  full((2, QB, QB)),
                  blk(W, AB_ZQ), blk(IDX_H * IDX_D, AB_ZQI), blk(DSA_C, AB_ZC),
                  blk(LANES, AB_ZKI), blk(LANES, AB_ZWI),
                  full((1, DSA_C)), full((DSA_H, DSA_C, DSA_D)), full((DSA_H, DSA_D, DSA_C))],
        out_specs=pl.BlockSpec((QB, W), lambda b, q: (b * nq + q, 0)),
        out_shape=jax.ShapeDtypeStruct((B * L, W), BF16),
        scratch_shapes=[pltpu.VMEM((nq, QB, DSA_C), BF16), pltpu.VMEM((nq, DSA_C, QB), BF16),
                        pltpu.VMEM((nq, 2 * QB, LANES), BF16), pltpu.VMEM((nq, QB, QB), I32),
                        pltpu.VMEM((LANES, IDX_H // 2 * QB), BF16), pltpu.VMEM((DSA_C, W), BF16),
                        pltpu.VMEM((2, DSA_H, QB, QB), F32),
                        pltpu.VMEM((1, W), F32), pltpu.VMEM((1, W), F32), pltpu.VMEM((1, W), F32),
                        pltpu.VMEM((DSA_C, W), F32)],
        compiler_params=_cp(("arbitrary", "arbitrary"), 48),
        name="dsa",
    )(rel_bias, jnp.asarray(_near_buckets()), z, z, z, z, z, ckv_norm.reshape(1, DSA_C), wukT, wuvT)


def _gdn_gates_body(z_ref, alog_ref, dtb_ref, o_ref):
    TB = z_ref.shape[0]
    C = GDN_CHUNK
    z = z_ref[...]
    sub = _iota((1, LANES), 1) % 8
    beta = _sigmoid(z)
    g = -jnp.exp(alog_ref[...]) * _softplus(z + dtb_ref[...])
    rt = _iota((TB, TB), 0)
    ct = _iota((TB, TB), 1)
    same = (rt // C) == (ct // C)
    g_hi, g_rest = _split(g)
    g_mid, g_lo = _split(g - g_hi.astype(F32))
    tri = jnp.where(same & (ct <= rt), 1.0, 0.0).astype(BF16)
    blk = jnp.where(same, 1.0, 0.0).astype(BF16)
    cum = _dot(tri, g_hi) + _dot(tri, g_mid) + _dot(tri, g_lo)
    tot = _dot(blk, g_hi) + _dot(blk, g_mid) + _dot(blk, g_lo)
    tile =jnp.where(sub < 2, beta, jnp.where(sub < 4, g, jnp.where(sub < 6, cum, tot)))
    o_ref[...] = tile.T


def _gdn_gates(z, B, L, a_log, dt_bias):
    TB = GDN_GATES_TB
    nb = L // TB
    spread =lambda t: jnp.zeros((GDN_QK_H, 8), F32).at[:, 2:].set(
        jnp.tile(t.reshape(GDN_QK_H, 2), (1, 3))).reshape(1, LANES)
    return pl.pallas_call(
        _gdn_gates_body,
        grid=(B, nb),
        in_specs=[pl.BlockSpec((TB, LANES), lambda b, i: (b * nb + i, GD_GATES // LANES)),
                  pl.BlockSpec((1, LANES), lambda b, i: (0, 0)),
                  pl.BlockSpec((1, LANES), lambda b, i: (0, 0))],
        out_specs=pl.BlockSpec((None, LANES, TB), lambda b, i: (b, 0, i)),
        out_shape=jax.ShapeDtypeStruct((B, LANES, L), F32),
        compiler_params=_cp(("parallel", "parallel"), 32),
        name="gdn_gates",
    )(z, spread(a_log), spread(dt_bias))


def _gdn_body(zq_ref, zk_ref, zv_ref, zg_ref, gates_ref, cwq_ref, cwk_ref, cwv_ref, on_ref, o_ref,
              xbuf, state, q_s, k_s, v_s, *, hg):
    TB = zq_ref.shape[0]
    C = GDN_CHUNK
    D = GDN_D
    KW = GDN_CONV_W
    W = 4 * D * hg

    @pl.when(pl.program_id(2) == 0)
    def _():
        xbuf[0:8, :] = jnp.zeros((8, W), F32)
        state[...] = jnp.zeros_like(state)

    xbuf[8:TB + 8, 0:D * hg] = zq_ref[...]
    xbuf[8:TB + 8, D * hg:2 * D * hg] = zk_ref[...]
    xbuf[8:TB + 8, 2 * D * hg:W] = zv_ref[...]
    cw = jnp.concatenate([cwq_ref[...], cwk_ref[...], cwv_ref[...]], axis=1)
    l2 = lambda t: t * lax.rsqrt(jnp.sum(t * t, axis=-1, keepdims=True) + 1e-6)
    for g in range(W // D):
        cols = slice(g * D, (g + 1) * D)
        acc = jnp.zeros((TB, D), F32)
        for j in range(KW):
            acc = acc + cw[j:j + 1, cols] * xbuf[8 - (KW - 1) + j:8 - (KW - 1) + j + TB, cols]
        act = _silu(acc)
        if g < hg:
            q_s[:, cols] = l2(act) * D ** -0.5
        elif g < 2 * hg:
            k_s[:, (g - hg) * D:(g - hg + 1) * D] = l2(act)
        else:
            v_s[:, (g - 2 * hg) * D:(g - 2 * hg + 1) * D] = act
    xbuf[0:8, :] = xbuf[TB:TB + 8, :]

    lane = _iota((1, LANES), 1)
    lo = lane < C
    r128 = _iota((LANES, LANES), 0)
    c128 = _iota((LANES, LANES), 1)
    same = (r128 // C) == (c128 // C)
    eye_m = r128 == c128
    eye = jnp.where(eye_m, 1.0, 0.0).astype(F32)
    strict = same & (c128 < r128)
    incl = same & (c128 <= r128)

    heads = range(hg)
    for c in range(TB // C):
        rows = slice(c * C, (c + 1) * C)
        e_st, e_neg, fb_st, etot_cat, kc, qc, dec = [], [], [], [], [], [], []
        for h in heads:
            win = gates_ref[8 * h:8 * h + 8, (c // 2) * LANES:(c // 2 + 1) * LANES]
            win_sw = pltpu.roll(win, C, axis=1)
            first, second = (win, win_sw) if c % 2 == 0 else (win_sw, win)
            st = lambda r: jnp.where(lo, first[r:r + 1, :], second[r + 1:r + 2, :])
            beta, cum, tot = st(0), st(4), st(6)
            etot = jnp.exp(tot)
            etot_sw = pltpu.roll(etot, C, axis=1)
            e_st.append(jnp.exp(cum))
            e_neg.append(-jnp.exp(cum))
            fb_st.append(jnp.exp(tot - cum) * beta)
            etot_cat.append(jnp.concatenate([jnp.where(lo, etot, etot_sw),
                                             jnp.where(lo, etot_sw, etot)], axis=1))
            kc.append(k_s[rows, h * D:(h + 1) * D])
            qc.append(q_s[rows, h * D:(h + 1) * D])
            cum_b = jnp.broadcast_to(cum, (LANES, LANES))
            dec.append(jnp.exp(jnp.where(incl, cum_b.T - cum_b, NEG)) * beta)
        score = [_dot_nt(jnp.concatenate([kc[h], kc[h], qc[h], qc[h]], axis=0).astype(BF16),
                         jnp.concatenate([kc[h], kc[h]], axis=0).astype(BF16))
                 for h in heads]
        t_inv = _unit_lower_inverses(
            [jnp.where(strict, score[h][:LANES] * dec[h], 0.0) for h in heads], eye, 5, -1)
        t_cat = [jnp.concatenate([t_inv[h], t_inv[h] * e_neg[h]], axis=1).astype(BF16) for h in heads]
        a_cat = [jnp.concatenate([eye * e_st[h], score[h][LANES:] * dec[h]], axis=1).astype(BF16)
                 for h in heads]
        s_old = [state[h] for h in heads]
        proj = [_dot(jnp.concatenate([kc[h], qc[h]], axis=0).astype(BF16), s_old[h].astype(BF16))
                for h in heads]
        rhs = [jnp.concatenate([v_s[rows, 2 * h * D:(2 * h + 1) * D],
                                v_s[rows, (2 * h + 1) * D:(2 * h + 2) * D],
                                proj[h][:C, :D], proj[h][:C, D:]], axis=0).astype(BF16) for h in heads]
        vn = [_dot(t_cat[h], rhs[h]) for h in heads]
        vn_bf = [x.astype(BF16) for x in vn]
        o_st = [_dot(a_cat[h], jnp.concatenate(
            [proj[h][C:, :D].astype(BF16), proj[h][C:, D:].astype(BF16), vn_bf[h]], axis=0))
            for h in heads]
        fv = [_dot((eye * fb_st[h]).astype(BF16), vn_bf[h]) for h in heads]
        for h in heads:
            state[h] = s_old[h] * etot_cat[h] + _dot(
                kc[h].T.astype(BF16), jnp.concatenate([fv[h][:C], fv[h][C:]], axis=1).astype(BF16))
        for h in heads:
            for u in range(2):
                cols = slice((2 * h + u) * D, (2 * h + u + 1) * D)
                oh = o_st[h][u * C:(u + 1) * C]
                o_ref[rows, cols] = (_rms(oh, on_ref[...]) * _silu(zg_ref[rows, cols])).astype(o_ref.dtype)


def _gdn(z, gates, B, L, conv_w, out_norm):
    TB = GDN_TB
    nb = L // TB
    D = GDN_D
    hg = GDN_HEADS_PER_STEP
    zspec = lambda w, off: pl.BlockSpec((TB, w), lambda b, h, i: (b * nb + i, off // w + h))
    cspec = lambda w, off: pl.BlockSpec((GDN_CONV_W, w), lambda b, h, i: (0, off // w + h))
    return pl.pallas_call(
        functools.partial(_gdn_body, hg=hg),
        grid=(B, GDN_QK_H // hg, nb),
        in_specs=[zspec(D * hg, GD_Q), zspec(D * hg, GD_K), zspec(2 * D * hg, GD_V),
                  zspec(2 * D * hg, GD_ZG),
                  pl.BlockSpec((None, 8 * hg, TB), lambda b, h, i: (b, h, i)),
                  cspec(D * hg, GD_Q), cspec(D * hg, GD_K), cspec(2 * D * hg, GD_V),
                  pl.BlockSpec((1, D), lambda b, h, i: (0, 0))],
        out_specs=pl.BlockSpec((TB, 2 * D * hg), lambda b, h, i: (b * nb + i, h)),
        out_shape=jax.ShapeDtypeStruct((B * L, GDN_VW), BF16),
        scratch_shapes=[pltpu.VMEM((TB + 8, 4 * D * hg), F32), pltpu.VMEM((hg, D, 2 * D), F32),
                        pltpu.VMEM((TB, D * hg), F32), pltpu.VMEM((TB, D * hg), F32),
                        pltpu.VMEM((TB, 2 * D * hg), F32)],
        compiler_params=_cp(("parallel", "parallel", "arbitrary"), 40),
        name="gdn",
    )(z, z, z, z, gates, conv_w, conv_w, conv_w, out_norm.reshape(1, D))


def _gd_in_weight(w_in):
    w_in = w_in.astype(BF16)
    qkv, zg, b, a = jnp.split(w_in, np.cumsum([2 * GDN_KW + GDN_VW, GDN_VW, GDN_V_H]).tolist(), axis=1)
    D = w_in.shape[0]
    pair = lambda t: t.reshape(D, GDN_QK_H, 2)
    gates = jnp.concatenate([pair(b), pair(a), pair(a), pair(a)], axis=2).reshape(D, LANES)
    pad = jnp.zeros((D, GD_COLS - GD_GATES - LANES), w_in.dtype)
    return jnp.concatenate([qkv, zg, gates, pad], axis=1)


def _router_body(x_ref, g_ref, wr_ref, o_ref):
    x_hi, x_lo = _split(_rms(x_ref[...], g_ref[...]))
    w_hi, w_lo = _split(wr_ref[...])
    logits = _dot(x_hi, w_hi) + _dot(x_lo, w_hi) + _dot(x_hi, w_lo)
    lane = _iota(logits.shape, 1)
    logits = jnp.where(lane < N_EXPERTS, logits, -jnp.inf)
    m1 = jnp.max(logits, axis=-1, keepdims=True)
    i1 = jnp.min(jnp.where(logits == m1, lane, LANES), axis=-1, keepdims=True)
    rest = jnp.where(lane == i1, -jnp.inf, logits)
    m2 = jnp.max(rest, axis=-1, keepdims=True)
    i2 = jnp.min(jnp.where(rest == m2, lane, LANES), axis=-1, keepdims=True)
    e = jnp.exp(m2 - m1)
    w1 = 1.0 / (1.0 + e)
    o_ref[...] = jnp.where(lane == 0, i1.astype(F32),
                           jnp.where(lane == 1, i2.astype(F32),
                                     jnp.where(lane == 2, w1, jnp.where(lane == 3, e * w1, 0.0))))


def _router(x, g, w_router, *, tm):
    T, D = x.shape
    wr = jnp.pad(w_router, ((0, 0), (0, LANES - N_EXPERTS)))
    return pl.pallas_call(
        _router_body,
        grid=(T // tm,),
        in_specs=[pl.BlockSpec((tm, D), lambda i: (i, 0)),
                  pl.BlockSpec((1, D), lambda i: (0, 0)),
                  pl.BlockSpec((D, LANES), lambda i: (0, 0))],
        out_specs=pl.BlockSpec((tm, LANES), lambda i: (i, 0)),
        out_shape=jax.ShapeDtypeStruct((T, LANES), F32),
        compiler_params=_cp(("parallel",), 32),
        name="router",
    )(x, g.reshape(1, D), wr)


def _row_copy(src_hbm, row, dst, r, sem):
    return pltpu.make_async_copy(src_hbm.at[pl.ds(row, 1), :], dst.at[pl.ds(r, 1), :], sem)


def _experts_body(tok_ref, be_ref, nu_ref, x_hbm, g_ref, wg_ref, wu_ref, wd_ref, o_ref,
                  xbuf, xn_ref, sem):
    i = pl.program_id(0)
    f = pl.program_id(1)
    MB = xbuf.shape[1]
    active = i < nu_ref[0]
    slot = i % 2

    def gather(blk, s):
        def issue(r, c):
            _row_copy(x_hbm, tok_ref[blk * MB + r], xbuf.at[s], r, sem.at[s]).start()
            return c

        lax.fori_loop(0, MB, issue, 0, unroll=8)

    @pl.when(f == 0)
    def _():
        o_ref[...] = jnp.zeros_like(o_ref)

    @pl.when((f == 0) & (i == 0) & active)
    def _():
        gather(0, 0)

    @pl.when((f == 0) & active)
    def _():
        pltpu.make_async_copy(x_hbm.at[pl.ds(0, MB), :], xbuf.at[slot], sem.at[slot]).wait()
        xn_ref[...] = _rms(xbuf[slot], g_ref[...]).astype(BF16)

    @pl.when((f == 1) & (i + 1 < nu_ref[0]))
    def _():
        gather(i + 1, 1 - slot)

    @pl.when(active)
    def _():
        xn = xn_ref[...]
        h = (_silu(_dot(xn, wg_ref[...])) * _dot(xn, wu_ref[...])).astype(BF16)
        o_ref[...] += _dot(h, wd_ref[...])


def _experts(x, g, tok, blk_e, n_used, wg, wu, wd, *, tf):
    T, D = x.shape
    MB = MOE_BLOCK
    n_blk = tok.shape[0] // MB
    Fh = wg.shape[2]
    assert Fh // tf >= 2, "the next block's rows are requested during hidden chunk 1"
    fe = lambda i, f, nu: jnp.where(i < nu[0], f, 0)
    return pl.pallas_call(
        _experts_body,
        grid_spec=pltpu.PrefetchScalarGridSpec(
            num_scalar_prefetch=3,
            grid=(n_blk, Fh // tf),
            in_specs=[pl.BlockSpec(memory_space=pl.ANY),
                      pl.BlockSpec((1, D), lambda i, f, tk, be, nu: (0, 0)),
                      pl.BlockSpec((None, D, tf), lambda i, f, tk, be, nu: (be[i], 0, fe(i, f, nu))),
                      pl.BlockSpec((None, D, tf), lambda i, f, tk, be, nu: (be[i], 0, fe(i, f, nu))),
                      pl.BlockSpec((None, tf, D), lambda i, f, tk, be, nu: (be[i], fe(i, f, nu), 0))],
            out_specs=pl.BlockSpec((MB, D), lambda i, f, tk, be, nu: (i, 0)),
            scratch_shapes=[pltpu.VMEM((2, MB, D), F32), pltpu.VMEM((MB, D), BF16),
                            pltpu.SemaphoreType.DMA((2,))]),
        out_shape=jax.ShapeDtypeStruct((n_blk * MB, D), F32),
        compiler_params=_cp(("arbitrary", "arbitrary"), 52),
        name="experts",
    )(tok, blk_e, n_used, x, g.reshape(1, D), wg, wu, wd)


def _combine_ple_body(slot_ref, y_hbm, x_ref, r_ref, g_ref, wg_ref, p_ref, wp_ref, fg_ref, o_ref,
                      y0, y1, sem, *, final):
    i = pl.program_id(0)
    tm = x_ref.shape[0]
    slot = i % 2

    def gather(blk, s):
        def issue(r, c):
            a = (blk * tm + r) * TOP_K
            _row_copy(y_hbm, slot_ref[a], y0.at[s], r, sem.at[s]).start()
            _row_copy(y_hbm, slot_ref[a + 1], y1.at[s], r, sem.at[s]).start()
            return c

        lax.fori_loop(0, tm, issue, 0, unroll=8)

    @pl.when(i == 0)
    def _():
        gather(0, 0)

    pltpu.make_async_copy(y_hbm.at[pl.ds(0, tm), :], y0.at[slot], sem.at[slot]).wait()
    pltpu.make_async_copy(y_hbm.at[pl.ds(0, tm), :], y1.at[slot], sem.at[slot]).wait()

    @pl.when(i + 1 < pl.num_programs(0))
    def _():
        gather(i + 1, 1 - slot)

    route = r_ref[...]
    x = x_ref[...] + y0[slot] * route[:, 2:3] + y1[slot] * route[:, 3:4]
    gate = _sigmoid(_dot(_rms(x, g_ref[...]).astype(BF16), wg_ref[...]))
    y = x + gate * _dot(p_ref[...].astype(BF16), wp_ref[...])
    if final:
        y = _rms(y, fg_ref[...])
    o_ref[...] = y


def _combine_ple(x, y, route, slots, g, wg, p, wp, fg, *, final, tm):
    T, D = x.shape
    P = p.shape[1]
    const = lambda shape: pl.BlockSpec(shape, lambda i, s: (0, 0))
    return pl.pallas_call(
        functools.partial(_combine_ple_body, final=final),
        grid_spec=pltpu.PrefetchScalarGridSpec(
            num_scalar_prefetch=1,
            grid=(T // tm,),
            in_specs=[pl.BlockSpec(memory_space=pl.ANY),
                      pl.BlockSpec((tm, D), lambda i, s: (i, 0)),
                      pl.BlockSpec((tm, LANES), lambda i, s: (i, 0)),
                      const((1, D)), const((D, D)),
                      pl.BlockSpec((tm, P), lambda i, s: (i, 0)),
                      const((P, D)), const((1, D))],
            out_specs=pl.BlockSpec((tm, D), lambda i, s: (i, 0)),
            scratch_shapes=[pltpu.VMEM((2, tm, D), F32), pltpu.VMEM((2, tm, D), F32),
                            pltpu.SemaphoreType.DMA((2,))]),
        out_shape=jax.ShapeDtypeStruct((T, D), F32),
        compiler_params=_cp(("arbitrary",), 48),
        name="combine_ple",
    )(slots, y, x, route, g.reshape(1, D), wg, p, wp, fg.reshape(1, D))


def _moe(x, g, w_router, wg, wu, wd):
    T, D = x.shape
    MB = MOE_BLOCK
    A = T * TOP_K
    route = _router(x, g, w_router, tm=512)
    flat_e = route[:, :TOP_K].astype(I32).reshape(A)
    onehot = (flat_e[:, None] == jnp.arange(N_EXPERTS, dtype=I32)[None, :]).astype(I32)
    csum = jnp.cumsum(onehot, axis=0)
    rank = jnp.take_along_axis(csum, flat_e[:, None], axis=1)[:, 0] - 1
    padded = (csum[-1] + MB - 1) // MB * MB
    pend = jnp.cumsum(padded)
    slots = (pend - padded)[flat_e] + rank
    n_blk = A // MB + N_EXPERTS
    tok = jnp.zeros((n_blk * MB,), I32).at[slots].set(jnp.arange(A, dtype=I32) // TOP_K)
    blk_start = jnp.arange(n_blk, dtype=I32) * MB
    blk_e = jnp.minimum(jnp.sum(blk_start[:, None] >= pend[None, :], axis=1), N_EXPERTS - 1).astype(I32)
    n_used = (pend[-1:] // MB).astype(I32)
    y = _experts(x, g, tok, blk_e, n_used, wg, wu, wd, tf=min(1024, wg.shape[2]))
    return y, route, slots.astype(I32)


def _ab_in_weight(w_in):
    w_in = w_in.astype(BF16)
    r, k, v, wl, al, gl, zq, zc, zqi, zki, zwi = jnp.split(
        w_in, np.cumsum([1024, 1024, 1024, 96, 96, 256, 1024, 256, 1024, 64]).tolist(), axis=1)
    D = w_in.shape[0]
    z = lambda n: jnp.zeros((D, n), w_in.dtype)
    cols = [r, k, v, zq, zqi, gl, zc, wl, z(32), al, z(32), zki, zki, zwi, z(AB_COLS - AB_ZWI - 16)]
    return jnp.concatenate(cols, axis=1)


def kernel(x, p, norm_mix, norm_ffn, ab_w_in, ab_mu, rwkv_w0, rwkv_w_up, rwkv_a0, rwkv_a_up, rwkv_g_up, rwkv_k_k, rwkv_k_a, rwkv_r_k, rwkv_gn_w, rwkv_gn_b, dsa_ckv_norm, dsa_w_uk, dsa_w_uv, ab_w_out, rel_bias, ffn_w_gate, ffn_w_up, ffn_w_down, gdn_w_in, gdn_conv, gdn_a_log, gdn_dt_bias, gdn_out_norm, gdn_w_out, moe_router, moe_w_gate, moe_w_up, moe_w_down, ple_norm, ple_w_gate, ple_w_proj, final_norm):
    B, L, D = x.shape
    T = B * L
    xf = x.reshape(T, D)
    bf = lambda w: w.astype(BF16)

    z = _norm_mm(xf, norm_mix[0], _ab_in_weight(ab_w_in[0]), tm=1024, tn=1024)
    y_a = _rwkv(z, B, L, ab_mu[0], rwkv_w0[0], rwkv_w_up[0], rwkv_a0[0], rwkv_a_up[0], rwkv_g_up[0],
                rwkv_k_k[0], rwkv_k_a[0], rwkv_r_k[0], rwkv_gn_w[0], rwkv_gn_b[0])
    y_b = _dsa(z, B, L, dsa_ckv_norm[0], dsa_w_uk[0], dsa_w_uv[0], rel_bias)
    xf = _mm_res([y_a, y_b], bf(ab_w_out[0]), xf, tm=1024, tn=1024)
    xf = _ffn(xf, norm_ffn[0], ffn_w_gate[0], ffn_w_up[0], ffn_w_down[0], tm=1024, tf=256)
    xf = _ple(xf, ple_norm[0], bf(ple_w_gate[0]), p[0].reshape(T, PLE_DIM), bf(ple_w_proj[0]),
              final_norm, final=False, tm=512)

    z = _norm_mm(xf, norm_mix[1], _gd_in_weight(gdn_w_in[0]), tm=1024, tn=1280)
    gates = _gdn_gates(z, B, L, gdn_a_log[0], gdn_dt_bias[0])
    o = _gdn(z, gates, B, L, gdn_conv[0], gdn_out_norm[0])
    xf = _mm_res([o], bf(gdn_w_out[0]), xf, tm=512, tn=1024)
    y, route, slots = _moe(xf, norm_ffn[1], moe_router[0], bf(moe_w_gate[0]), bf(moe_w_up[0]),
                           bf(moe_w_down[0]))
    xf = _combine_ple(xf, y, route, slots, ple_norm[1], bf(ple_w_gate[1]), p[1].reshape(T, PLE_DIM),
                      bf(ple_w_proj[1]), final_norm, final=True, tm=256)
    return xf.reshape(B, L, D)
```

```python
import functools
import math

import numpy as np
import jax
import jax.numpy as jnp
from jax import lax
from jax.experimental import pallas as pl
from jax.experimental.pallas import tpu as pltpu

F32 = jnp.float32
BF16 = jnp.bfloat16
I32 = jnp.int32
HI = lax.Precision.HIGHEST

EPS = 1e-6
LANES = 128
MIB = 1024 * 1024

RWKV_H, RWKV_N = 16, 64
RWKV_W = RWKV_H * RWKV_N
RWKV_LORA = 96
RWKV_GATE = 256
RWKV_GN_EPS = 6.4e-4
RWKV_CHUNK = 64
RWKV_TB = 256

DSA_H, DSA_D, DSA_C = 8, 128, 256
IDX_H, IDX_D = 16, 64
TOPK_MAX = 256
QB = 128
KEY_GROUP = 4
SCORE_GROUP = 8
ATTEND_GROUP = 8
REL_BUCKETS, REL_MAX_DIST = 32, 128
NEG = -1e30
INT_MIN = -(2 ** 31)

GDN_QK_H, GDN_V_H, GDN_D = 16, 32, 128
GDN_KW = GDN_QK_H * GDN_D
GDN_VW = GDN_V_H * GDN_D
GDN_CHUNK = 64
GDN_TB = 128
GDN_GATES_TB = 256
GDN_CONV_W = 4
GDN_HEADS_PER_STEP = 16

N_EXPERTS, TOP_K = 8, 2
MOE_BLOCK = 512
PLE_DIM = 256

AB_R, AB_K, AB_V = 0, 1024, 2048
AB_ZQ, AB_ZQI = 3072, 4096
AB_GL, AB_ZC = 5120, 5376
AB_WL, AB_AL = 5632, 5760
AB_ZKI, AB_ZWI = 5888, 6016
AB_COLS = 6144
GD_Q, GD_K, GD_V, GD_ZG, GD_GATES = 0, 2048, 4096, 8192, 12288
GD_COLS = 12800


def _cp(sem, vmem_mib):
    return pltpu.CompilerParams(dimension_semantics=sem, vmem_limit_bytes=vmem_mib * MIB)


def _dot(a, b, precision=None):
    return jnp.dot(a, b, preferred_element_type=F32, precision=precision)


def _dot_nt(a, b, precision=None):
    return lax.dot_general(a, b, (((1,), (1,)), ((), ())), preferred_element_type=F32,
                           precision=precision)


def _split(x):
    hi = x.astype(BF16)
    return hi, (x - hi.astype(F32)).astype(BF16)


def _seg_dot(x, ones_bf):
    hi, lo = _split(x)
    return _dot(hi, ones_bf) + _dot(lo, ones_bf)


def _rms(x, g):
    ms = jnp.mean(x * x, axis=-1, keepdims=True)
    return x * lax.rsqrt(ms + EPS) * g


def _sigmoid(x):
    return 0.5 * jnp.tanh(0.5 * x) + 0.5


def _silu(x):
    return x * _sigmoid(x)


def _softplus(x):
    return jnp.maximum(x, 0.0) + jnp.log1p(jnp.exp(-jnp.abs(x)))


def _iota(shape, dim):
    return lax.broadcasted_iota(I32, shape, dim)


def _norm_mm_body(x_ref, g_ref, w_ref, o_ref, xn_ref):
    @pl.when(pl.program_id(1) == 0)
    def _():
        xn_ref[...] = _rms(x_ref[...], g_ref[...]).astype(BF16)

    o_ref[...] = _dot(xn_ref[...], w_ref[...]).astype(o_ref.dtype)


def _norm_mm(x, g, w, *, tm, tn):
    T, K = x.shape
    N = w.shape[1]
    return pl.pallas_call(
        _norm_mm_body,
        grid=(T // tm, N // tn),
        in_specs=[pl.BlockSpec((tm, K), lambda i, j: (i, 0)),
                  pl.BlockSpec((1, K), lambda i, j: (0, 0)),
                  pl.BlockSpec((K, tn), lambda i, j: (0, j))],
        out_specs=pl.BlockSpec((tm, tn), lambda i, j: (i, j)),
        out_shape=jax.ShapeDtypeStruct((T, N), F32),
        scratch_shapes=[pltpu.VMEM((tm, K), BF16)],
        compiler_params=_cp(("parallel", "arbitrary"), 48),
        name="norm_mm",
    )(x, g.reshape(1, K), w)


def _mm_res_body(*refs):
    *a_refs, w_ref, r_ref, o_ref = refs
    acc = r_ref[...]
    off = 0
    for a_ref in a_refs:
        k = a_ref.shape[1]
        acc = acc + _dot(a_ref[...], w_ref[off:off + k, :])
        off += k
    o_ref[...] = acc


def _mm_res(parts, w, res, *, tm, tn):
    T = res.shape[0]
    K, N = w.shape
    assert sum(a.shape[1] for a in parts) == K
    return pl.pallas_call(
        _mm_res_body,
        grid=(T // tm, N // tn),
        in_specs=[pl.BlockSpec((tm, a.shape[1]), lambda i, j: (i, 0)) for a in parts]
        + [pl.BlockSpec((K, tn), lambda i, j: (0, j)),
           pl.BlockSpec((tm, tn), lambda i, j: (i, j))],
        out_specs=pl.BlockSpec((tm, tn), lambda i, j: (i, j)),
        out_shape=jax.ShapeDtypeStruct((T, N), F32),
        compiler_params=_cp(("parallel", "arbitrary"), 48),
        name="mm_res",
    )(*parts, w, res)


def _ple_body(x_ref, g_ref, wg_ref, p_ref, wp_ref, fg_ref, o_ref, *, final):
    x = x_ref[...]
    gate = _sigmoid(_dot(_rms(x, g_ref[...]).astype(BF16), wg_ref[...]))
    y = x + gate * _dot(p_ref[...].astype(BF16), wp_ref[...])
    if final:
        y = _rms(y, fg_ref[...])
    o_ref[...] = y


def _ple(x, g, wg, p, wp, fg, *, final, tm):
    T, D = x.shape
    P = p.shape[1]
    return pl.pallas_call(
        functools.partial(_ple_body, final=final),
        grid=(T // tm,),
        in_specs=[pl.BlockSpec((tm, D), lambda i: (i, 0)),
                  pl.BlockSpec((1, D), lambda i: (0, 0)),
                  pl.BlockSpec((D, D), lambda i: (0, 0)),
                  pl.BlockSpec((tm, P), lambda i: (i, 0)),
                  pl.BlockSpec((P, D), lambda i: (0, 0)),
                  pl.BlockSpec((1, D), lambda i: (0, 0))],
        out_specs=pl.BlockSpec((tm, D), lambda i: (i, 0)),
        out_shape=jax.ShapeDtypeStruct((T, D), F32),
        compiler_params=_cp(("parallel",), 48),
        name="ple",
    )(x, g.reshape(1, D), wg, p, wp, fg.reshape(1, D))


def _ffn_body(x_ref, g_ref, wg_ref, wu_ref, wd_ref, o_ref, xn_ref):
    @pl.when(pl.program_id(1) == 0)
    def _():
        x = x_ref[...]
        xn_ref[...] = _rms(x, g_ref[...]).astype(BF16)
        o_ref[...] = x

    xn = xn_ref[...]
    h = (_silu(_dot(xn, wg_ref[...])) * _dot(xn, wu_ref[...])).astype(BF16)
    o_ref[...] += _dot(h, wd_ref[...])


def _ffn(x, g, wg, wu, wd, *, tm, tf):
    T, D = x.shape
    Fh = wg.shape[1]
    return pl.pallas_call(
        _ffn_body,
        grid=(T // tm, Fh // tf),
        in_specs=[pl.BlockSpec((tm, D), lambda i, f: (i, 0)),
                  pl.BlockSpec((1, D), lambda i, f: (0, 0)),
                  pl.BlockSpec((D, tf), lambda i, f: (0, f)),
                  pl.BlockSpec((D, tf), lambda i, f: (0, f)),
                  pl.BlockSpec((tf, D), lambda i, f: (f, 0))],
        out_specs=pl.BlockSpec((tm, D), lambda i, f: (i, 0)),
        out_shape=jax.ShapeDtypeStruct((T, D), F32),
        scratch_shapes=[pltpu.VMEM((tm, D), BF16)],
        compiler_params=_cp(("parallel", "arbitrary"), 56),
        name="ffn",
    )(x, g.reshape(1, D), wg, wu, wd)


def _shift_mix(x, prev_row, mu):
    xs = pltpu.roll(x, 1, axis=0)
    xs = jnp.where(_iota(x.shape, 0) == 0, prev_row, xs)
    return x + (xs - x) * mu


def _bd(x, lo):
    return jnp.concatenate([jnp.where(lo, x, 0.0), jnp.where(lo, 0.0, x)], axis=0)


def _unit_lower_inverses(xs, eye, steps, sign):
    ps = [eye + x if sign > 0 else eye - x for x in xs]
    xbs = [x.astype(BF16) for x in xs]
    for _ in range(steps):
        xbs = [_dot(xb, xb).astype(BF16) for xb in xbs]
        ps = [p + _dot(p.astype(BF16), xb) for p, xb in zip(ps, xbs)]
    return ps


def _rwkv_body(rkv_ref, gl_ref, wa_ref, mu_rkv_ref, mu_gl_ref, mu_wa_ref, w0_ref, wup_ref,
               a0_ref, aup_ref, gup_ref, kk_ref, ka_ref, rk_ref, gnw_ref, gnb_ref, o_ref,
               prev_rkv, prev_gl, prev_wa, state, r_s, k_s, v_s, kn_s, a_s, lw_s, cg_s, g_s, y_s):
    NB, TB = rkv_ref.shape[0], rkv_ref.shape[1]
    C = RWKV_CHUNK
    NP = RWKV_W // LANES

    @pl.when(pl.program_id(0) == 0)
    def _():
        prev_rkv[...] = jnp.zeros_like(prev_rkv)
        prev_gl[...] = jnp.zeros_like(prev_gl)
        prev_wa[...] = jnp.zeros_like(prev_wa)
        state[...] = jnp.zeros_like(state)

    lane = _iota((1, LANES), 1)
    lo = lane < RWKV_N
    r128 = _iota((LANES, LANES), 0)
    c128 = _iota((LANES, LANES), 1)
    same = (r128 // C) == (c128 // C)
    seg_ones = jnp.where(same, 1.0, 0.0).astype(BF16)
    eye = jnp.where(r128 == c128, 1.0, 0.0).astype(F32)
    strict = same & (c128 < r128)
    incl = same & (c128 <= r128)
    rt = _iota((TB, TB), 0)
    ct = _iota((TB, TB), 1)
    tri_chunks = jnp.where(((rt // C) == (ct // C)) & (ct <= rt), 1.0, 0.0).astype(BF16)

    for bi in range(NB):
        rb = slice(bi * TB, (bi + 1) * TB)
        pb = slice(bi, bi + 1)
        wa_raw = wa_ref[bi]
        wa = _shift_mix(wa_raw, prev_wa[pb, :], mu_wa_ref[...])
        prev_wa[pb, :] = wa_raw[TB - 1:TB, :]
        wl = jnp.tanh(wa[:, :LANES]).astype(BF16)
        al = wa[:, LANES:].astype(BF16)
        w = -_softplus(-(w0_ref[...] + _dot(wl, wup_ref[...]))) - 0.5
        lw = -jnp.exp(w)
        lw_s[rb, :] = lw
        lw_hi, lw_lo = _split(lw)
        cg_s[rb, :] = _dot(tri_chunks, lw_hi) + _dot(tri_chunks, lw_lo)
        a_s[rb, :] = _sigmoid(a0_ref[...] + _dot(al, aup_ref[...]))
        gl_raw = gl_ref[bi]
        gl = _shift_mix(gl_raw, prev_gl[pb, :], mu_gl_ref[...])
        prev_gl[pb, :] = gl_raw[TB - 1:TB, :]
        g_s[rb, :] = _dot(_sigmoid(gl).astype(BF16), gup_ref[...])

        for p in range(NP):
            cs = slice(p * LANES, (p + 1) * LANES)
            cols = [slice(off + p * LANES, off + (p + 1) * LANES) for off in (AB_R, AB_K, AB_V)]
            mixed = []
            for c in cols:
                raw = rkv_ref[bi, :, c]
                mixed.append(_shift_mix(raw, prev_rkv[pb, c], mu_rkv_ref[:, c]))
                prev_rkv[pb, c] = raw[TB - 1:TB, :]
            r, k, v = mixed
            kk = k * kk_ref[:, cs]
            kn_s[rb, cs] = kk * lax.rsqrt(_seg_dot(kk * kk, seg_ones) + 1e-6)
            r_s[rb, cs] = r
            k_s[rb, cs] = k * (1.0 + (a_s[rb, cs] - 1.0) * ka_ref[:, cs])
            v_s[rb, cs] = v

    def chunk(c, carry):
        pairs = range(NB * NP)
        rws = [pl.ds(pl.multiple_of((q // NP) * TB + c * C, C), C) for q in pairs]
        css = [slice((q % NP) * LANES, (q % NP + 1) * LANES) for q in pairs]
        gam_last, ar, bk, bk_end, v_f, v_t = [], [], [], [], [], []
        for rows, cs in zip(rws, css):
            cg = cg_s[rows, cs]
            gam = jnp.exp(cg)
            gam_inv = jnp.exp(-cg)
            gam_prev = jnp.exp(cg - lw_s[rows, cs])
            gl_ = gam[C - 1:C, :]
            kn = kn_s[rows, cs]
            b_raw = kn * a_s[rows, cs] * gam_inv
            k_raw = k_s[rows, cs] * gam_inv
            gam_last.append(gl_)
            ar.append(jnp.concatenate([_bd(-kn * gam_prev, lo), _bd(r_s[rows, cs] * gam, lo)],
                                      axis=0).astype(BF16))
            bk.append(jnp.concatenate([_bd(b_raw, lo), _bd(k_raw, lo)], axis=0).astype(BF16))
            bk_end.append(jnp.concatenate([_bd(b_raw * gl_, lo), _bd(k_raw * gl_, lo)],
                                          axis=0).astype(BF16))
            vf = _bd(v_s[rows, cs], lo)
            v_f.append(vf)
            v_t.append(vf.astype(BF16))
        score = [_dot_nt(ar[p], bk[p]) for p in pairs]
        a_ab = [jnp.where(strict, s[:LANES, :LANES], 0.0) for s in score]
        a_akv = [_dot(jnp.where(strict, score[p][:LANES, LANES:], 0.0).astype(BF16), v_t[p])
                 for p in pairs]
        r_abk = [jnp.concatenate([jnp.where(incl, s[LANES:, :LANES], 0.0),
                                  jnp.where(incl, s[LANES:, LANES:], 0.0)], axis=1).astype(BF16)
                 for s in score]
        t_inv = [t.astype(BF16) for t in _unit_lower_inverses(a_ab, eye, 5, 1)]
        s_old = [state[p] for p in pairs]
        sproj = [_dot_nt(ar[p], s_old[p].astype(BF16)) for p in pairs]
        u = [_dot(t_inv[p], (sproj[p][:LANES] + a_akv[p]).astype(BF16)) for p in pairs]
        y_bd = [sproj[p][LANES:] + _dot(r_abk[p], jnp.concatenate([u[p].astype(BF16), v_t[p]], axis=0))
                for p in pairs]
        for p in pairs:
            y_s[rws[p], css[p]] = y_bd[p][:C] + y_bd[p][C:]
        uvT = [jnp.concatenate([u[p].T, v_f[p].T], axis=1).astype(BF16) for p in pairs]
        for p in pairs:
            state[p] = s_old[p] * gam_last[p] + _dot(uvT[p], bk_end[p])
        return carry

    lax.fori_loop(0, TB // C, chunk, 0)

    for bi in range(NB):
        rb = slice(bi * TB, (bi + 1) * TB)
        for p in range(NP):
            cs = slice(p * LANES, (p + 1) * LANES)
            y = y_s[rb, cs]
            mean = _seg_dot(y, seg_ones) * (1.0 / RWKV_N)
            d = y - mean
            var = _seg_dot(d * d, seg_ones) * (1.0 / RWKV_N)
            yn = d * lax.rsqrt(var + RWKV_GN_EPS) * gnw_ref[:, cs] + gnb_ref[:, cs]
            bonus = _seg_dot(r_s[rb, cs] * k_s[rb, cs] * rk_ref[:, cs], seg_ones) * v_s[rb, cs]
            o_ref[bi, :, cs] = ((yn + bonus) * g_s[rb, cs]).astype(o_ref.dtype)


def _rwkv(z, B, L, mu, w0, w_up, a0, a_up, g_up, k_k, k_a, r_k, gn_w, gn_b):
    TB = RWKV_TB
    nb = L // TB
    W = RWKV_W
    mu_r, mu_k, mu_v, mu_wl, mu_al, mu_gl = jnp.split(
        mu, np.cumsum([W, W, W, RWKV_LORA, RWKV_LORA])[:].tolist())
    pad = LANES - RWKV_LORA
    mu_rkv = jnp.concatenate([mu_r, mu_k, mu_v]).reshape(1, 3 * W)
    mu_wa = jnp.concatenate([jnp.pad(mu_wl, (0, pad)), jnp.pad(mu_al, (0, pad))]).reshape(1, 2 * LANES)
    wup = jnp.pad(w_up, ((0, pad), (0, 0))).astype(BF16)
    aup = jnp.pad(a_up, ((0, pad), (0, 0))).astype(BF16)
    row = lambda t: t.reshape(1, W)
    vec = lambda n: pl.BlockSpec((1, n), lambda i: (0, 0))
    big = lambda: pltpu.VMEM((B * TB, W), F32)
    z3 = z.reshape(B, L, AB_COLS)
    out = pl.pallas_call(
        _rwkv_body,
        grid=(nb,),
        in_specs=[pl.BlockSpec((B, TB, 3 * W), lambda i: (0, i, 0)),
                  pl.BlockSpec((B, TB, RWKV_GATE), lambda i: (0, i, AB_GL // RWKV_GATE)),
                  pl.BlockSpec((B, TB, 2 * LANES), lambda i: (0, i, AB_WL // (2 * LANES))),
                  vec(3 * W), vec(RWKV_GATE), vec(2 * LANES), vec(W),
                  pl.BlockSpec((LANES, W), lambda i: (0, 0)),
                  vec(W),
                  pl.BlockSpec((LANES, W), lambda i: (0, 0)),
                  pl.BlockSpec((RWKV_GATE, W), lambda i: (0, 0)),
                  vec(W), vec(W), vec(W), vec(W), vec(W)],
        out_specs=pl.BlockSpec((B, TB, W), lambda i: (0, i, 0)),
        out_shape=jax.ShapeDtypeStruct((B, L, W), BF16),
        scratch_shapes=[pltpu.VMEM((B, 3 * W), F32), pltpu.VMEM((B, RWKV_GATE), F32),
                        pltpu.VMEM((B, 2 * LANES), F32),
                        pltpu.VMEM((B * W // LANES, LANES, LANES), F32),
                        big(), big(), big(), big(), big(), big(), big(), big(), big()],
        compiler_params=_cp(("arbitrary",), 56),
        name="rwkv7",
    )(z3, z3, z3, mu_rkv, mu_gl.reshape(1, RWKV_GATE), mu_wa, row(w0), wup, row(a0), aup,
      g_up.astype(BF16), row(k_k), row(k_a), row(r_k), row(gn_w), row(gn_b))
    return out.reshape(B * L, W)


def _t5_bucket_np(dist):
    n = np.maximum(dist, 0)
    exact = REL_BUCKETS // 2
    ratio = np.log(np.maximum(n, 1).astype(np.float32) / np.float32(exact)) / np.float32(
        math.log(REL_MAX_DIST / exact))
    large = exact + (ratio.astype(np.float32) * np.float32(REL_BUCKETS - exact)).astype(np.int32)
    return np.where(n < exact, n, np.minimum(large, REL_BUCKETS - 1)).astype(np.int32)


def _near_buckets():
    kl = np.arange(QB)[:, None]
    ql = np.arange(QB)[None, :]
    return np.stack([_t5_bucket_np(ql - kl), _t5_bucket_np(QB + ql - kl)])


def _dsa_body(tbl_ref, bkt_ref, zq_ref, zqi_ref, zc_ref, zki_ref, zwi_ref, cn_ref, wukT_ref,
              wuvT_ref, o_ref, c_all, cT_all, kibd_all, sc, qiT, qlatT, bias, m_s, l_s, alpha_s,
              oT, *, topk):
    b = pl.program_id(0)
    qb = pl.program_id(1)
    lo = _iota((1, LANES), 1) < IDX_D
    krow = _iota((QB, QB), 0)
    qcol = _iota((QB, QB), 1)
    hsl = [slice(h * QB, (h + 1) * QB) for h in range(DSA_H)]

    @pl.when((b == 0) & (qb == 0))
    def _():
        for t in range(2):
            bk = bkt_ref[t]
            for h in range(DSA_H):
                far = tbl_ref[REL_BUCKETS - 1, h]
                acc = jnp.zeros((QB, QB), F32)
                for bb in range(REL_BUCKETS - 1):
                    acc = jnp.where(bk == bb, tbl_ref[bb, h] - far, acc)
                bias[t, h] = acc

    c_new = _rms(zc_ref[...], cn_ref[...])
    c_all[qb] = c_new.astype(BF16)
    cT_all[qb] = c_new.T.astype(BF16)
    kibd_all[qb] = _bd(zki_ref[...], lo).astype(BF16)

    for p in range(IDX_H // 2):
        qiT[:, p * QB:(p + 1) * QB] = zqi_ref[:, p * LANES:(p + 1) * LANES].T.astype(BF16)
    wT = zwi_ref[...].T * (IDX_D ** -0.5 * IDX_H ** -0.5)
    w_rows = [wT[h:h + 1, :] for h in range(IDX_H)]
    for h in range(DSA_H):
        qhT = zq_ref[:, hsl[h]].T.astype(BF16)
        qlatT[:, hsl[h]] = (_dot(wukT_ref[h], qhT) * DSA_D ** -0.5).astype(BF16)

    def score_blocks(j, nk):
        kb = kibd_all[pl.ds(j, nk)].reshape(nk * 2 * QB, LANES)
        acc = [jnp.zeros((QB, QB), F32) for _ in range(nk)]
        for pp in range(IDX_H // 4):
            s = _dot(kb, qiT[:, 2 * pp * QB:2 * (pp + 1) * QB])
            for k in range(nk):
                even = s[2 * k * QB:(2 * k + 1) * QB]
                odd = s[(2 * k + 1) * QB:(2 * k + 2) * QB]
                acc[k] = (acc[k] + w_rows[4 * pp] * jnp.maximum(even[:, :QB], 0.0)
                          + w_rows[4 * pp + 1] * jnp.maximum(odd[:, :QB], 0.0)
                          + w_rows[4 * pp + 2] * jnp.maximum(even[:, QB:], 0.0)
                          + w_rows[4 * pp + 3] * jnp.maximum(odd[:, QB:], 0.0))
        for k in range(nk):
            bits = lax.bitcast_convert_type(acc[k], I32)
            key = bits ^ ((bits >> 31) & 0x7FFFFFFF)
            sc[j + k] = jnp.where((j + k == qb) & (krow > qcol), INT_MIN, key)

    def score_group(jj, carry):
        score_blocks(jj * SCORE_GROUP, SCORE_GROUP)
        return carry

    def score_single(j, carry):
        score_blocks(j, 1)
        return carry

    n_sgroup = (qb + 1) // SCORE_GROUP
    lax.fori_loop(0, n_sgroup, score_group, 0)
    s_done = n_sgroup * SCORE_GROUP
    s_half = (qb + 1 - s_done) // (SCORE_GROUP // 2)

    @pl.when(s_half == 1)
    def _():
        score_blocks(s_done, SCORE_GROUP // 2)

    lax.fori_loop(s_done + s_half * (SCORE_GROUP // 2), qb + 1, score_single, 0)

    n_quad = (qb + 1) // KEY_GROUP

    def count(pred):
        def quad(jj, a):
            blk = sc[pl.ds(jj * KEY_GROUP, KEY_GROUP)]
            for k in range(KEY_GROUP):
                a = a + jnp.where(pred(blk[k]), 1, 0)
            return a

        def single(j, a):
            return a + jnp.where(pred(sc[j]), 1, 0)

        a = lax.fori_loop(0, n_quad, quad, jnp.zeros((QB, QB), I32))
        a = lax.fori_loop(n_quad * KEY_GROUP, qb + 1, single, a)
        return jnp.sum(a, axis=0, keepdims=True)

    def bis_body(i, carry):
        t, n_t = carry
        cand = t ^ jnp.left_shift(jnp.int32(1), 31 - i)
        tot = count(lambda key: key >= cand)
        ok = tot >= topk
        return jnp.where(ok, cand, t), jnp.where(ok, tot, n_t)

    thr_raw, n_ge = lax.fori_loop(
        0, 32, bis_body,
        (jnp.full((1, QB), INT_MIN, I32), jnp.broadcast_to((qb + 1) * QB, (1, QB)).astype(I32)))
    thr = jnp.maximum(thr_raw, INT_MIN + 1)

    @pl.when(jnp.max(n_ge) > topk)
    def _():
        keep = (topk - count(lambda key: key > thr_raw)).astype(F32)
        tri = jnp.where(qcol <= krow, 1.0, 0.0).astype(BF16)

        def drop(j, seen):
            key = sc[j]
            tied = key == thr_raw
            tied_bf = jnp.where(tied, 1.0, 0.0).astype(BF16)
            rank = seen + _dot(tri, tied_bf)
            sc[j] = jnp.where(tied & (rank > keep), INT_MIN, key)
            return seen + jnp.sum(tied_bf.astype(F32), axis=0, keepdims=True)

        lax.fori_loop(0, qb + 1, drop, jnp.zeros((1, QB), F32))

    m_s[...] = jnp.full(m_s.shape, NEG, F32)
    l_s[...] = jnp.zeros_like(l_s)
    oT[...] = jnp.zeros_like(oT)

    def attend(j, nk, near):
        sel = sc[pl.ds(j, nk)].reshape(nk * QB, QB) >= thr
        lg = _dot(c_all[pl.ds(j, nk)].reshape(nk * QB, DSA_C), qlatT[...])
        cT = jnp.concatenate([cT_all[j + k] for k in range(nk)], axis=1)
        prs = []
        for h in range(DSA_H):
            lgh = lg[:, hsl[h]]
            if near is not None:
                lgh = lgh + bias[near, h]
            lgh = jnp.where(sel, lgh, NEG)
            m_old = m_s[:, hsl[h]]
            m_new = jnp.maximum(m_old, jnp.max(lgh, axis=0, keepdims=True))
            pr = jnp.exp(lgh - m_new)
            alpha_s[:, hsl[h]] = jnp.exp(m_old - m_new)
            l_s[:, hsl[h]] = alpha_s[:, hsl[h]] * l_s[:, hsl[h]] + jnp.sum(pr, axis=0, keepdims=True)
            m_s[:, hsl[h]] = m_new
            prs.append(pr.astype(BF16))
        oT[...] = alpha_s[...] * oT[...] + _dot(cT, jnp.concatenate(prs, axis=1))

    def far_group(jj, carry):
        attend(jj * ATTEND_GROUP, ATTEND_GROUP, None)
        return carry

    def far_single(j, carry):
        attend(j, 1, None)
        return carry

    n_far = jnp.maximum(qb - 1, 0)
    n_group = n_far // ATTEND_GROUP
    lax.fori_loop(0, n_group, far_group, 0)
    done = n_group * ATTEND_GROUP
    half = (n_far - done) // (ATTEND_GROUP // 2)

    @pl.when(half == 1)
    def _():
        attend(done, ATTEND_GROUP // 2, None)

    lax.fori_loop(done + half * (ATTEND_GROUP // 2), n_far, far_single, 0)

    @pl.when(qb >= 1)
    def _():
        attend(qb - 1, 1, 1)

    attend(qb, 1, 0)

    inv_l = 1.0 / l_s[...]
    for h in range(DSA_H):
        oh = (oT[:, hsl[h]] * inv_l[:, hsl[h]]).astype(BF16)
        o_ref[:, hsl[h]] = _dot(wuvT_ref[h], oh).T.astype(o_ref.dtype)


def _dsa(z, B, L, ckv_norm, w_uk, w_uv, rel_bias):
    nq = L // QB
    topk = min(TOPK_MAX, L // 4)
    wukT = jnp.swapaxes(w_uk, 1, 2).astype(BF16)
    wuvT = jnp.swapaxes(w_uv, 1, 2).astype(BF16)
    blk = lambda w, off: pl.BlockSpec((QB, w), lambda b, q: (b * nq + q, off // w))
    full = lambda shape: pl.BlockSpec(shape, lambda b, q: (0,) * len(shape))
    W = DSA_H * DSA_D
    return pl.pallas_call(
        functools.partial(_dsa_body, topk=topk),
        grid=(B, nq),
        in_specs=[pl.BlockSpec(memory_space=pltpu.SMEM),
                  full((2, QB, QB)),
                  blk(W, AB_ZQ), blk(IDX_H * IDX_D, AB_ZQI), blk(DSA_C, AB_ZC),
                  blk(LANES, AB_ZKI), blk(LANES, AB_ZWI),
                  full((1, DSA_C)), full((DSA_H, DSA_C, DSA_D)), full((DSA_H, DSA_D, DSA_C))],
        out_specs=pl.BlockSpec((QB, W), lambda b, q: (b * nq + q, 0)),
        out_shape=jax.ShapeDtypeStruct((B * L, W), BF16),
        scratch_shapes=[pltpu.VMEM((nq, QB, DSA_C), BF16), pltpu.VMEM((nq, DSA_C, QB), BF16),
                        pltpu.VMEM((nq, 2 * QB, LANES), BF16), pltpu.VMEM((nq, QB, QB), I32),
                        pltpu.VMEM((LANES, IDX_H // 2 * QB), BF16), pltpu.VMEM((DSA_C, W), BF16),
                        pltpu.VMEM((2, DSA_H, QB, QB), F32),
                        pltpu.VMEM((1, W), F32), pltpu.VMEM((1, W), F32), pltpu.VMEM((1, W), F32),
                        pltpu.VMEM((DSA_C, W), F32)],
        compiler_params=_cp(("arbitrary", "arbitrary"), 48),
        name="dsa",
    )(rel_bias, jnp.asarray(_near_buckets()), z, z, z, z, z, ckv_norm.reshape(1, DSA_C), wukT, wuvT)


def _gdn_gates_body(z_ref, alog_ref, dtb_ref, o_ref):
    TB = z_ref.shape[0]
    C = GDN_CHUNK
    z = z_ref[...]
    sub = _iota((1, LANES), 1) % 8
    beta = _sigmoid(z)
    g = -jnp.exp(alog_ref[...]) * _softplus(z + dtb_ref[...])
    rt = _iota((TB, TB), 0)
    ct = _iota((TB, TB), 1)
    same = (rt // C) == (ct // C)
    g_hi, g_rest = _split(g)
    g_mid, g_lo = _split(g - g_hi.astype(F32))
    tri = jnp.where(same & (ct <= rt), 1.0, 0.0).astype(BF16)
    blk = jnp.where(same, 1.0, 0.0).astype(BF16)
    cum = _dot(tri, g_hi) + _dot(tri, g_mid) + _dot(tri, g_lo)
    tot = _dot(blk, g_hi) + _dot(blk, g_mid) + _dot(blk, g_lo)
    tile =jnp.where(sub < 2, beta, jnp.where(sub < 4, g, jnp.where(sub < 6, cum, tot)))
    o_ref[...] = tile.T


def _gdn_gates(z, B, L, a_log, dt_bias):
    TB = GDN_GATES_TB
    nb = L // TB
    spread =lambda t: jnp.zeros((GDN_QK_H, 8), F32).at[:, 2:].set(
        jnp.tile(t.reshape(GDN_QK_H, 2), (1, 3))).reshape(1, LANES)
    return pl.pallas_call(
        _gdn_gates_body,
        grid=(B, nb),
        in_specs=[pl.BlockSpec((TB, LANES), lambda b, i: (b * nb + i, GD_GATES // LANES)),
                  pl.BlockSpec((1, LANES), lambda b, i: (0, 0)),
                  pl.BlockSpec((1, LANES), lambda b, i: (0, 0))],
        out_specs=pl.BlockSpec((None, LANES, TB), lambda b, i: (b, 0, i)),
        out_shape=jax.ShapeDtypeStruct((B, LANES, L), F32),
        compiler_params=_cp(("parallel", "parallel"), 32),
        name="gdn_gates",
    )(z, spread(a_log), spread(dt_bias))


def _gdn_body(zq_ref, zk_ref, zv_ref, zg_ref, gates_ref, cwq_ref, cwk_ref, cwv_ref, on_ref, o_ref,
              xbuf, state, q_s, k_s, v_s, *, hg):
    TB = zq_ref.shape[0]
    C = GDN_CHUNK
    D = GDN_D
    KW = GDN_CONV_W
    W = 4 * D * hg

    @pl.when(pl.program_id(2) == 0)
    def _():
        xbuf[0:8, :] = jnp.zeros((8, W), F32)
        state[...] = jnp.zeros_like(state)

    xbuf[8:TB + 8, 0:D * hg] = zq_ref[...]
    xbuf[8:TB + 8, D * hg:2 * D * hg] = zk_ref[...]
    xbuf[8:TB + 8, 2 * D * hg:W] = zv_ref[...]
    cw = jnp.concatenate([cwq_ref[...], cwk_ref[...], cwv_ref[...]], axis=1)
    l2 = lambda t: t * lax.rsqrt(jnp.sum(t * t, axis=-1, keepdims=True) + 1e-6)
    for g in range(W // D):
        cols = slice(g * D, (g + 1) * D)
        acc = jnp.zeros((TB, D), F32)
        for j in range(KW):
            acc = acc + cw[j:j + 1, cols] * xbuf[8 - (KW - 1) + j:8 - (KW - 1) + j + TB, cols]
        act = _silu(acc)
        if g < hg:
            q_s[:, cols] = l2(act) * D ** -0.5
        elif g < 2 * hg:
            k_s[:, (g - hg) * D:(g - hg + 1) * D] = l2(act)
        else:
            v_s[:, (g - 2 * hg) * D:(g - 2 * hg + 1) * D] = act
    xbuf[0:8, :] = xbuf[TB:TB + 8, :]

    lane = _iota((1, LANES), 1)
    lo = lane < C
    r128 = _iota((LANES, LANES), 0)
    c128 = _iota((LANES, LANES), 1)
    same = (r128 // C) == (c128 // C)
    eye_m = r128 == c128
    eye = jnp.where(eye_m, 1.0, 0.0).astype(F32)
    strict = same & (c128 < r128)
    incl = same & (c128 <= r128)

    heads = range(hg)
    for c in range(TB // C):
        rows = slice(c * C, (c + 1) * C)
        e_st, e_neg, fb_st, etot_cat, kc, qc, dec = [], [], [], [], [], [], []
        for h in heads:
            win = gates_ref[8 * h:8 * h + 8, (c // 2) * LANES:(c // 2 + 1) * LANES]
            win_sw = pltpu.roll(win, C, axis=1)
            first, second = (win, win_sw) if c % 2 == 0 else (win_sw, win)
            st = lambda r: jnp.where(lo, first[r:r + 1, :], second[r + 1:r + 2, :])
            beta, cum, tot = st(0), st(4), st(6)
            etot = jnp.exp(tot)
            etot_sw = pltpu.roll(etot, C, axis=1)
            e_st.append(jnp.exp(cum))
            e_neg.append(-jnp.exp(cum))
            fb_st.append(jnp.exp(tot - cum) * beta)
            etot_cat.append(jnp.concatenate([jnp.where(lo, etot, etot_sw),
                                             jnp.where(lo, etot_sw, etot)], axis=1))
            kc.append(k_s[rows, h * D:(h + 1) * D])
            qc.append(q_s[rows, h * D:(h + 1) * D])
            cum_b = jnp.broadcast_to(cum, (LANES, LANES))
            dec.append(jnp.exp(jnp.where(incl, cum_b.T - cum_b, NEG)) * beta)
        score = [_dot_nt(jnp.concatenate([kc[h], kc[h], qc[h], qc[h]], axis=0).astype(BF16),
                         jnp.concatenate([kc[h], kc[h]], axis=0).astype(BF16))
                 for h in heads]
        t_inv = _unit_lower_inverses(
            [jnp.where(strict, score[h][:LANES] * dec[h], 0.0) for h in heads], eye, 5, -1)
        t_cat = [jnp.concatenate([t_inv[h], t_inv[h] * e_neg[h]], axis=1).astype(BF16) for h in heads]
        a_cat = [jnp.concatenate([eye * e_st[h], score[h][LANES:] * dec[h]], axis=1).astype(BF16)
                 for h in heads]
        s_old = [state[h] for h in heads]
        proj = [_dot(jnp.concatenate([kc[h], qc[h]], axis=0).astype(BF16), s_old[h].astype(BF16))
                for h in heads]
        rhs = [jnp.concatenate([v_s[rows, 2 * h * D:(2 * h + 1) * D],
                                v_s[rows, (2 * h + 1) * D:(2 * h + 2) * D],
                                proj[h][:C, :D], proj[h][:C, D:]], axis=0).astype(BF16) for h in heads]
        vn = [_dot(t_cat[h], rhs[h]) for h in heads]
        vn_bf = [x.astype(BF16) for x in vn]
        o_st = [_dot(a_cat[h], jnp.concatenate(
            [proj[h][C:, :D].astype(BF16), proj[h][C:, D:].astype(BF16), vn_bf[h]], axis=0))
            for h in heads]
        fv = [_dot((eye * fb_st[h]).astype(BF16), vn_bf[h]) for h in heads]
        for h in heads:
            state[h] = s_old[h] * etot_cat[h] + _dot(
                kc[h].T.astype(BF16), jnp.concatenate([fv[h][:C], fv[h][C:]], axis=1).astype(BF16))
        for h in heads:
            for u in range(2):
                cols = slice((2 * h + u) * D, (2 * h + u + 1) * D)
                oh = o_st[h][u * C:(u + 1) * C]
                o_ref[rows, cols] = (_rms(oh, on_ref[...]) * _silu(zg_ref[rows, cols])).astype(o_ref.dtype)


def _gdn(z, gates, B, L, conv_w, out_norm):
    TB = GDN_TB
    nb = L // TB
    D = GDN_D
    hg = GDN_HEADS_PER_STEP
    zspec = lambda w, off: pl.BlockSpec((TB, w), lambda b, h, i: (b * nb + i, off // w + h))
    cspec = lambda w, off: pl.BlockSpec((GDN_CONV_W, w), lambda b, h, i: (0, off // w + h))
    return pl.pallas_call(
        functools.partial(_gdn_body, hg=hg),
        grid=(B, GDN_QK_H // hg, nb),
        in_specs=[zspec(D * hg, GD_Q), zspec(D * hg, GD_K), zspec(2 * D * hg, GD_V),
                  zspec(2 * D * hg, GD_ZG),
                  pl.BlockSpec((None, 8 * hg, TB), lambda b, h, i: (b, h, i)),
                  cspec(D * hg, GD_Q), cspec(D * hg, GD_K), cspec(2 * D * hg, GD_V),
                  pl.BlockSpec((1, D), lambda b, h, i: (0, 0))],
        out_specs=pl.BlockSpec((TB, 2 * D * hg), lambda b, h, i: (b * nb + i, h)),
        out_shape=jax.ShapeDtypeStruct((B * L, GDN_VW), BF16),
        scratch_shapes=[pltpu.VMEM((TB + 8, 4 * D * hg), F32), pltpu.VMEM((hg, D, 2 * D), F32),
                        pltpu.VMEM((TB, D * hg), F32), pltpu.VMEM((TB, D * hg), F32),
                        pltpu.VMEM((TB, 2 * D * hg), F32)],
        compiler_params=_cp(("parallel", "parallel", "arbitrary"), 40),
        name="gdn",
    )(z, z, z, z, gates, conv_w, conv_w, conv_w, out_norm.reshape(1, D))


def _gd_in_weight(w_in):
    w_in = w_in.astype(BF16)
    qkv, zg, b, a = jnp.split(w_in, np.cumsum([2 * GDN_KW + GDN_VW, GDN_VW, GDN_V_H]).tolist(), axis=1)
    D = w_in.shape[0]
    pair = lambda t: t.reshape(D, GDN_QK_H, 2)
    gates = jnp.concatenate([pair(b), pair(a), pair(a), pair(a)], axis=2).reshape(D, LANES)
    pad = jnp.zeros((D, GD_COLS - GD_GATES - LANES), w_in.dtype)
    return jnp.concatenate([qkv, zg, gates, pad], axis=1)


def _router_body(x_ref, g_ref, wr_ref, o_ref):
    x_hi, x_lo = _split(_rms(x_ref[...], g_ref[...]))
    w_hi, w_lo = _split(wr_ref[...])
    logits = _dot(x_hi, w_hi) + _dot(x_lo, w_hi) + _dot(x_hi, w_lo)
    lane = _iota(logits.shape, 1)
    logits = jnp.where(lane < N_EXPERTS, logits, -jnp.inf)
    m1 = jnp.max(logits, axis=-1, keepdims=True)
    i1 = jnp.min(jnp.where(logits == m1, lane, LANES), axis=-1, keepdims=True)
    rest = jnp.where(lane == i1, -jnp.inf, logits)
    m2 = jnp.max(rest, axis=-1, keepdims=True)
    i2 = jnp.min(jnp.where(rest == m2, lane, LANES), axis=-1, keepdims=True)
    e = jnp.exp(m2 - m1)
    w1 = 1.0 / (1.0 + e)
    o_ref[...] = jnp.where(lane == 0, i1.astype(F32),
                           jnp.where(lane == 1, i2.astype(F32),
                                     jnp.where(lane == 2, w1, jnp.where(lane == 3, e * w1, 0.0))))


def _router(x, g, w_router, *, tm):
    T, D = x.shape
    wr = jnp.pad(w_router, ((0, 0), (0, LANES - N_EXPERTS)))
    return pl.pallas_call(
        _router_body,
        grid=(T // tm,),
        in_specs=[pl.BlockSpec((tm, D), lambda i: (i, 0)),
                  pl.BlockSpec((1, D), lambda i: (0, 0)),
                  pl.BlockSpec((D, LANES), lambda i: (0, 0))],
        out_specs=pl.BlockSpec((tm, LANES), lambda i: (i, 0)),
        out_shape=jax.ShapeDtypeStruct((T, LANES), F32),
        compiler_params=_cp(("parallel",), 32),
        name="router",
    )(x, g.reshape(1, D), wr)


def _row_copy(src_hbm, row, dst, r, sem):
    return pltpu.make_async_copy(src_hbm.at[pl.ds(row, 1), :], dst.at[pl.ds(r, 1), :], sem)


def _experts_body(tok_ref, be_ref, nu_ref, x_hbm, g_ref, wg_ref, wu_ref, wd_ref, o_ref,
                  xbuf, xn_ref, sem):
    i = pl.program_id(0)
    f = pl.program_id(1)
    MB = xbuf.shape[1]
    active = i < nu_ref[0]
    slot = i % 2

    def gather(blk, s):
        def issue(r, c):
            _row_copy(x_hbm, tok_ref[blk * MB + r], xbuf.at[s], r, sem.at[s]).start()
            return c

        lax.fori_loop(0, MB, issue, 0, unroll=8)

    @pl.when(f == 0)
    def _():
        o_ref[...] = jnp.zeros_like(o_ref)

    @pl.when((f == 0) & (i == 0) & active)
    def _():
        gather(0, 0)

    @pl.when((f == 0) & active)
    def _():
        pltpu.make_async_copy(x_hbm.at[pl.ds(0, MB), :], xbuf.at[slot], sem.at[slot]).wait()
        xn_ref[...] = _rms(xbuf[slot], g_ref[...]).astype(BF16)

    @pl.when((f == 1) & (i + 1 < nu_ref[0]))
    def _():
        gather(i + 1, 1 - slot)

    @pl.when(active)
    def _():
        xn = xn_ref[...]
        h = (_silu(_dot(xn, wg_ref[...])) * _dot(xn, wu_ref[...])).astype(BF16)
        o_ref[...] += _dot(h, wd_ref[...])


def _experts(x, g, tok, blk_e, n_used, wg, wu, wd, *, tf):
    T, D = x.shape
    MB = MOE_BLOCK
    n_blk = tok.shape[0] // MB
    Fh = wg.shape[2]
    assert Fh // tf >= 2, "the next block's rows are requested during hidden chunk 1"
    fe = lambda i, f, nu: jnp.where(i < nu[0], f, 0)
    return pl.pallas_call(
        _experts_body,
        grid_spec=pltpu.PrefetchScalarGridSpec(
            num_scalar_prefetch=3,
            grid=(n_blk, Fh // tf),
            in_specs=[pl.BlockSpec(memory_space=pl.ANY),
                      pl.BlockSpec((1, D), lambda i, f, tk, be, nu: (0, 0)),
                      pl.BlockSpec((None, D, tf), lambda i, f, tk, be, nu: (be[i], 0, fe(i, f, nu))),
                      pl.BlockSpec((None, D, tf), lambda i, f, tk, be, nu: (be[i], 0, fe(i, f, nu))),
                      pl.BlockSpec((None, tf, D), lambda i, f, tk, be, nu: (be[i], fe(i, f, nu), 0))],
            out_specs=pl.BlockSpec((MB, D), lambda i, f, tk, be, nu: (i, 0)),
            scratch_shapes=[pltpu.VMEM((2, MB, D), F32), pltpu.VMEM((MB, D), BF16),
                            pltpu.SemaphoreType.DMA((2,))]),
        out_shape=jax.ShapeDtypeStruct((n_blk * MB, D), F32),
        compiler_params=_cp(("arbitrary", "arbitrary"), 52),
        name="experts",
    )(tok, blk_e, n_used, x, g.reshape(1, D), wg, wu, wd)


def _combine_ple_body(slot_ref, y_hbm, x_ref, r_ref, g_ref, wg_ref, p_ref, wp_ref, fg_ref, o_ref,
                      y0, y1, sem, *, final):
    i = pl.program_id(0)
    tm = x_ref.shape[0]
    slot = i % 2

    def gather(blk, s):
        def issue(r, c):
            a = (blk * tm + r) * TOP_K
            _row_copy(y_hbm, slot_ref[a], y0.at[s], r, sem.at[s]).start()
            _row_copy(y_hbm, slot_ref[a + 1], y1.at[s], r, sem.at[s]).start()
            return c

        lax.fori_loop(0, tm, issue, 0, unroll=8)

    @pl.when(i == 0)
    def _():
        gather(0, 0)

    pltpu.make_async_copy(y_hbm.at[pl.ds(0, tm), :], y0.at[slot], sem.at[slot]).wait()
    pltpu.make_async_copy(y_hbm.at[pl.ds(0, tm), :], y1.at[slot], sem.at[slot]).wait()

    @pl.when(i + 1 < pl.num_programs(0))
    def _():
        gather(i + 1, 1 - slot)

    route = r_ref[...]
    x = x_ref[...] + y0[slot] * route[:, 2:3] + y1[slot] * route[:, 3:4]
    gate = _sigmoid(_dot(_rms(x, g_ref[...]).astype(BF16), wg_ref[...]))
    y = x + gate * _dot(p_ref[...].astype(BF16), wp_ref[...])
    if final:
        y = _rms(y, fg_ref[...])
    o_ref[...] = y


def _combine_ple(x, y, route, slots, g, wg, p, wp, fg, *, final, tm):
    T, D = x.shape
    P = p.shape[1]
    const = lambda shape: pl.BlockSpec(shape, lambda i, s: (0, 0))
    return pl.pallas_call(
        functools.partial(_combine_ple_body, final=final),
        grid_spec=pltpu.PrefetchScalarGridSpec(
            num_scalar_prefetch=1,
            grid=(T // tm,),
            in_specs=[pl.BlockSpec(memory_space=pl.ANY),
                      pl.BlockSpec((tm, D), lambda i, s: (i, 0)),
                      pl.BlockSpec((tm, LANES), lambda i, s: (i, 0)),
                      const((1, D)), const((D, D)),
                      pl.BlockSpec((tm, P), lambda i, s: (i, 0)),
                      const((P, D)), const((1, D))],
            out_specs=pl.BlockSpec((tm, D), lambda i, s: (i, 0)),
            scratch_shapes=[pltpu.VMEM((2, tm, D), F32), pltpu.VMEM((2, tm, D), F32),
                            pltpu.SemaphoreType.DMA((2,))]),
        out_shape=jax.ShapeDtypeStruct((T, D), F32),
        compiler_params=_cp(("arbitrary",), 48),
        name="combine_ple",
    )(slots, y, x, route, g.reshape(1, D), wg, p, wp, fg.reshape(1, D))


def _moe(x, g, w_router, wg, wu, wd):
    T, D = x.shape
    MB = MOE_BLOCK
    A = T * TOP_K
    route = _router(x, g, w_router, tm=512)
    flat_e = route[:, :TOP_K].astype(I32).reshape(A)
    onehot = (flat_e[:, None] == jnp.arange(N_EXPERTS, dtype=I32)[None, :]).astype(I32)
    csum = jnp.cumsum(onehot, axis=0)
    rank = jnp.take_along_axis(csum, flat_e[:, None], axis=1)[:, 0] - 1
    padded = (csum[-1] + MB - 1) // MB * MB
    pend = jnp.cumsum(padded)
    slots = (pend - padded)[flat_e] + rank
    n_blk = A // MB + N_EXPERTS
    tok = jnp.zeros((n_blk * MB,), I32).at[slots].set(jnp.arange(A, dtype=I32) // TOP_K)
    blk_start = jnp.arange(n_blk, dtype=I32) * MB
    blk_e = jnp.minimum(jnp.sum(blk_start[:, None] >= pend[None, :], axis=1), N_EXPERTS - 1).astype(I32)
    n_used = (pend[-1:] // MB).astype(I32)
    y = _experts(x, g, tok, blk_e, n_used, wg, wu, wd, tf=min(1024, wg.shape[2]))
    return y, route, slots.astype(I32)


def _ab_in_weight(w_in):
    w_in = w_in.astype(BF16)
    r, k, v, wl, al, gl, zq, zc, zqi, zki, zwi = jnp.split(
        w_in, np.cumsum([1024, 1024, 1024, 96, 96, 256, 1024, 256, 1024, 64]).tolist(), axis=1)
    D = w_in.shape[0]
    z = lambda n: jnp.zeros((D, n), w_in.dtype)
    cols = [r, k, v, zq, zqi, gl, zc, wl, z(32), al, z(32), zki, zki, zwi, z(AB_COLS - AB_ZWI - 16)]
    return jnp.concatenate(cols, axis=1)


def kernel(x, p, norm_mix, norm_ffn, ab_w_in, ab_mu, rwkv_w0, rwkv_w_up, rwkv_a0, rwkv_a_up, rwkv_g_up, rwkv_k_k, rwkv_k_a, rwkv_r_k, rwkv_gn_w, rwkv_gn_b, dsa_ckv_norm, dsa_w_uk, dsa_w_uv, ab_w_out, rel_bias, ffn_w_gate, ffn_w_up, ffn_w_down, gdn_w_in, gdn_conv, gdn_a_log, gdn_dt_bias, gdn_out_norm, gdn_w_out, moe_router, moe_w_gate, moe_w_up, moe_w_down, ple_norm, ple_w_gate, ple_w_proj, final_norm):
    B, L, D = x.shape
    T = B * L
    xf = x.reshape(T, D)
    bf = lambda w: w.astype(BF16)

    z = _norm_mm(xf, norm_mix[0], _ab_in_weight(ab_w_in[0]), tm=1024, tn=1024)
    y_a = _rwkv(z, B, L, ab_mu[0], rwkv_w0[0], rwkv_w_up[0], rwkv_a0[0], rwkv_a_up[0], rwkv_g_up[0],
                rwkv_k_k[0], rwkv_k_a[0], rwkv_r_k[0], rwkv_gn_w[0], rwkv_gn_b[0])
    y_b = _dsa(z, B, L, dsa_ckv_norm[0], dsa_w_uk[0], dsa_w_uv[0], rel_bias)
    xf = _mm_res([y_a, y_b], bf(ab_w_out[0]), xf, tm=1024, tn=1024)
    xf = _ffn(xf, norm_ffn[0], bf(ffn_w_gate[0]), bf(ffn_w_up[0]), bf(ffn_w_down[0]), tm=1024, tf=512)
    xf = _ple(xf, ple_norm[0], bf(ple_w_gate[0]), p[0].reshape(T, PLE_DIM), bf(ple_w_proj[0]),
              final_norm, final=False, tm=512)

    z = _norm_mm(xf, norm_mix[1], _gd_in_weight(gdn_w_in[0]), tm=1024, tn=1280)
    gates = _gdn_gates(z, B, L, gdn_a_log[0], gdn_dt_bias[0])
    o = _gdn(z, gates, B, L, gdn_conv[0], gdn_out_norm[0])
    xf = _mm_res([o], bf(gdn_w_out[0]), xf, tm=512, tn=1024)
    y, route, slots = _moe(xf, norm_ffn[1], moe_router[0], bf(moe_w_gate[0]), bf(moe_w_up[0]),
                           bf(moe_w_down[0]))
    xf = _combine_ple(xf, y, route, slots, ple_norm[1], bf(ple_w_gate[1]), p[1].reshape(T, PLE_DIM),
                      bf(ple_w_proj[1]), final_norm, final=True, tm=256)
    return xf.reshape(B, L, D)
```

```python
import functools
import math

import numpy as np
import jax
import jax.numpy as jnp
from jax import lax
from jax.experimental import pallas as pl
from jax.experimental.pallas import tpu as pltpu

F32 = jnp.float32
BF16 = jnp.bfloat16
I32 = jnp.int32
HI = lax.Precision.HIGHEST

EPS = 1e-6
LANES = 128
MIB = 1024 * 1024

RWKV_H, RWKV_N = 16, 64
RWKV_W = RWKV_H * RWKV_N
RWKV_LORA = 96
RWKV_GATE = 256
RWKV_GN_EPS = 6.4e-4
RWKV_CHUNK = 64
RWKV_TB = 256

DSA_H, DSA_D, DSA_C = 8, 128, 256
IDX_H, IDX_D = 16, 64
TOPK_MAX = 256
QB = 128
KEY_GROUP = 4
SCORE_GROUP = 8
ATTEND_GROUP = 8
REL_BUCKETS, REL_MAX_DIST = 32, 128
NEG = -1e30
INT_MIN = -(2 ** 31)

GDN_QK_H, GDN_V_H, GDN_D = 16, 32, 128
GDN_KW = GDN_QK_H * GDN_D
GDN_VW = GDN_V_H * GDN_D
GDN_CHUNK = 64
GDN_TB = 128
GDN_GATES_TB = 256
GDN_CONV_W = 4
GDN_HEADS_PER_STEP = 16

N_EXPERTS, TOP_K = 8, 2
MOE_BLOCK = 512
PLE_DIM = 256

AB_R, AB_K, AB_V = 0, 1024, 2048
AB_ZQ, AB_ZQI = 3072, 4096
AB_GL, AB_ZC = 5120, 5376
AB_WL, AB_AL = 5632, 5760
AB_ZKI, AB_ZWI = 5888, 6016
AB_COLS = 6144
GD_Q, GD_K, GD_V, GD_ZG, GD_GATES = 0, 2048, 4096, 8192, 12288
GD_COLS = 12800


def _cp(sem, vmem_mib):
    return pltpu.CompilerParams(dimension_semantics=sem, vmem_limit_bytes=vmem_mib * MIB)


def _dot(a, b, precision=None):
    return jnp.dot(a, b, preferred_element_type=F32, precision=precision)


def _dot_nt(a, b, precision=None):
    return lax.dot_general(a, b, (((1,), (1,)), ((), ())), preferred_element_type=F32,
                           precision=precision)


def _split(x):
    hi = x.astype(BF16)
    return hi, (x - hi.astype(F32)).astype(BF16)


def _seg_dot(x, ones_bf):
    hi, lo = _split(x)
    return _dot(hi, ones_bf) + _dot(lo, ones_bf)


def _rms(x, g):
    ms = jnp.mean(x * x, axis=-1, keepdims=True)
    return x * lax.rsqrt(ms + EPS) * g


def _sigmoid(x):
    return 0.5 * jnp.tanh(0.5 * x) + 0.5


def _silu(x):
    return x * _sigmoid(x)


def _softplus(x):
    return jnp.maximum(x, 0.0) + jnp.log1p(jnp.exp(-jnp.abs(x)))


def _iota(shape, dim):
    return lax.broadcasted_iota(I32, shape, dim)


def _norm_mm_body(x_ref, g_ref, w_ref, o_ref, xn_ref):
    @pl.when(pl.program_id(1) == 0)
    def _():
        xn_ref[...] = _rms(x_ref[...], g_ref[...]).astype(BF16)

    o_ref[...] = _dot(xn_ref[...], w_ref[...]).astype(o_ref.dtype)


def _norm_mm(x, g, w, *, tm, tn):
    T, K = x.shape
    N = w.shape[1]
    return pl.pallas_call(
        _norm_mm_body,
        grid=(T // tm, N // tn),
        in_specs=[pl.BlockSpec((tm, K), lambda i, j: (i, 0)),
                  pl.BlockSpec((1, K), lambda i, j: (0, 0)),
                  pl.BlockSpec((K, tn), lambda i, j: (0, j))],
        out_specs=pl.BlockSpec((tm, tn), lambda i, j: (i, j)),
        out_shape=jax.ShapeDtypeStruct((T, N), F32),
        scratch_shapes=[pltpu.VMEM((tm, K), BF16)],
        compiler_params=_cp(("parallel", "arbitrary"), 48),
        name="norm_mm",
    )(x, g.reshape(1, K), w)


def _mm_res_body(*refs):
    *a_refs, w_ref, r_ref, o_ref = refs
    acc = r_ref[...]
    off = 0
    for a_ref in a_refs:
        k = a_ref.shape[1]
        acc = acc + _dot(a_ref[...], w_ref[off:off + k, :])
        off += k
    o_ref[...] = acc


def _mm_res(parts, w, res, *, tm, tn):
    T = res.shape[0]
    K, N = w.shape
    assert sum(a.shape[1] for a in parts) == K
    return pl.pallas_call(
        _mm_res_body,
        grid=(T // tm, N // tn),
        in_specs=[pl.BlockSpec((tm, a.shape[1]), lambda i, j: (i, 0)) for a in parts]
        + [pl.BlockSpec((K, tn), lambda i, j: (0, j)),
           pl.BlockSpec((tm, tn), lambda i, j: (i, j))],
        out_specs=pl.BlockSpec((tm, tn), lambda i, j: (i, j)),
        out_shape=jax.ShapeDtypeStruct((T, N), F32),
        compiler_params=_cp(("parallel", "arbitrary"), 48),
        name="mm_res",
    )(*parts, w, res)


def _ple_body(x_ref, g_ref, wg_ref, p_ref, wp_ref, fg_ref, o_ref, *, final):
    x = x_ref[...]
    gate = _sigmoid(_dot(_rms(x, g_ref[...]).astype(BF16), wg_ref[...]))
    y = x + gate * _dot(p_ref[...].astype(BF16), wp_ref[...])
    if final:
        y = _rms(y, fg_ref[...])
    o_ref[...] = y


def _ple(x, g, wg, p, wp, fg, *, final, tm):
    T, D = x.shape
    P = p.shape[1]
    return pl.pallas_call(
        functools.partial(_ple_body, final=final),
        grid=(T // tm,),
        in_specs=[pl.BlockSpec((tm, D), lambda i: (i, 0)),
                  pl.BlockSpec((1, D), lambda i: (0, 0)),
                  pl.BlockSpec((D, D), lambda i: (0, 0)),
                  pl.BlockSpec((tm, P), lambda i: (i, 0)),
                  pl.BlockSpec((P, D), lambda i: (0, 0)),
                  pl.BlockSpec((1, D), lambda i: (0, 0))],
        out_specs=pl.BlockSpec((tm, D), lambda i: (i, 0)),
        out_shape=jax.ShapeDtypeStruct((T, D), F32),
        compiler_params=_cp(("parallel",), 48),
        name="ple",
    )(x, g.reshape(1, D), wg, p, wp, fg.reshape(1, D))


def _ffn_body(x_ref, g_ref, wg_ref, wu_ref, wd_ref, o_ref, xn_ref):
    @pl.when(pl.program_id(1) == 0)
    def _():
        x = x_ref[...]
        xn_ref[...] = _rms(x, g_ref[...]).astype(BF16)
        o_ref[...] = x

    xn = xn_ref[...]
    h = (_silu(_dot(xn, wg_ref[...])) * _dot(xn, wu_ref[...])).astype(BF16)
    o_ref[...] += _dot(h, wd_ref[...])


def _ffn(x, g, wg, wu, wd, *, tm, tf):
    T, D = x.shape
    Fh = wg.shape[1]
    return pl.pallas_call(
        _ffn_body,
        grid=(T // tm, Fh // tf),
        in_specs=[pl.BlockSpec((tm, D), lambda i, f: (i, 0)),
                  pl.BlockSpec((1, D), lambda i, f: (0, 0)),
                  pl.BlockSpec((D, tf), lambda i, f: (0, f)),
                  pl.BlockSpec((D, tf), lambda i, f: (0, f)),
                  pl.BlockSpec((tf, D), lambda i, f: (f, 0))],
        out_specs=pl.BlockSpec((tm, D), lambda i, f: (i, 0)),
        out_shape=jax.ShapeDtypeStruct((T, D), F32),
        scratch_shapes=[pltpu.VMEM((tm, D), BF16)],
        compiler_params=_cp(("parallel", "arbitrary"), 56),
        name="ffn",
    )(x, g.reshape(1, D), wg, wu, wd)


def _shift_mix(x, prev_row, mu):
    xs = pltpu.roll(x, 1, axis=0)
    xs = jnp.where(_iota(x.shape, 0) == 0, prev_row, xs)
    return x + (xs - x) * mu


def _bd(x, lo):
    return jnp.concatenate([jnp.where(lo, x, 0.0), jnp.where(lo, 0.0, x)], axis=0)


def _unit_lower_inverses(xs, eye, steps, sign):
    ps = [eye + x if sign > 0 else eye - x for x in xs]
    xbs = [x.astype(BF16) for x in xs]
    for _ in range(steps):
        xbs = [_dot(xb, xb).astype(BF16) for xb in xbs]
        ps = [p + _dot(p.astype(BF16), xb) for p, xb in zip(ps, xbs)]
    return ps


def _rwkv_body(rkv_ref, gl_ref, wa_ref, mu_rkv_ref, mu_gl_ref, mu_wa_ref, w0_ref, wup_ref,
               a0_ref, aup_ref, gup_ref, kk_ref, ka_ref, rk_ref, gnw_ref, gnb_ref, o_ref,
               prev_rkv, prev_gl, prev_wa, state, r_s, k_s, v_s, kn_s, a_s, lw_s, cg_s, g_s, y_s):
    NB, TB = rkv_ref.shape[0], rkv_ref.shape[1]
    C = RWKV_CHUNK
    NP = RWKV_W // LANES

    @pl.when(pl.program_id(0) == 0)
    def _():
        prev_rkv[...] = jnp.zeros_like(prev_rkv)
        prev_gl[...] = jnp.zeros_like(prev_gl)
        prev_wa[...] = jnp.zeros_like(prev_wa)
        state[...] = jnp.zeros_like(state)

    lane = _iota((1, LANES), 1)
    lo = lane < RWKV_N
    r128 = _iota((LANES, LANES), 0)
    c128 = _iota((LANES, LANES), 1)
    same = (r128 // C) == (c128 // C)
    seg_ones = jnp.where(same, 1.0, 0.0).astype(BF16)
    eye = jnp.where(r128 == c128, 1.0, 0.0).astype(F32)
    strict = same & (c128 < r128)
    incl = same & (c128 <= r128)
    rt = _iota((TB, TB), 0)
    ct = _iota((TB, TB), 1)
    tri_chunks = jnp.where(((rt // C) == (ct // C)) & (ct <= rt), 1.0, 0.0).astype(BF16)

    for bi in range(NB):
        rb = slice(bi * TB, (bi + 1) * TB)
        pb = slice(bi, bi + 1)
        wa_raw = wa_ref[bi]
        wa = _shift_mix(wa_raw, prev_wa[pb, :], mu_wa_ref[...])
        prev_wa[pb, :] = wa_raw[TB - 1:TB, :]
        wl = jnp.tanh(wa[:, :LANES]).astype(BF16)
        al = wa[:, LANES:].astype(BF16)
        w = -_softplus(-(w0_ref[...] + _dot(wl, wup_ref[...]))) - 0.5
        lw = -jnp.exp(w)
        lw_s[rb, :] = lw
        lw_hi, lw_lo = _split(lw)
        cg_s[rb, :] = _dot(tri_chunks, lw_hi) + _dot(tri_chunks, lw_lo)
        a_s[rb, :] = _sigmoid(a0_ref[...] + _dot(al, aup_ref[...]))
        gl_raw = gl_ref[bi]
        gl = _shift_mix(gl_raw, prev_gl[pb, :], mu_gl_ref[...])
        prev_gl[pb, :] = gl_raw[TB - 1:TB, :]
        g_s[rb, :] = _dot(_sigmoid(gl).astype(BF16), gup_ref[...])

        for p in range(NP):
            cs = slice(p * LANES, (p + 1) * LANES)
            cols = [slice(off + p * LANES, off + (p + 1) * LANES) for off in (AB_R, AB_K, AB_V)]
            mixed = []
            for c in cols:
                raw = rkv_ref[bi, :, c]
                mixed.append(_shift_mix(raw, prev_rkv[pb, c], mu_rkv_ref[:, c]))
                prev_rkv[pb, c] = raw[TB - 1:TB, :]
            r, k, v = mixed
            kk = k * kk_ref[:, cs]
            kn_s[rb, cs] = kk * lax.rsqrt(_seg_dot(kk * kk, seg_ones) + 1e-6)
            r_s[rb, cs] = r
            k_s[rb, cs] = k * (1.0 + (a_s[rb, cs] - 1.0) * ka_ref[:, cs])
            v_s[rb, cs] = v

    def chunk(c, carry):
        pairs = range(NB * NP)
        rws = [pl.ds(pl.multiple_of((q // NP) * TB + c * C, C), C) for q in pairs]
        css = [slice((q % NP) * LANES, (q % NP + 1) * LANES) for q in pairs]
        gam_last, ar, bk, bk_end, v_f, v_t = [], [], [], [], [], []
        for rows, cs in zip(rws, css):
            cg = cg_s[rows, cs]
            gam = jnp.exp(cg)
            gam_inv = jnp.exp(-cg)
            gam_prev = jnp.exp(cg - lw_s[rows, cs])
            gl_ = gam[C - 1:C, :]
            kn = kn_s[rows, cs]
            b_raw = kn * a_s[rows, cs] * gam_inv
            k_raw = k_s[rows, cs] * gam_inv
            gam_last.append(gl_)
            ar.append(jnp.concatenate([_bd(-kn * gam_prev, lo), _bd(r_s[rows, cs] * gam, lo)],
                                      axis=0).astype(BF16))
            bk.append(jnp.concatenate([_bd(b_raw, lo), _bd(k_raw, lo)], axis=0).astype(BF16))
            bk_end.append(jnp.concatenate([_bd(b_raw * gl_, lo), _bd(k_raw * gl_, lo)],
                                          axis=0).astype(BF16))
            vf = _bd(v_s[rows, cs], lo)
            v_f.append(vf)
            v_t.append(vf.astype(BF16))
        score = [_dot_nt(ar[p], bk[p]) for p in pairs]
        a_ab = [jnp.where(strict, s[:LANES, :LANES], 0.0) for s in score]
        a_akv = [_dot(jnp.where(strict, score[p][:LANES, LANES:], 0.0).astype(BF16), v_t[p])
                 for p in pairs]
        r_abk = [jnp.concatenate([jnp.where(incl, s[LANES:, :LANES], 0.0),
                                  jnp.where(incl, s[LANES:, LANES:], 0.0)], axis=1).astype(BF16)
                 for s in score]
        t_inv = [t.astype(BF16) for t in _unit_lower_inverses(a_ab, eye, 5, 1)]
        s_old = [state[p] for p in pairs]
        sproj = [_dot_nt(ar[p], s_old[p].astype(BF16)) for p in pairs]
        u = [_dot(t_inv[p], (sproj[p][:LANES] + a_akv[p]).astype(BF16)) for p in pairs]
        y_bd = [sproj[p][LANES:] + _dot(r_abk[p], jnp.concatenate([u[p].astype(BF16), v_t[p]], axis=0))
                for p in pairs]
        for p in pairs:
            y_s[rws[p], css[p]] = y_bd[p][:C] + y_bd[p][C:]
        uvT = [jnp.concatenate([u[p].T, v_f[p].T], axis=1).astype(BF16) for p in pairs]
        for p in pairs:
            state[p] = s_old[p] * gam_last[p] + _dot(uvT[p], bk_end[p])
        return carry

    lax.fori_loop(0, TB // C, chunk, 0)

    for bi in range(NB):
        rb = slice(bi * TB, (bi + 1) * TB)
        for p in range(NP):
            cs = slice(p * LANES, (p + 1) * LANES)
            y = y_s[rb, cs]
            mean = _seg_dot(y, seg_ones) * (1.0 / RWKV_N)
            d = y - mean
            var = _seg_dot(d * d, seg_ones) * (1.0 / RWKV_N)
            yn = d * lax.rsqrt(var + RWKV_GN_EPS) * gnw_ref[:, cs] + gnb_ref[:, cs]
            bonus = _seg_dot(r_s[rb, cs] * k_s[rb, cs] * rk_ref[:, cs], seg_ones) * v_s[rb, cs]
            o_ref[bi, :, cs] = ((yn + bonus) * g_s[rb, cs]).astype(o_ref.dtype)


def _rwkv(z, B, L, mu, w0, w_up, a0, a_up, g_up, k_k, k_a, r_k, gn_w, gn_b):
    TB = RWKV_TB
    nb = L // TB
    W = RWKV_W
    mu_r, mu_k, mu_v, mu_wl, mu_al, mu_gl = jnp.split(
        mu, np.cumsum([W, W, W, RWKV_LORA, RWKV_LORA])[:].tolist())
    pad = LANES - RWKV_LORA
    mu_rkv = jnp.concatenate([mu_r, mu_k, mu_v]).reshape(1, 3 * W)
    mu_wa = jnp.concatenate([jnp.pad(mu_wl, (0, pad)), jnp.pad(mu_al, (0, pad))]).reshape(1, 2 * LANES)
    wup = jnp.pad(w_up, ((0, pad), (0, 0))).astype(BF16)
    aup = jnp.pad(a_up, ((0, pad), (0, 0))).astype(BF16)
    row = lambda t: t.reshape(1, W)
    vec = lambda n: pl.BlockSpec((1, n), lambda i: (0, 0))
    big = lambda: pltpu.VMEM((B * TB, W), F32)
    z3 = z.reshape(B, L, AB_COLS)
    out = pl.pallas_call(
        _rwkv_body,
        grid=(nb,),
        in_specs=[pl.BlockSpec((B, TB, 3 * W), lambda i: (0, i, 0)),
                  pl.BlockSpec((B, TB, RWKV_GATE), lambda i: (0, i, AB_GL // RWKV_GATE)),
                  pl.BlockSpec((B, TB, 2 * LANES), lambda i: (0, i, AB_WL // (2 * LANES))),
                  vec(3 * W), vec(RWKV_GATE), vec(2 * LANES), vec(W),
                  pl.BlockSpec((LANES, W), lambda i: (0, 0)),
                  vec(W),
                  pl.BlockSpec((LANES, W), lambda i: (0, 0)),
                  pl.BlockSpec((RWKV_GATE, W), lambda i: (0, 0)),
                  vec(W), vec(W), vec(W), vec(W), vec(W)],
        out_specs=pl.BlockSpec((B, TB, W), lambda i: (0, i, 0)),
        out_shape=jax.ShapeDtypeStruct((B, L, W), BF16),
        scratch_shapes=[pltpu.VMEM((B, 3 * W), F32), pltpu.VMEM((B, RWKV_GATE), F32),
                        pltpu.VMEM((B, 2 * LANES), F32),
                        pltpu.VMEM((B * W // LANES, LANES, LANES), F32),
                        big(), big(), big(), big(), big(), big(), big(), big(), big()],
        compiler_params=_cp(("arbitrary",), 56),
        name="rwkv7",
    )(z3, z3, z3, mu_rkv, mu_gl.reshape(1, RWKV_GATE), mu_wa, row(w0), wup, row(a0), aup,
      g_up.astype(BF16), row(k_k), row(k_a), row(r_k), row(gn_w), row(gn_b))
    return out.reshape(B * L, W)


def _t5_bucket_np(dist):
    n = np.maximum(dist, 0)
    exact = REL_BUCKETS // 2
    ratio = np.log(np.maximum(n, 1).astype(np.float32) / np.float32(exact)) / np.float32(
        math.log(REL_MAX_DIST / exact))
    large = exact + (ratio.astype(np.float32) * np.float32(REL_BUCKETS - exact)).astype(np.int32)
    return np.where(n < exact, n, np.minimum(large, REL_BUCKETS - 1)).astype(np.int32)


def _near_buckets():
    kl = np.arange(QB)[:, None]
    ql = np.arange(QB)[None, :]
    return np.stack([_t5_bucket_np(ql - kl), _t5_bucket_np(QB + ql - kl)])


def _dsa_body(tbl_ref, bkt_ref, zq_ref, zqi_ref, zc_ref, zki_ref, zwi_ref, cn_ref, wukT_ref,
              wuvT_ref, o_ref, c_all, cT_all, kibd_all, sc, qiT, qlatT, bias, m_s, l_s, alpha_s,
              oT, *, topk):
    b = pl.program_id(0)
    qb = pl.program_id(1)
    lo = _iota((1, LANES), 1) < IDX_D
    krow = _iota((QB, QB), 0)
    qcol = _iota((QB, QB), 1)
    hsl = [slice(h * QB, (h + 1) * QB) for h in range(DSA_H)]

    @pl.when((b == 0) & (qb == 0))
    def _():
        for t in range(2):
            bk = bkt_ref[t]
            for h in range(DSA_H):
                far = tbl_ref[REL_BUCKETS - 1, h]
                acc = jnp.zeros((QB, QB), F32)
                for bb in range(REL_BUCKETS - 1):
                    acc = jnp.where(bk == bb, tbl_ref[bb, h] - far, acc)
                bias[t, h] = acc

    c_new = _rms(zc_ref[...], cn_ref[...])
    c_all[qb] = c_new.astype(BF16)
    cT_all[qb] = c_new.T.astype(BF16)
    kibd_all[qb] = _bd(zki_ref[...], lo).astype(BF16)

    for p in range(IDX_H // 2):
        qiT[:, p * QB:(p + 1) * QB] = zqi_ref[:, p * LANES:(p + 1) * LANES].T.astype(BF16)
    wT = zwi_ref[...].T * (IDX_D ** -0.5 * IDX_H ** -0.5)
    w_rows = [wT[h:h + 1, :] for h in range(IDX_H)]
    for h in range(DSA_H):
        qhT = zq_ref[:, hsl[h]].T.astype(BF16)
        qlatT[:, hsl[h]] = (_dot(wukT_ref[h], qhT) * DSA_D ** -0.5).astype(BF16)

    def score_blocks(j, nk):
        kb = kibd_all[pl.ds(j, nk)].reshape(nk * 2 * QB, LANES)
        acc = [jnp.zeros((QB, QB), F32) for _ in range(nk)]
        for pp in range(IDX_H // 4):
            s = _dot(kb, qiT[:, 2 * pp * QB:2 * (pp + 1) * QB])
            for k in range(nk):
                even = s[2 * k * QB:(2 * k + 1) * QB]
                odd = s[(2 * k + 1) * QB:(2 * k + 2) * QB]
                acc[k] = (acc[k] + w_rows[4 * pp] * jnp.maximum(even[:, :QB], 0.0)
                          + w_rows[4 * pp + 1] * jnp.maximum(odd[:, :QB], 0.0)
                          + w_rows[4 * pp + 2] * jnp.maximum(even[:, QB:], 0.0)
                          + w_rows[4 * pp + 3] * jnp.maximum(odd[:, QB:], 0.0))
        for k in range(nk):
            bits = lax.bitcast_convert_type(acc[k], I32)
            key = bits ^ ((bits >> 31) & 0x7FFFFFFF)
            sc[j + k] = jnp.where((j + k == qb) & (krow > qcol), INT_MIN, key)

    def score_group(jj, carry):
        score_blocks(jj * SCORE_GROUP, SCORE_GROUP)
        return carry

    def score_single(j, carry):
        score_blocks(j, 1)
        return carry

    n_sgroup = (qb + 1) // SCORE_GROUP
    lax.fori_loop(0, n_sgroup, score_group, 0)
    s_done = n_sgroup * SCORE_GROUP
    s_half = (qb + 1 - s_done) // (SCORE_GROUP // 2)

    @pl.when(s_half == 1)
    def _():
        score_blocks(s_done, SCORE_GROUP // 2)

    lax.fori_loop(s_done + s_half * (SCORE_GROUP // 2), qb + 1, score_single, 0)

    n_quad = (qb + 1) // KEY_GROUP

    def count(pred):
        def quad(jj, a):
            blk = sc[pl.ds(jj * KEY_GROUP, KEY_GROUP)]
            for k in range(KEY_GROUP):
                a = a + jnp.where(pred(blk[k]), 1, 0)
            return a

        def single(j, a):
            return a + jnp.where(pred(sc[j]), 1, 0)

        a = lax.fori_loop(0, n_quad, quad, jnp.zeros((QB, QB), I32))
        a = lax.fori_loop(n_quad * KEY_GROUP, qb + 1, single, a)
        return jnp.sum(a, axis=0, keepdims=True)

    def bis_body(i, carry):
        t, n_t = carry
        cand = t ^ jnp.left_shift(jnp.int32(1), 31 - i)
        tot = count(lambda key: key >= cand)
        ok = tot >= topk
        return jnp.where(ok, cand, t), jnp.where(ok, tot, n_t)

    thr_raw, n_ge = lax.fori_loop(
        0, 32, bis_body,
        (jnp.full((1, QB), INT_MIN, I32), jnp.broadcast_to((qb + 1) * QB, (1, QB)).astype(I32)))
    thr = jnp.maximum(thr_raw, INT_MIN + 1)

    @pl.when(jnp.max(n_ge) > topk)
    def _():
        keep = (topk - count(lambda key: key > thr_raw)).astype(F32)
        tri = jnp.where(qcol <= krow, 1.0, 0.0).astype(BF16)

        def drop(j, seen):
            key = sc[j]
            tied = key == thr_raw
            tied_bf = jnp.where(tied, 1.0, 0.0).astype(BF16)
            rank = seen + _dot(tri, tied_bf)
            sc[j] = jnp.where(tied & (rank > keep), INT_MIN, key)
            return seen + jnp.sum(tied_bf.astype(F32), axis=0, keepdims=True)

        lax.fori_loop(0, qb + 1, drop, jnp.zeros((1, QB), F32))

    m_s[...] = jnp.full(m_s.shape, NEG, F32)
    l_s[...] = jnp.zeros_like(l_s)
    oT[...] = jnp.zeros_like(oT)

    def attend(j, nk, near):
        sel = sc[pl.ds(j, nk)].reshape(nk * QB, QB) >= thr
        lg = _dot(c_all[pl.ds(j, nk)].reshape(nk * QB, DSA_C), qlatT[...])
        cT = jnp.concatenate([cT_all[j + k] for k in range(nk)], axis=1)
        prs = []
        for h in range(DSA_H):
            lgh = lg[:, hsl[h]]
            if near is not None:
                lgh = lgh + bias[near, h]
            lgh = jnp.where(sel, lgh, NEG)
            m_old = m_s[:, hsl[h]]
            m_new = jnp.maximum(m_old, jnp.max(lgh, axis=0, keepdims=True))
            pr = jnp.exp(lgh - m_new)
            alpha_s[:, hsl[h]] = jnp.exp(m_old - m_new)
            l_s[:, hsl[h]] = alpha_s[:, hsl[h]] * l_s[:, hsl[h]] + jnp.sum(pr, axis=0, keepdims=True)
            m_s[:, hsl[h]] = m_new
            prs.append(pr.astype(BF16))
        oT[...] = alpha_s[...] * oT[...] + _dot(cT, jnp.concatenate(prs, axis=1))

    def far_group(jj, carry):
        attend(jj * ATTEND_GROUP, ATTEND_GROUP, None)
        return carry

    def far_single(j, carry):
        attend(j, 1, None)
        return carry

    n_far = jnp.maximum(qb - 1, 0)
    n_group = n_far // ATTEND_GROUP
    lax.fori_loop(0, n_group, far_group, 0)
    done = n_group * ATTEND_GROUP
    half = (n_far - done) // (ATTEND_GROUP // 2)

    @pl.when(half == 1)
    def _():
        attend(done, ATTEND_GROUP // 2, None)

    lax.fori_loop(done + half * (ATTEND_GROUP // 2), n_far, far_single, 0)

    @pl.when(qb >= 1)
    def _():
        attend(qb - 1, 1, 1)

    attend(qb, 1, 0)

    inv_l = 1.0 / l_s[...]
    for h in range(DSA_H):
        oh = (oT[:, hsl[h]] * inv_l[:, hsl[h]]).astype(BF16)
        o_ref[:, hsl[h]] = _dot(wuvT_ref[h], oh).T.astype(o_ref.dtype)


def _dsa(z, B, L, ckv_norm, w_uk, w_uv, rel_bias):
    nq = L // QB
    topk = min(TOPK_MAX, L // 4)
    wukT = jnp.swapaxes(w_uk, 1, 2).astype(BF16)
    wuvT = jnp.swapaxes(w_uv, 1, 2).astype(BF16)
    blk = lambda w, off: pl.BlockSpec((QB, w), lambda b, q: (b * nq + q, off // w))
    full = lambda shape: pl.BlockSpec(shape, lambda b, q: (0,) * len(shape))
    W = DSA_H * DSA_D
    return pl.pallas_call(
        functools.partial(_dsa_body, topk=topk),
        grid=(B, nq),
        in_specs=[pl.BlockSpec(memory_space=pltpu.SMEM),
                  full((2, QB, QB)),
                  blk(W, AB_ZQ), blk(IDX_H * IDX_D, AB_ZQI), blk(DSA_C, AB_ZC),
                  blk(LANES, AB_ZKI), blk(LANES, AB_ZWI),
                  full((1, DSA_C)), full((DSA_H, DSA_C, DSA_D)), full((DSA_H, DSA_D, DSA_C))],
        out_specs=pl.BlockSpec((QB, W), lambda b, q: (b * nq + q, 0)),
        out_shape=jax.ShapeDtypeStruct((B * L, W), BF16),
        scratch_shapes=[pltpu.VMEM((nq, QB, DSA_C), BF16), pltpu.VMEM((nq, DSA_C, QB), BF16),
                        pltpu.VMEM((nq, 2 * QB, LANES), BF16), pltpu.VMEM((nq, QB, QB), I32),
                        pltpu.VMEM((LANES, IDX_H // 2 * QB), BF16), pltpu.VMEM((DSA_C, W), BF16),
                        pltpu.VMEM((2, DSA_H, QB, QB), F32),
                        pltpu.VMEM((1, W), F32), pltpu.VMEM((1, W), F32), pltpu.VMEM((1, W), F32),
                        pltpu.VMEM((DSA_C, W), F32)],
        compiler_params=_cp(("arbitrary", "arbitrary"), 48),
        name="dsa",
    )(rel_bias, jnp.asarray(_near_buckets()), z, z, z, z, z, ckv_norm.reshape(1, DSA_C), wukT, wuvT)


def _gdn_gates_body(z_ref, alog_ref, dtb_ref, o_ref):
    TB = z_ref.shape[0]
    C = GDN_CHUNK
    z = z_ref[...]
    sub = _iota((1, LANES), 1) % 8
    beta = _sigmoid(z)
    g = -jnp.exp(alog_ref[...]) * _softplus(z + dtb_ref[...])
    rt = _iota((TB, TB), 0)
    ct = _iota((TB, TB), 1)
    same = (rt // C) == (ct // C)
    g_hi, g_rest = _split(g)
    g_mid, g_lo = _split(g - g_hi.astype(F32))
    tri = jnp.where(same & (ct <= rt), 1.0, 0.0).astype(BF16)
    blk = jnp.where(same, 1.0, 0.0).astype(BF16)
    cum = _dot(tri, g_hi) + _dot(tri, g_mid) + _dot(tri, g_lo)
    tot = _dot(blk, g_hi) + _dot(blk, g_mid) + _dot(blk, g_lo)
    tile =jnp.where(sub < 2, beta, jnp.where(sub < 4, g, jnp.where(sub < 6, cum, tot)))
    o_ref[...] = tile.T


def _gdn_gates(z, B, L, a_log, dt_bias):
    TB = GDN_GATES_TB
    nb = L // TB
    spread =lambda t: jnp.zeros((GDN_QK_H, 8), F32).at[:, 2:].set(
        jnp.tile(t.reshape(GDN_QK_H, 2), (1, 3))).reshape(1, LANES)
    return pl.pallas_call(
        _gdn_gates_body,
        grid=(B, nb),
        in_specs=[pl.BlockSpec((TB, LANES), lambda b, i: (b * nb + i, GD_GATES // LANES)),
                  pl.BlockSpec((1, LANES), lambda b, i: (0, 0)),
                  pl.BlockSpec((1, LANES), lambda b, i: (0, 0))],
        out_specs=pl.BlockSpec((None, LANES, TB), lambda b, i: (b, 0, i)),
        out_shape=jax.ShapeDtypeStruct((B, LANES, L), F32),
        compiler_params=_cp(("parallel", "parallel"), 32),
        name="gdn_gates",
    )(z, spread(a_log), spread(dt_bias))


def _gdn_body(zq_ref, zk_ref, zv_ref, zg_ref, gates_ref, cwq_ref, cwk_ref, cwv_ref, on_ref, o_ref,
              xbuf, state, q_s, k_s, v_s, *, hg):
    TB = zq_ref.shape[0]
    C = GDN_CHUNK
    D = GDN_D
    KW = GDN_CONV_W
    W = 4 * D * hg

    @pl.when(pl.program_id(2) == 0)
    def _():
        xbuf[0:8, :] = jnp.zeros((8, W), F32)
        state[...] = jnp.zeros_like(state)

    xbuf[8:TB + 8, 0:D * hg] = zq_ref[...]
    xbuf[8:TB + 8, D * hg:2 * D * hg] = zk_ref[...]
    xbuf[8:TB + 8, 2 * D * hg:W] = zv_ref[...]
    cw = jnp.concatenate([cwq_ref[...], cwk_ref[...], cwv_ref[...]], axis=1)
    l2 = lambda t: t * lax.rsqrt(jnp.sum(t * t, axis=-1, keepdims=True) + 1e-6)
    for g in range(W // D):
        cols = slice(g * D, (g + 1) * D)
        acc = jnp.zeros((TB, D), F32)
        for j in range(KW):
            acc = acc + cw[j:j + 1, cols] * xbuf[8 - (KW - 1) + j:8 - (KW - 1) + j + TB, cols]
        act = _silu(acc)
        if g < hg:
            q_s[:, cols] = l2(act) * D ** -0.5
        elif g < 2 * hg:
            k_s[:, (g - hg) * D:(g - hg + 1) * D] = l2(act)
        else:
            v_s[:, (g - 2 * hg) * D:(g - 2 * hg + 1) * D] = act
    xbuf[0:8, :] = xbuf[TB:TB + 8, :]

    lane = _iota((1, LANES), 1)
    lo = lane < C
    r128 = _iota((LANES, LANES), 0)
    c128 = _iota((LANES, LANES), 1)
    same = (r128 // C) == (c128 // C)
    eye_m = r128 == c128
    eye = jnp.where(eye_m, 1.0, 0.0).astype(F32)
    strict = same & (c128 < r128)
    incl = same & (c128 <= r128)

    heads = range(hg)
    for c in range(TB // C):
        rows = slice(c * C, (c + 1) * C)
        e_st, e_neg, fb_st, etot_cat, kc, qc, dec = [], [], [], [], [], [], []
        for h in heads:
            win = gates_ref[8 * h:8 * h + 8, (c // 2) * LANES:(c // 2 + 1) * LANES]
            win_sw = pltpu.roll(win, C, axis=1)
            first, second = (win, win_sw) if c % 2 == 0 else (win_sw, win)
            st = lambda r: jnp.where(lo, first[r:r + 1, :], second[r + 1:r + 2, :])
            beta, cum, tot = st(0), st(4), st(6)
            etot = jnp.exp(tot)
            etot_sw = pltpu.roll(etot, C, axis=1)
            e_st.append(jnp.exp(cum))
            e_neg.append(-jnp.exp(cum))
            fb_st.append(jnp.exp(tot - cum) * beta)
            etot_cat.append(jnp.concatenate([jnp.where(lo, etot, etot_sw),
                                             jnp.where(lo, etot_sw, etot)], axis=1))
            kc.append(k_s[rows, h * D:(h + 1) * D])
            qc.append(q_s[rows, h * D:(h + 1) * D])
            cum_b = jnp.broadcast_to(cum, (LANES, LANES))
            dec.append(jnp.exp(jnp.where(incl, cum_b.T - cum_b, NEG)) * beta)
        score = [_dot_nt(jnp.concatenate([kc[h], kc[h], qc[h], qc[h]], axis=0).astype(BF16),
                         jnp.concatenate([kc[h], kc[h]], axis=0).astype(BF16))
                 for h in heads]
        t_inv = _unit_lower_inverses(
            [jnp.where(strict, score[h][:LANES] * dec[h], 0.0) for h in heads], eye, 5, -1)
        t_cat = [jnp.concatenate([t_inv[h], t_inv[h] * e_neg[h]], axis=1).astype(BF16) for h in heads]
        a_cat = [jnp.concatenate([eye * e_st[h], score[h][LANES:] * dec[h]], axis=1).astype(BF16)
                 for h in heads]
        s_old = [state[h] for h in heads]
        proj = [_dot(jnp.concatenate([kc[h], qc[h]], axis=0).astype(BF16), s_old[h].astype(BF16))
                for h in heads]
        rhs = [jnp.concatenate([v_s[rows, 2 * h * D:(2 * h + 1) * D],
                                v_s[rows, (2 * h + 1) * D:(2 * h + 2) * D],
                                proj[h][:C, :D], proj[h][:C, D:]], axis=0).astype(BF16) for h in heads]
        vn = [_dot(t_cat[h], rhs[h]) for h in heads]
        vn_bf = [x.astype(BF16) for x in vn]
        o_st = [_dot(a_cat[h], jnp.concatenate(
            [proj[h][C:, :D].astype(BF16), proj[h][C:, D:].astype(BF16), vn_bf[h]], axis=0))
            for h in heads]
        fv = [_dot((eye * fb_st[h]).astype(BF16), vn_bf[h]) for h in heads]
        for h in heads:
            state[h] = s_old[h] * etot_cat[h] + _dot(
                kc[h].T.astype(BF16), jnp.concatenate([fv[h][:C], fv[h][C:]], axis=1).astype(BF16))
        for h in heads:
            for u in range(2):
                cols = slice((2 * h + u) * D, (2 * h + u + 1) * D)
                oh = o_st[h][u * C:(u + 1) * C]
                o_ref[rows, cols] = (_rms(oh, on_ref[...]) * _silu(zg_ref[rows, cols])).astype(o_ref.dtype)


def _gdn(z, gates, B, L, conv_w, out_norm):
    TB = GDN_TB
    nb = L // TB
    D = GDN_D
    hg = GDN_HEADS_PER_STEP
    zspec = lambda w, off: pl.BlockSpec((TB, w), lambda b, h, i: (b * nb + i, off // w + h))
    cspec = lambda w, off: pl.BlockSpec((GDN_CONV_W, w), lambda b, h, i: (0, off // w + h))
    return pl.pallas_call(
        functools.partial(_gdn_body, hg=hg),
        grid=(B, GDN_QK_H // hg, nb),
        in_specs=[zspec(D * hg, GD_Q), zspec(D * hg, GD_K), zspec(2 * D * hg, GD_V),
                  zspec(2 * D * hg, GD_ZG),
                  pl.BlockSpec((None, 8 * hg, TB), lambda b, h, i: (b, h, i)),
                  cspec(D * hg, GD_Q), cspec(D * hg, GD_K), cspec(2 * D * hg, GD_V),
                  pl.BlockSpec((1, D), lambda b, h, i: (0, 0))],
        out_specs=pl.BlockSpec((TB, 2 * D * hg), lambda b, h, i: (b * nb + i, h)),
        out_shape=jax.ShapeDtypeStruct((B * L, GDN_VW), BF16),
        scratch_shapes=[pltpu.VMEM((TB + 8, 4 * D * hg), F32), pltpu.VMEM((hg, D, 2 * D), F32),
                        pltpu.VMEM((TB, D * hg), F32), pltpu.VMEM((TB, D * hg), F32),
                        pltpu.VMEM((TB, 2 * D * hg), F32)],
        compiler_params=_cp(("parallel", "parallel", "arbitrary"), 40),
        name="gdn",
    )(z, z, z, z, gates, conv_w, conv_w, conv_w, out_norm.reshape(1, D))


def _gd_in_weight(w_in):
    w_in = w_in.astype(BF16)
    qkv, zg, b, a = jnp.split(w_in, np.cumsum([2 * GDN_KW + GDN_VW, GDN_VW, GDN_V_H]).tolist(), axis=1)
    D = w_in.shape[0]
    pair = lambda t: t.reshape(D, GDN_QK_H, 2)
    gates = jnp.concatenate([pair(b), pair(a), pair(a), pair(a)], axis=2).reshape(D, LANES)
    pad = jnp.zeros((D, GD_COLS - GD_GATES - LANES), w_in.dtype)
    return jnp.concatenate([qkv, zg, gates, pad], axis=1)


def _router_body(x_ref, g_ref, wr_ref, o_ref):
    x_hi, x_lo = _split(_rms(x_ref[...], g_ref[...]))
    w_hi, w_lo = _split(wr_ref[...])
    logits = _dot(x_hi, w_hi) + _dot(x_lo, w_hi) + _dot(x_hi, w_lo)
    lane = _iota(logits.shape, 1)
    logits = jnp.where(lane < N_EXPERTS, logits, -jnp.inf)
    m1 = jnp.max(logits, axis=-1, keepdims=True)
    i1 = jnp.min(jnp.where(logits == m1, lane, LANES), axis=-1, keepdims=True)
    rest = jnp.where(lane == i1, -jnp.inf, logits)
    m2 = jnp.max(rest, axis=-1, keepdims=True)
    i2 = jnp.min(jnp.where(rest == m2, lane, LANES), axis=-1, keepdims=True)
    e = jnp.exp(m2 - m1)
    w1 = 1.0 / (1.0 + e)
    o_ref[...] = jnp.where(lane == 0, i1.astype(F32),
                           jnp.where(lane == 1, i2.astype(F32),
                                     jnp.where(lane == 2, w1, jnp.where(lane == 3, e * w1, 0.0))))


def _router(x, g, w_router, *, tm):
    T, D = x.shape
    wr = jnp.pad(w_router, ((0, 0), (0, LANES - N_EXPERTS)))
    return pl.pallas_call(
        _router_body,
        grid=(T // tm,),
        in_specs=[pl.BlockSpec((tm, D), lambda i: (i, 0)),
                  pl.BlockSpec((1, D), lambda i: (0, 0)),
                  pl.BlockSpec((D, LANES), lambda i: (0, 0))],
        out_specs=pl.BlockSpec((tm, LANES), lambda i: (i, 0)),
        out_shape=jax.ShapeDtypeStruct((T, LANES), F32),
        compiler_params=_cp(("parallel",), 32),
        name="router",
    )(x, g.reshape(1, D), wr)


def _row_copy(src_hbm, row, dst, r, sem):
    return pltpu.make_async_copy(src_hbm.at[pl.ds(row, 1), :], dst.at[pl.ds(r, 1), :], sem)


def _experts_body(tok_ref, be_ref, nu_ref, x_hbm, g_ref, wg_ref, wu_ref, wd_ref, o_ref,
                  xbuf, xn_ref, sem):
    i = pl.program_id(0)
    f = pl.program_id(1)
    MB = xbuf.shape[1]
    active = i < nu_ref[0]
    slot = i % 2

    def gather(blk, s):
        def issue(h, c):
            for pr in range(2):
                r = 2 * h + pr
                _row_copy(x_hbm, tok_ref[blk * MB + r], xbuf.at[s], r, sem.at[s]).start(priority=pr)
            return c

        lax.fori_loop(0, MB // 2, issue, 0, unroll=4)

    @pl.when(f == 0)
    def _():
        o_ref[...] = jnp.zeros_like(o_ref)

    @pl.when((f == 0) & (i == 0) & active)
    def _():
        gather(0, 0)

    @pl.when((f == 0) & active)
    def _():
        pltpu.make_async_copy(x_hbm.at[pl.ds(0, MB), :], xbuf.at[slot], sem.at[slot]).wait()
        xn_ref[...] = _rms(xbuf[slot], g_ref[...]).astype(BF16)

    @pl.when((f == 1) & (i + 1 < nu_ref[0]))
    def _():
        gather(i + 1, 1 - slot)

    @pl.when(active)
    def _():
        xn = xn_ref[...]
        h = (_silu(_dot(xn, wg_ref[...])) * _dot(xn, wu_ref[...])).astype(BF16)
        o_ref[...] += _dot(h, wd_ref[...])


def _experts(x, g, tok, blk_e, n_used, wg, wu, wd, *, tf):
    T, D = x.shape
    MB = MOE_BLOCK
    n_blk = tok.shape[0] // MB
    Fh = wg.shape[2]
    assert Fh // tf >= 2, "the next block's rows are requested during hidden chunk 1"
    fe = lambda i, f, nu: jnp.where(i < nu[0], f, 0)
    return pl.pallas_call(
        _experts_body,
        grid_spec=pltpu.PrefetchScalarGridSpec(
            num_scalar_prefetch=3,
            grid=(n_blk, Fh // tf),
            in_specs=[pl.BlockSpec(memory_space=pl.ANY),
                      pl.BlockSpec((1, D), lambda i, f, tk, be, nu: (0, 0)),
                      pl.BlockSpec((None, D, tf), lambda i, f, tk, be, nu: (be[i], 0, fe(i, f, nu))),
                      pl.BlockSpec((None, D, tf), lambda i, f, tk, be, nu: (be[i], 0, fe(i, f, nu))),
                      pl.BlockSpec((None, tf, D), lambda i, f, tk, be, nu: (be[i], fe(i, f, nu), 0))],
            out_specs=pl.BlockSpec((MB, D), lambda i, f, tk, be, nu: (i, 0)),
            scratch_shapes=[pltpu.VMEM((2, MB, D), F32), pltpu.VMEM((MB, D), BF16),
                            pltpu.SemaphoreType.DMA((2,))]),
        out_shape=jax.ShapeDtypeStruct((n_blk * MB, D), F32),
        compiler_params=_cp(("arbitrary", "arbitrary"), 52),
        name="experts",
    )(tok, blk_e, n_used, x, g.reshape(1, D), wg, wu, wd)


def _combine_ple_body(slot_ref, y_hbm, x_ref, r_ref, g_ref, wg_ref, p_ref, wp_ref, fg_ref, o_ref,
                      y0, y1, sem, *, final):
    i = pl.program_id(0)
    tm = x_ref.shape[0]
    slot = i % 2

    def gather(blk, s):
        def issue(r, c):
            a = (blk * tm + r) * TOP_K
            _row_copy(y_hbm, slot_ref[a], y0.at[s], r, sem.at[s]).start(priority=0)
            _row_copy(y_hbm, slot_ref[a + 1], y1.at[s], r, sem.at[s]).start(priority=1)
            return c

        lax.fori_loop(0, tm, issue, 0, unroll=8)

    @pl.when(i == 0)
    def _():
        gather(0, 0)

    pltpu.make_async_copy(y_hbm.at[pl.ds(0, tm), :], y0.at[slot], sem.at[slot]).wait()
    pltpu.make_async_copy(y_hbm.at[pl.ds(0, tm), :], y1.at[slot], sem.at[slot]).wait()

    @pl.when(i + 1 < pl.num_programs(0))
    def _():
        gather(i + 1, 1 - slot)

    route = r_ref[...]
    x = x_ref[...] + y0[slot] * route[:, 2:3] + y1[slot] * route[:, 3:4]
    gate = _sigmoid(_dot(_rms(x, g_ref[...]).astype(BF16), wg_ref[...]))
    y = x + gate * _dot(p_ref[...].astype(BF16), wp_ref[...])
    if final:
        y = _rms(y, fg_ref[...])
    o_ref[...] = y


def _combine_ple(x, y, route, slots, g, wg, p, wp, fg, *, final, tm):
    T, D = x.shape
    P = p.shape[1]
    const = lambda shape: pl.BlockSpec(shape, lambda i, s: (0, 0))
    return pl.pallas_call(
        functools.partial(_combine_ple_body, final=final),
        grid_spec=pltpu.PrefetchScalarGridSpec(
            num_scalar_prefetch=1,
            grid=(T // tm,),
            in_specs=[pl.BlockSpec(memory_space=pl.ANY),
                      pl.BlockSpec((tm, D), lambda i, s: (i, 0)),
                      pl.BlockSpec((tm, LANES), lambda i, s: (i, 0)),
                      const((1, D)), const((D, D)),
                      pl.BlockSpec((tm, P), lambda i, s: (i, 0)),
                      const((P, D)), const((1, D))],
            out_specs=pl.BlockSpec((tm, D), lambda i, s: (i, 0)),
            scratch_shapes=[pltpu.VMEM((2, tm, D), F32), pltpu.VMEM((2, tm, D), F32),
                            pltpu.SemaphoreType.DMA((2,))]),
        out_shape=jax.ShapeDtypeStruct((T, D), F32),
        compiler_params=_cp(("arbitrary",), 48),
        name="combine_ple",
    )(slots, y, x, route, g.reshape(1, D), wg, p, wp, fg.reshape(1, D))


def _moe(x, g, w_router, wg, wu, wd):
    T, D = x.shape
    MB = MOE_BLOCK
    A = T * TOP_K
    route = _router(x, g, w_router, tm=512)
    flat_e = route[:, :TOP_K].astype(I32).reshape(A)
    onehot = (flat_e[:, None] == jnp.arange(N_EXPERTS, dtype=I32)[None, :]).astype(I32)
    csum = jnp.cumsum(onehot, axis=0)
    rank = jnp.take_along_axis(csum, flat_e[:, None], axis=1)[:, 0] - 1
    padded = (csum[-1] + MB - 1) // MB * MB
    pend = jnp.cumsum(padded)
    slots = (pend - padded)[flat_e] + rank
    n_blk = A // MB + N_EXPERTS
    tok = jnp.zeros((n_blk * MB,), I32).at[slots].set(jnp.arange(A, dtype=I32) // TOP_K)
    blk_start = jnp.arange(n_blk, dtype=I32) * MB
    blk_e = jnp.minimum(jnp.sum(blk_start[:, None] >= pend[None, :], axis=1), N_EXPERTS - 1).astype(I32)
    n_used = (pend[-1:] // MB).astype(I32)
    y = _experts(x, g, tok, blk_e, n_used, wg, wu, wd, tf=min(1024, wg.shape[2]))
    return y, route, slots.astype(I32)


def _ab_in_weight(w_in):
    w_in = w_in.astype(BF16)
    r, k, v, wl, al, gl, zq, zc, zqi, zki, zwi = jnp.split(
        w_in, np.cumsum([1024, 1024, 1024, 96, 96, 256, 1024, 256, 1024, 64]).tolist(), axis=1)
    D = w_in.shape[0]
    z = lambda n: jnp.zeros((D, n), w_in.dtype)
    cols = [r, k, v, zq, zqi, gl, zc, wl, z(32), al, z(32), zki, zki, zwi, z(AB_COLS - AB_ZWI - 16)]
    return jnp.concatenate(cols, axis=1)


def kernel(x, p, norm_mix, norm_ffn, ab_w_in, ab_mu, rwkv_w0, rwkv_w_up, rwkv_a0, rwkv_a_up, rwkv_g_up, rwkv_k_k, rwkv_k_a, rwkv_r_k, rwkv_gn_w, rwkv_gn_b, dsa_ckv_norm, dsa_w_uk, dsa_w_uv, ab_w_out, rel_bias, ffn_w_gate, ffn_w_up, ffn_w_down, gdn_w_in, gdn_conv, gdn_a_log, gdn_dt_bias, gdn_out_norm, gdn_w_out, moe_router, moe_w_gate, moe_w_up, moe_w_down, ple_norm, ple_w_gate, ple_w_proj, final_norm):
    B, L, D = x.shape
    T = B * L
    xf = x.reshape(T, D)
    bf = lambda w: w.astype(BF16)

    z = _norm_mm(xf, norm_mix[0], _ab_in_weight(ab_w_in[0]), tm=1024, tn=1024)
    y_a = _rwkv(z, B, L, ab_mu[0], rwkv_w0[0], rwkv_w_up[0], rwkv_a0[0], rwkv_a_up[0], rwkv_g_up[0],
                rwkv_k_k[0], rwkv_k_a[0], rwkv_r_k[0], rwkv_gn_w[0], rwkv_gn_b[0])
    y_b = _dsa(z, B, L, dsa_ckv_norm[0], dsa_w_uk[0], dsa_w_uv[0], rel_bias)
    xf = _mm_res([y_a, y_b], bf(ab_w_out[0]), xf, tm=1024, tn=1024)
    xf = _ffn(xf, norm_ffn[0], bf(ffn_w_gate[0]), bf(ffn_w_up[0]), bf(ffn_w_down[0]), tm=1024, tf=512)
    xf = _ple(xf, ple_norm[0], bf(ple_w_gate[0]), p[0].reshape(T, PLE_DIM), bf(ple_w_proj[0]),
              final_norm, final=False, tm=512)

    z = _norm_mm(xf, norm_mix[1], _gd_in_weight(gdn_w_in[0]), tm=1024, tn=1280)
    gates = _gdn_gates(z, B, L, gdn_a_log[0], gdn_dt_bias[0])
    o = _gdn(z, gates, B, L, gdn_conv[0], gdn_out_norm[0])
    xf = _mm_res([o], bf(gdn_w_out[0]), xf, tm=512, tn=1024)
    y, route, slots = _moe(xf, norm_ffn[1], moe_router[0], bf(moe_w_gate[0]), bf(moe_w_up[0]),
                           bf(moe_w_down[0]))
    xf = _combine_ple(xf, y, route, slots, ple_norm[1], bf(ple_w_gate[1]), p[1].reshape(T, PLE_DIM),
                      bf(ple_w_proj[1]), final_norm, final=True, tm=256)
    return xf.reshape(B, L, D)
```
